```python
import math
import jax, jax.numpy as jnp
from jax import lax
import numpy as np

D_MODEL = 1024
BATCH = 2
SEQ = 8192
DEPTH = 2

N_MEM = 256
CONV_WIDTH = D_MODEL // 2
CONV_KSIZE = 31
ATT_PATTERNS = ((128, 1), (512, 4), (2048, 16))
N_GROUPS = len(ATT_PATTERNS)
HEADS_PER_GROUP = 4
ATT_HEAD_DIM = D_MODEL // 16
N_ATT_HEADS = N_GROUPS * HEADS_PER_GROUP
ATT_WIDTH = N_ATT_HEADS * ATT_HEAD_DIM
ATT_OUT_WIDTH = HEADS_PER_GROUP * ATT_HEAD_DIM
MEM_HEADS = 4
MEM_HEAD_DIM = D_MODEL // 8
MEM_WIDTH = MEM_HEADS * MEM_HEAD_DIM
N_BRANCHES = 3
IN_WIDTH = 2 * CONV_WIDTH + 3 * ATT_WIDTH + MEM_WIDTH + N_BRANCHES * D_MODEL
FFN_HIDDEN = -(-(8 * D_MODEL) // (3 * 256)) * 256
NUM_BUCKETS = 32
MAX_DISTANCE = 1024
RMS_EPS = 1e-6
LN_EPS = 1e-5
NEG_INF = -1e30

kernel_name = "hybrid_gated_conv_dilated_memory_encoder"


def _rms_norm(x, g):
    xf = x.astype(jnp.float32)
    y = xf * lax.rsqrt(jnp.mean(xf * xf, axis=-1, keepdims=True) + RMS_EPS)
    return (y * g.astype(jnp.float32)).astype(x.dtype)


def _layer_norm(x, g, b):
    xf = x.astype(jnp.float32)
    mu = jnp.mean(xf, axis=-1, keepdims=True)
    xc = xf - mu
    y = xc * lax.rsqrt(jnp.mean(xc * xc, axis=-1, keepdims=True) + LN_EPS)
    return (y * g.astype(jnp.float32) + b.astype(jnp.float32)).astype(x.dtype)


def _t5_bucket(rel):
    nb = NUM_BUCKETS // 2
    max_exact = nb // 2
    ret = jnp.where(rel > 0, nb, 0)
    n = jnp.abs(rel)
    nf = jnp.maximum(n, 1).astype(jnp.float32)
    large = max_exact + (jnp.log(nf / max_exact) / math.log(MAX_DISTANCE / max_exact)
                         * (nb - max_exact)).astype(jnp.int32)
    large = jnp.minimum(large, nb - 1)
    return ret + jnp.where(n < max_exact, n, large)


def _dilated_group(q, k, v, bias_tab, dilation, radius):
    B, S, H, E = q.shape
    blk = radius
    unit = dilation * blk
    sp = -(-S // unit) * unit
    L = sp // dilation
    nb = L // blk

    def to_blocks(t):
        t = jnp.pad(t, ((0, 0), (0, sp - S), (0, 0), (0, 0)))
        t = t.reshape(B, L, dilation, H, E).transpose(0, 2, 1, 3, 4)
        return t.reshape(B, dilation, nb, blk, H, E)

    def band_keys(t):
        tp = jnp.pad(t, ((0, 0), (0, 0), (1, 1), (0, 0), (0, 0), (0, 0)))
        return jnp.concatenate([tp[:, :, :-2], tp[:, :, 1:-1], tp[:, :, 2:]], axis=3)

    qb = to_blocks(q)
    kb = band_keys(to_blocks(k))
    vb = band_keys(to_blocks(v))
    valid = (jnp.arange(sp) < S).reshape(L, dilation).T.reshape(dilation, nb, blk)
    vp = jnp.pad(valid, ((0, 0), (1, 1), (0, 0)))
    kvalid = jnp.concatenate([vp[:, :-2], vp[:, 1:-1], vp[:, 2:]], axis=2)

    s = jnp.einsum('brnqhe,brnkhe->brnhqk', qb, kb).astype(jnp.float32) * (E ** -0.5)
    off = jnp.arange(3 * blk)[None, :] - blk - jnp.arange(blk)[:, None]
    band = jnp.abs(off) <= radius
    bias = bias_tab[_t5_bucket(off * dilation)].transpose(2, 0, 1).astype(jnp.float32)
    mask = band & kvalid[:, :, None, None, :]
    s = jnp.where(mask, s + bias, NEG_INF)
    m = jnp.max(s, axis=-1, keepdims=True)
    e = jnp.exp(s - m)
    den = jnp.sum(e, axis=-1, keepdims=True)
    lse = (m + jnp.log(den))[..., 0]
    o = jnp.einsum('brnhqk,brnkhe->brnqhe', (e / den).astype(v.dtype), vb)

    def from_blocks(t):
        tail = t.shape[4:]
        t = t.reshape(B, dilation, L, *tail).swapaxes(1, 2).reshape(B, sp, *tail)
        return t[:, :S]

    return from_blocks(o), from_blocks(lse.transpose(0, 1, 2, 4, 3))


def _dilated_attention(att_in, rel_bias):
    B, S, _ = att_in.shape
    qkv = att_in.reshape(B, S, 3, N_GROUPS, HEADS_PER_GROUP, ATT_HEAD_DIM)
    outs, lses = [], []
    for g, (window, dilation) in enumerate(ATT_PATTERNS):
        radius = window // (2 * dilation)
        tab = rel_bias[:, g * HEADS_PER_GROUP:(g + 1) * HEADS_PER_GROUP]
        o, l = _dilated_group(qkv[:, :, 0, g], qkv[:, :, 1, g], qkv[:, :, 2, g], tab, dilation, radius)
        outs.append(o)
        lses.append(l)
    w = jax.nn.softmax(jnp.stack(lses, axis=0), axis=0)
    o = jnp.sum(w[..., None].astype(outs[0].dtype) * jnp.stack(outs, axis=0), axis=0)
    return o.reshape(B, S, ATT_OUT_WIDTH)


def _conv_module(u, w_dw, b_dw, ln_g, ln_b, w_o):
    a, gt = jnp.split(u, 2, axis=-1)
    u = a * jax.nn.sigmoid(gt)
    pad = CONV_KSIZE // 2
    y = lax.conv_general_dilated(u, w_dw[:, None, :], window_strides=(1,), padding=((pad, pad),),
                                 dimension_numbers=('NWC', 'WIO', 'NWC'),
                                 feature_group_count=CONV_WIDTH) + b_dw
    y = jax.nn.silu(_layer_norm(y, ln_g, ln_b))
    return y @ w_o


def _memory_attention(q_in, mem, g_mem, w_kv, w_o):
    B, S, _ = q_in.shape
    M = mem.shape[1]
    q = q_in.reshape(B, S, MEM_HEADS, MEM_HEAD_DIM)
    kv = (_rms_norm(mem, g_mem) @ w_kv).reshape(B, M, 2, MEM_HEADS, MEM_HEAD_DIM)
    k, v = kv[:, :, 0], kv[:, :, 1]
    s = jnp.einsum('bshe,bmhe->bhsm', q, k).astype(jnp.float32) * (MEM_HEAD_DIM ** -0.5)
    p = jax.nn.softmax(s, axis=-1).astype(v.dtype)
    o = jnp.einsum('bhsm,bmhe->bshe', p, v).reshape(B, S, MEM_WIDTH)
    return o @ w_o


def setup_inputs(seed: int = 0) -> dict:
    key = jax.random.key(seed)
    ks = jax.random.split(key, 24)
    f32 = jnp.float32

    def nrm(k, shape, scale):
        return jax.random.normal(k, shape, f32) * scale

    def gain(k, shape):
        return 1.0 + 0.05 * jax.random.normal(k, shape, f32)

    return {
        "x": jax.random.normal(ks[0], (BATCH, SEQ, D_MODEL), f32),
        "mem": jax.random.normal(ks[1], (BATCH, N_MEM, D_MODEL), f32),
        "rel_bias": nrm(ks[2], (NUM_BUCKETS, N_ATT_HEADS), 0.5),
        "norm_mix_pre": gain(ks[3], (DEPTH, D_MODEL)),
        "w_in": nrm(ks[4], (DEPTH, D_MODEL, IN_WIDTH), D_MODEL ** -0.5),
        "b_gate": nrm(ks[5], (DEPTH, N_BRANCHES * D_MODEL), 0.01),
        "conv_dw": nrm(ks[6], (DEPTH, CONV_KSIZE, CONV_WIDTH), CONV_KSIZE ** -0.5),
        "conv_dw_bias": nrm(ks[7], (DEPTH, CONV_WIDTH), 0.02),
        "conv_ln_g": gain(ks[8], (DEPTH, CONV_WIDTH)),
        "conv_ln_b": nrm(ks[9], (DEPTH, CONV_WIDTH), 0.02),
        "w_conv_out": nrm(ks[10], (DEPTH, CONV_WIDTH, D_MODEL), CONV_WIDTH ** -0.5),
        "w_att_out": nrm(ks[11], (DEPTH, ATT_OUT_WIDTH, D_MODEL), ATT_OUT_WIDTH ** -0.5),
        "norm_mem": gain(ks[12], (DEPTH, D_MODEL)),
        "w_mem_kv": nrm(ks[13], (DEPTH, D_MODEL, 2 * MEM_WIDTH), D_MODEL ** -0.5),
        "w_mem_out": nrm(ks[14], (DEPTH, MEM_WIDTH, D_MODEL), MEM_WIDTH ** -0.5),
        "w_out": nrm(ks[15], (DEPTH, D_MODEL, D_MODEL), D_MODEL ** -0.5),
        "norm_mix_post": gain(ks[16], (DEPTH, D_MODEL)),
        "norm_ffn_pre": gain(ks[17], (DEPTH, D_MODEL)),
        "w_ffn_in": nrm(ks[18], (DEPTH, D_MODEL, 2 * FFN_HIDDEN), D_MODEL ** -0.5),
        "w_ffn_out": nrm(ks[19], (DEPTH, FFN_HIDDEN, D_MODEL), FFN_HIDDEN ** -0.5),
        "norm_ffn_post": gain(ks[20], (DEPTH, D_MODEL)),
    }


def reference(x, mem, rel_bias, norm_mix_pre, w_in, b_gate, conv_dw, conv_dw_bias, conv_ln_g,
              conv_ln_b, w_conv_out, w_att_out, norm_mem, w_mem_kv, w_mem_out, w_out,
              norm_mix_post, norm_ffn_pre, w_ffn_in, w_ffn_out, norm_ffn_post):
    B, S, _ = x.shape
    c1 = 2 * CONV_WIDTH
    c2 = c1 + 3 * ATT_WIDTH
    c3 = c2 + MEM_WIDTH
    for l in range(DEPTH):
        h = _rms_norm(x, norm_mix_pre[l])
        z = h @ w_in[l]
        y_conv = _conv_module(z[..., :c1], conv_dw[l], conv_dw_bias[l], conv_ln_g[l],
                              conv_ln_b[l], w_conv_out[l])
        y_att = _dilated_attention(z[..., c1:c2], rel_bias) @ w_att_out[l]
        y_mem = _memory_attention(z[..., c2:c3], mem, norm_mem[l], w_mem_kv[l], w_mem_out[l])
        gates = jax.nn.sigmoid(z[..., c3:] + b_gate[l]).reshape(B, S, N_BRANCHES, D_MODEL)
        merged = gates[:, :, 0] * y_conv + gates[:, :, 1] * y_att + gates[:, :, 2] * y_mem
        x = x + _rms_norm(merged @ w_out[l], norm_mix_post[l])
        h = _rms_norm(x, norm_ffn_pre[l])
        gu = h @ w_ffn_in[l]
        g_ff, u_ff = gu[..., :FFN_HIDDEN], gu[..., FFN_HIDDEN:]
        x = x + _rms_norm((jax.nn.silu(g_ff) * u_ff) @ w_ffn_out[l], norm_ffn_post[l])
    return x
```

```python
import functools
import math

import jax
import jax.numpy as jnp
from jax import lax
from jax.experimental import pallas as pl
from jax.experimental.pallas import tpu as pltpu

F32 = jnp.float32
BF16 = jnp.bfloat16

D_MODEL = 1024
CONV_WIDTH = 512
CONV_KSIZE = 31
CONV_PAD = CONV_KSIZE // 2
ATT_PATTERNS = ((128, 1), (512, 4), (2048, 16))
N_GROUPS = 3
HEADS = 4
HEAD_DIM = 64
ATT_OUT = HEADS * HEAD_DIM
MEM_HEADS = 4
MEM_HEAD_DIM = 128
MEM_WIDTH = 512
FFN_HIDDEN = 2816
NUM_BUCKETS = 32
MAX_DISTANCE = 1024
RMS_EPS = 1e-6
LN_EPS = 1e-5
NEG_INF = -1e30

MAIN_WIDTH = 2 * CONV_WIDTH + MEM_WIDTH + 3 * N_GROUPS * ATT_OUT
GATE_WIDTH = 3 * D_MODEL
COLB = 256
MAIN_COLB = MAIN_WIDTH // COLB
Q_COLB, K_COLB, V_COLB = 6, 9, 12
RADIUS = 64
SUBQ = 128
SUBK = SUBQ + 2 * RADIUS
LSE_LANES = 128
LSE_PER_HEAD = LSE_LANES // HEADS

VMEM_LIMIT = 48 * 1024 * 1024


def _sigmoid(v):
    return 1.0 / (1.0 + jnp.exp(-v))


def _rms(v, g):
    return v * lax.rsqrt(jnp.mean(v * v, axis=-1, keepdims=True) + RMS_EPS) * g


IN_TM = 1024
IN_TN = 768
N_MAIN_TILES = MAIN_WIDTH // IN_TN
N_GATE_TILES = GATE_WIDTH // IN_TN


def _in_proj_kernel(x_ref, g_ref, w_ref, b_ref, zm_ref, gate_ref, h_ref):
    j = pl.program_id(1)

    @pl.when(j == 0)
    def _():
        h_ref[...] = _rms(x_ref[...], g_ref[...]).astype(BF16)

    acc = jnp.dot(h_ref[...], w_ref[...], preferred_element_type=F32)

    @pl.when(j < N_MAIN_TILES)
    def _():
        zm_ref[...] = acc.astype(BF16)

    @pl.when(j >= N_MAIN_TILES)
    def _():
        gate_ref[...] = _sigmoid(acc + b_ref[...]).astype(BF16)


def _in_proj(x2, g, w, b):
    T = x2.shape[0]
    return pl.pallas_call(
        _in_proj_kernel,
        grid=(T // IN_TM, N_MAIN_TILES + N_GATE_TILES),
        in_specs=[
            pl.BlockSpec((IN_TM, D_MODEL), lambda i, j: (i, 0)),
            pl.BlockSpec((1, D_MODEL), lambda i, j: (0, 0)),
            pl.BlockSpec((D_MODEL, IN_TN), lambda i, j: (0, j)),
            pl.BlockSpec((1, IN_TN), lambda i, j: (0, jnp.maximum(j - N_MAIN_TILES, 0))),
        ],
        out_specs=[
            pl.BlockSpec((IN_TM, IN_TN), lambda i, j: (i, jnp.minimum(j, N_MAIN_TILES - 1))),
            pl.BlockSpec((IN_TM, IN_TN), lambda i, j: (i, jnp.maximum(j - N_MAIN_TILES, 0))),
        ],
        out_shape=[jax.ShapeDtypeStruct((T, MAIN_WIDTH), BF16),
                   jax.ShapeDtypeStruct((T, GATE_WIDTH), BF16)],
        scratch_shapes=[pltpu.VMEM((IN_TM, D_MODEL), BF16)],
        compiler_params=pltpu.CompilerParams(
            dimension_semantics=("parallel", "arbitrary"), vmem_limit_bytes=VMEM_LIMIT),
        name="in_proj",
    )(x2, g, w, b)


CONV_TT = 256
CONV_HALO = 16
CONV_RC = 64


def _conv_kernel(a_ref, gt_ref, ap_ref, gp_ref, an_ref, gn_ref, w_ref, b_ref, lg_ref, lb_ref,
                 out_ref, u_ref):
    i = pl.program_id(1)
    n = pl.num_programs(1)

    def glu(a, g):
        return a.astype(F32) * _sigmoid(g.astype(F32))

    u_ref[CONV_HALO:CONV_HALO + CONV_TT, :] = glu(a_ref[0], gt_ref[0])
    u_ref[0:CONV_HALO, :] = jnp.where(i > 0, glu(ap_ref[0], gp_ref[0]), 0.0)
    u_ref[CONV_HALO + CONV_TT:, :] = jnp.where(i < n - 1, glu(an_ref[0], gn_ref[0]), 0.0)

    first = CONV_HALO - CONV_PAD
    for c in range(CONV_TT // CONV_RC):
        r0 = c * CONV_RC + first
        acc = jnp.zeros((CONV_RC, CONV_WIDTH), F32)
        for k in range(CONV_KSIZE):
            acc = acc + u_ref[r0 + k:r0 + k + CONV_RC, :] * w_ref[k:k + 1, :]
        y = acc + b_ref[...]
        mu = jnp.mean(y, axis=-1, keepdims=True)
        yc = y - mu
        yn = yc * lax.rsqrt(jnp.mean(yc * yc, axis=-1, keepdims=True) + LN_EPS)
        yn = yn * lg_ref[...] + lb_ref[...]
        out_ref[0, c * CONV_RC:(c + 1) * CONV_RC, :] = (yn * _sigmoid(yn)).astype(BF16)


def _conv(zm3, w_dw, b_dw, ln_g, ln_b):
    B, S, _ = zm3.shape
    nh = CONV_TT // CONV_HALO
    last_h = S // CONV_HALO - 1

    def cur(col):
        return pl.BlockSpec((1, CONV_TT, CONV_WIDTH), lambda b, i: (b, i, col))

    def prev(col):
        return pl.BlockSpec((1, CONV_HALO, CONV_WIDTH),
                            lambda b, i: (b, jnp.maximum(i * nh - 1, 0), col))

    def nxt(col):
        return pl.BlockSpec((1, CONV_HALO, CONV_WIDTH),
                            lambda b, i: (b, jnp.minimum((i + 1) * nh, last_h), col))

    def vec(rows):
        return pl.BlockSpec((rows, CONV_WIDTH), lambda b, i: (0, 0))

    return pl.pallas_call(
        _conv_kernel,
        grid=(B, S // CONV_TT),
        in_specs=[cur(0), cur(1), prev(0), prev(1), nxt(0), nxt(1),
                  vec(CONV_KSIZE), vec(1), vec(1), vec(1)],
        out_specs=pl.BlockSpec((1, CONV_TT, CONV_WIDTH), lambda b, i: (b, i, 0)),
        out_shape=jax.ShapeDtypeStruct((B, S, CONV_WIDTH), BF16),
        scratch_shapes=[pltpu.VMEM((CONV_TT + 2 * CONV_HALO, CONV_WIDTH), F32)],
        compiler_params=pltpu.CompilerParams(
            dimension_semantics=("parallel", "parallel"), vmem_limit_bytes=VMEM_LIMIT),
        name="conv_module",
    )(zm3, zm3, zm3, zm3, zm3, zm3, w_dw, b_dw, ln_g, ln_b)


def _att_kernel(q_ref, kc_ref, vc_ref, kp_ref, vp_ref, kn_ref, vn_ref, bias_ref,
                o_ref, lse_ref, kbuf, vbuf, *, tq):
    i = pl.program_id(2)
    n = pl.num_programs(2)
    kbuf[0:RADIUS, :] = kp_ref[0]
    kbuf[RADIUS:RADIUS + tq, :] = kc_ref[0]
    kbuf[RADIUS + tq:, :] = kn_ref[0]
    vbuf[0:RADIUS, :] = vp_ref[0]
    vbuf[RADIUS:RADIUS + tq, :] = vc_ref[0]
    vbuf[RADIUS + tq:, :] = vn_ref[0]

    head_of_lane = lax.broadcasted_iota(jnp.int32, (1, ATT_OUT), 1) // HEAD_DIM
    lse_head_of_lane = lax.broadcasted_iota(jnp.int32, (1, LSE_LANES), 1) // LSE_PER_HEAD
    key_col = lax.broadcasted_iota(jnp.int32, (SUBQ, SUBK), 1)
    nsub = tq // SUBQ
    scale = HEAD_DIM ** -0.5
    for j in range(nsub):
        q = q_ref[0, j * SUBQ:(j + 1) * SUBQ, :]
        k = kbuf[j * SUBQ:j * SUBQ + SUBK, :]
        v = vbuf[j * SUBQ:j * SUBQ + SUBK, :]
        o_acc = jnp.zeros((SUBQ, ATT_OUT), F32)
        lse_tile = jnp.zeros((SUBQ, LSE_LANES), F32)
        for h in range(HEADS):
            sel = head_of_lane == h
            qh = jnp.where(sel, q, jnp.zeros_like(q)) * jnp.asarray(scale, BF16)
            s = lax.dot_general(qh, k, (((1,), (1,)), ((), ())), preferred_element_type=F32)
            s = s + bias_ref[h]
            if j == 0:
                s = jnp.where(key_col >= jnp.where(i == 0, RADIUS, 0), s, NEG_INF)
            if j == nsub - 1:
                s = jnp.where(key_col < jnp.where(i == n - 1, SUBK - RADIUS, SUBK), s, NEG_INF)
            m = jnp.max(s, axis=-1, keepdims=True)
            e = jnp.exp(s - m)
            l = jnp.sum(e, axis=-1, keepdims=True)
            p = (e * (1.0 / l)).astype(BF16)
            vh = jnp.where(sel, v, jnp.zeros_like(v))
            o_acc = o_acc + jnp.dot(p, vh, preferred_element_type=F32)
            lse_tile = jnp.where(lse_head_of_lane == h, m + jnp.log(l), lse_tile)
        o_ref[0, j * SUBQ:(j + 1) * SUBQ, :] = o_acc.astype(BF16)
        lse_ref[0, j * SUBQ:(j + 1) * SUBQ, :] = lse_tile


def _attention_group(zm3, bias, g, dilation, tq):
    B, S, _ = zm3.shape
    L = S // dilation
    zv = zm3.reshape(B, L, dilation * MAIN_WIDTH)
    nb = tq // RADIUS
    last_b = L // RADIUS - 1

    def cur(colb):
        return pl.BlockSpec((1, tq, COLB), lambda b, r, i: (b, i, r * MAIN_COLB + colb + g))

    def prev(colb):
        return pl.BlockSpec((1, RADIUS, COLB),
                            lambda b, r, i: (b, jnp.maximum(i * nb - 1, 0), r * MAIN_COLB + colb + g))

    def nxt(colb):
        return pl.BlockSpec((1, RADIUS, COLB),
                            lambda b, r, i: (b, jnp.minimum((i + 1) * nb, last_b), r * MAIN_COLB + colb + g))

    o, lse = pl.pallas_call(
        functools.partial(_att_kernel, tq=tq),
        grid=(B, dilation, L // tq),
        in_specs=[cur(Q_COLB), cur(K_COLB), cur(V_COLB), prev(K_COLB), prev(V_COLB),
                  nxt(K_COLB), nxt(V_COLB),
                  pl.BlockSpec((HEADS, SUBQ, SUBK), lambda b, r, i: (0, 0, 0))],
        out_specs=[pl.BlockSpec((1, tq, ATT_OUT), lambda b, r, i: (b, i, r)),
                   pl.BlockSpec((1, tq, LSE_LANES), lambda b, r, i: (b, i, r))],
        out_shape=[jax.ShapeDtypeStruct((B, L, dilation * ATT_OUT), BF16),
                   jax.ShapeDtypeStruct((B, L, dilation * LSE_LANES), F32)],
        scratch_shapes=[pltpu.VMEM((tq + 2 * RADIUS, COLB), BF16),
                        pltpu.VMEM((tq + 2 * RADIUS, COLB), BF16)],
        compiler_params=pltpu.CompilerParams(
            dimension_semantics=("parallel", "parallel", "parallel"), vmem_limit_bytes=VMEM_LIMIT),
        name=f"dilated_att_g{g}",
    )(zv, zv, zv, zv, zv, zv, zv, bias)
    return o.reshape(B * S, ATT_OUT), lse.reshape(B * S, LSE_LANES)


def _t5_bucket_of(rel):
    nb = NUM_BUCKETS // 2
    max_exact = nb // 2
    ret = jnp.where(rel > 0, nb, 0)
    n = jnp.abs(rel)
    nf = jnp.maximum(n, 1).astype(F32)
    large = max_exact + (jnp.log(nf / max_exact) / math.log(MAX_DISTANCE / max_exact)
                         * (nb - max_exact)).astype(jnp.int32)
    large = jnp.minimum(large, nb - 1)
    return ret + jnp.where(n < max_exact, n, large)


def _band_bias(rel_bias, g, dilation):
    off = jnp.arange(SUBK)[None, :] - RADIUS - jnp.arange(SUBQ)[:, None]
    tab = rel_bias[:, g * HEADS:(g + 1) * HEADS]
    bias = tab[_t5_bucket_of(off * dilation)].transpose(2, 0, 1).astype(F32)
    return jnp.where((jnp.abs(off) <= RADIUS)[None], bias, NEG_INF)


def _mem_kv_kernel(mem_ref, g_ref, w_ref, kv_ref):
    h = _rms(mem_ref[0], g_ref[...]).astype(BF16)
    kv_ref[0] = jnp.dot(h, w_ref[...], preferred_element_type=F32).astype(BF16)


def _mem_kv(mem, g, w):
    B, M, _ = mem.shape
    return pl.pallas_call(
        _mem_kv_kernel,
        grid=(B,),
        in_specs=[pl.BlockSpec((1, M, D_MODEL), lambda b: (b, 0, 0)),
                  pl.BlockSpec((1, D_MODEL), lambda b: (0, 0)),
                  pl.BlockSpec((D_MODEL, 2 * MEM_WIDTH), lambda b: (0, 0))],
        out_specs=pl.BlockSpec((1, M, 2 * MEM_WIDTH), lambda b: (b, 0, 0)),
        out_shape=jax.ShapeDtypeStruct((B, M, 2 * MEM_WIDTH), BF16),
        compiler_params=pltpu.CompilerParams(
            dimension_semantics=("parallel",), vmem_limit_bytes=VMEM_LIMIT),
        name="mem_kv",
    )(mem, g, w)


MIX_TM = 512


def _mix_kernel(x_ref, c_ref, qm_ref, kv_ref, o0_ref, o1_ref, o2_ref, l0_ref, l1_ref, l2_ref,
                gate_ref, wc_ref, wa_ref, wm_ref, wo_ref, gpost_ref, out_ref):
    tm = x_ref.shape[0]
    scale = MEM_HEAD_DIM ** -0.5
    heads = []
    for h in range(MEM_HEADS):
        lo = h * MEM_HEAD_DIM
        qh = qm_ref[:, lo:lo + MEM_HEAD_DIM]
        kh = kv_ref[0, :, lo:lo + MEM_HEAD_DIM]
        vh = kv_ref[0, :, MEM_WIDTH + lo:MEM_WIDTH + lo + MEM_HEAD_DIM]
        s = lax.dot_general(qh, kh, (((1,), (1,)), ((), ())), preferred_element_type=F32) * scale
        m = jnp.max(s, axis=-1, keepdims=True)
        e = jnp.exp(s - m)
        p = (e * (1.0 / jnp.sum(e, axis=-1, keepdims=True))).astype(BF16)
        heads.append(jnp.dot(p, vh, preferred_element_type=F32).astype(BF16))
    o_mem = jnp.concatenate(heads, axis=-1)
    y_mem = jnp.dot(o_mem, wm_ref[...], preferred_element_type=F32)

    y_conv = jnp.dot(c_ref[...], wc_ref[...], preferred_element_type=F32)

    l0, l1, l2 = l0_ref[...], l1_ref[...], l2_ref[...]
    mx = jnp.maximum(jnp.maximum(l0, l1), l2)
    e0, e1, e2 = jnp.exp(l0 - mx), jnp.exp(l1 - mx), jnp.exp(l2 - mx)
    inv = 1.0 / (e0 + e1 + e2)
    head_of_lane = lax.broadcasted_iota(jnp.int32, (1, ATT_OUT), 1) // HEAD_DIM

    def expand(w):
        full = jnp.broadcast_to(w[:, 0:1], (tm, ATT_OUT))
        for h in range(1, HEADS):
            col = w[:, h * LSE_PER_HEAD:h * LSE_PER_HEAD + 1]
            full = jnp.where(head_of_lane == h, jnp.broadcast_to(col, (tm, ATT_OUT)), full)
        return full

    o_att = (expand(e0 * inv) * o0_ref[...].astype(F32)
             + expand(e1 * inv) * o1_ref[...].astype(F32)
             + expand(e2 * inv) * o2_ref[...].astype(F32))
    y_att = jnp.dot(o_att.astype(BF16), wa_ref[...], preferred_element_type=F32)

    merged = (gate_ref[:, 0:D_MODEL].astype(F32) * y_conv
              + gate_ref[:, D_MODEL:2 * D_MODEL].astype(F32) * y_att
              + gate_ref[:, 2 * D_MODEL:].astype(F32) * y_mem)
    y = jnp.dot(merged.astype(BF16), wo_ref[...], preferred_element_type=F32)
    out_ref[...] = x_ref[...] + _rms(y, gpost_ref[...])


def _mix(x2, c2, zm, kv, o_list, lse_list, gates, wc, wa, wm, wo, gpost, tiles_per_batch):
    T = x2.shape[0]
    M = kv.shape[1]

    def rows(width, col=0):
        return pl.BlockSpec((MIX_TM, width), lambda i: (i, col))

    def whole(shape):
        return pl.BlockSpec(shape, lambda i: (0,) * len(shape))

    return pl.pallas_call(
        _mix_kernel,
        grid=(T // MIX_TM,),
        in_specs=[rows(D_MODEL), rows(CONV_WIDTH),
                  rows(MEM_WIDTH, 2 * CONV_WIDTH // MEM_WIDTH),
                  pl.BlockSpec((1, M, 2 * MEM_WIDTH), lambda i: (i // tiles_per_batch, 0, 0)),
                  rows(ATT_OUT), rows(ATT_OUT), rows(ATT_OUT),
                  rows(LSE_LANES), rows(LSE_LANES), rows(LSE_LANES),
                  rows(GATE_WIDTH),
                  whole((CONV_WIDTH, D_MODEL)), whole((ATT_OUT, D_MODEL)),
                  whole((MEM_WIDTH, D_MODEL)), whole((D_MODEL, D_MODEL)), whole((1, D_MODEL))],
        out_specs=rows(D_MODEL),
        out_shape=jax.ShapeDtypeStruct((T, D_MODEL), F32),
        compiler_params=pltpu.CompilerParams(
            dimension_semantics=("parallel",), vmem_limit_bytes=VMEM_LIMIT),
        name="mix_out",
    )(x2, c2, zm, kv, *o_list, *lse_list, gates, wc, wa, wm, wo, gpost)


FFN_TM = 1024
FFN_TH = 256
FFN_CHUNKS = FFN_HIDDEN // FFN_TH


def _ffn_kernel(x_ref, gpre_ref, wg_ref, wu_ref, wd_ref, gpost_ref, out_ref, h_ref, acc_ref):
    j = pl.program_id(1)

    @pl.when(j == 0)
    def _():
        h_ref[...] = _rms(x_ref[...], gpre_ref[...]).astype(BF16)
        acc_ref[...] = jnp.zeros_like(acc_ref)

    h = h_ref[...]
    gv = jnp.dot(h, wg_ref[...], preferred_element_type=F32)
    uv = jnp.dot(h, wu_ref[...], preferred_element_type=F32)
    a = (gv * _sigmoid(gv) * uv).astype(BF16)
    acc_ref[...] += jnp.dot(a, wd_ref[...], preferred_element_type=F32)

    @pl.when(j == FFN_CHUNKS - 1)
    def _():
        out_ref[...] = x_ref[...] + _rms(acc_ref[...], gpost_ref[...])


def _ffn(x2, gpre, w_in, w_out, gpost):
    T = x2.shape[0]
    return pl.pallas_call(
        _ffn_kernel,
        grid=(T // FFN_TM, FFN_CHUNKS),
        in_specs=[pl.BlockSpec((FFN_TM, D_MODEL), lambda i, j: (i, 0)),
                  pl.BlockSpec((1, D_MODEL), lambda i, j: (0, 0)),
                  pl.BlockSpec((D_MODEL, FFN_TH), lambda i, j: (0, j)),
                  pl.BlockSpec((D_MODEL, FFN_TH), lambda i, j: (0, FFN_CHUNKS + j)),
                  pl.BlockSpec((FFN_TH, D_MODEL), lambda i, j: (j, 0)),
                  pl.BlockSpec((1, D_MODEL), lambda i, j: (0, 0))],
        out_specs=pl.BlockSpec((FFN_TM, D_MODEL), lambda i, j: (i, 0)),
        out_shape=jax.ShapeDtypeStruct((T, D_MODEL), F32),
        scratch_shapes=[pltpu.VMEM((FFN_TM, D_MODEL), BF16), pltpu.VMEM((FFN_TM, D_MODEL), F32)],
        compiler_params=pltpu.CompilerParams(
            dimension_semantics=("parallel", "arbitrary"), vmem_limit_bytes=VMEM_LIMIT),
        name="ffn",
    )(x2, gpre, w_in, w_in, w_out, gpost)


ATT_TQ = (1024, 1024, 512)


def kernel(x, mem, rel_bias, norm_mix_pre, w_in, b_gate, conv_dw, conv_dw_bias, conv_ln_g, conv_ln_b,
           w_conv_out, w_att_out, norm_mem, w_mem_kv, w_mem_out, w_out, norm_mix_post, norm_ffn_pre,
           w_ffn_in, w_ffn_out, norm_ffn_post):
    B, S, D = x.shape
    depth = w_in.shape[0]
    T = B * S
    c1 = 2 * CONV_WIDTH
    c2 = c1 + 3 * N_GROUPS * ATT_OUT
    c3 = c2 + MEM_WIDTH
    biases = [_band_bias(rel_bias, g, d) for g, (_, d) in enumerate(ATT_PATTERNS)]

    def row(v):
        return v.reshape(1, -1)

    x2 = x.reshape(T, D)
    for l in range(depth):
        wl = w_in[l]
        w_perm = jnp.concatenate([wl[:, :c1], wl[:, c2:c3], wl[:, c1:c2], wl[:, c3:]], axis=1).astype(BF16)
        zm, gates = _in_proj(x2, row(norm_mix_pre[l]), w_perm, row(b_gate[l]))
        zm3 = zm.reshape(B, S, MAIN_WIDTH)
        c = _conv(zm3, conv_dw[l], row(conv_dw_bias[l]), row(conv_ln_g[l]), row(conv_ln_b[l]))
        o_list, lse_list = [], []
        for g, (_, d) in enumerate(ATT_PATTERNS):
            o, lse = _attention_group(zm3, biases[g], g, d, ATT_TQ[g])
            o_list.append(o)
            lse_list.append(lse)
        kv = _mem_kv(mem, row(norm_mem[l]), w_mem_kv[l].astype(BF16))
        x2 = _mix(x2, c.reshape(T, CONV_WIDTH), zm, kv, o_list, lse_list, gates,
                  w_conv_out[l].astype(BF16), w_att_out[l].astype(BF16), w_mem_out[l].astype(BF16),
                  w_out[l].astype(BF16), row(norm_mix_post[l]), S // MIX_TM)
        x2 = _ffn(x2, row(norm_ffn_pre[l]), w_ffn_in[l].astype(BF16), w_ffn_out[l].astype(BF16),
                  row(norm_ffn_post[l]))
    return x2.reshape(B, S, D)
```

```python
import functools
import math

import numpy as np
import jax
import jax.numpy as jnp
from jax import lax
from jax.experimental import pallas as pl
from jax.experimental.pallas import tpu as pltpu

F32 = jnp.float32
BF16 = jnp.bfloat16

D_MODEL = 1024
CONV_WIDTH = 512
CONV_KSIZE = 31
CONV_PAD = CONV_KSIZE // 2
DILATIONS = (1, 4, 16)
RADIUS = 64
N_GROUPS = 3
HEADS = 4
HEAD_DIM = 64
ATT_OUT = HEADS * HEAD_DIM
QKV_WIDTH = 3 * ATT_OUT
MEM_HEADS = 4
MEM_HEAD_DIM = 128
MEM_WIDTH = 512
FFN_HIDDEN = 2816
NUM_BUCKETS = 32
MAX_DISTANCE = 1024
RMS_EPS = 1e-6
LN_EPS = 1e-5
NEG_INF = -1e30

MAIN_WIDTH = 2 * CONV_WIDTH + MEM_WIDTH
GATE_WIDTH = 3 * D_MODEL
SUBQ = 128
SUBK = SUBQ + 2 * RADIUS
LANES = 128
LSE_LANES = 128
LSE_PER_HEAD = LSE_LANES // HEADS

VMEM_LIMIT = 56 * 1024 * 1024


def _sigmoid(v):
    return 1.0 / (1.0 + jnp.exp(-v))


def _rms(v, g):
    return v * lax.rsqrt(jnp.mean(v * v, axis=-1, keepdims=True) + RMS_EPS) * g


IN_TM = 1024
IN_TN = 768
N_MAIN_TILES = MAIN_WIDTH // IN_TN
N_GATE_TILES = GATE_WIDTH // IN_TN
FIRST_GATE_TILE = N_MAIN_TILES + N_GROUPS
assert IN_TN == QKV_WIDTH


def _in_proj_kernel(x_ref, g_ref, w_ref, b_ref, main_ref, q0_ref, q1_ref, q2_ref, gate_ref,
                    h_ref, acc_ref):
    j = pl.program_id(1)

    @pl.when(j == 0)
    def _():
        h_ref[...] = _rms(x_ref[...], g_ref[...]).astype(BF16)

    acc = jnp.dot(h_ref[...], w_ref[...], preferred_element_type=F32)

    @pl.when(j < N_MAIN_TILES)
    def _():
        main_ref[...] = acc.astype(BF16)

    for g, (d, out_ref) in enumerate(zip(DILATIONS, (q0_ref, q1_ref, q2_ref))):
        @pl.when(j == N_MAIN_TILES + g)
        def _(d=d, out_ref=out_ref):
            if d == 1:
                out_ref[0, 0] = acc.astype(BF16)
            else:
                for cb in range(IN_TN // LANES):
                    acc_ref[cb] = acc[:, cb * LANES:(cb + 1) * LANES]
                for r in range(d):
                    for cb in range(IN_TN // LANES):
                        out_ref[0, r, :, cb * LANES:(cb + 1) * LANES] = (
                            acc_ref[cb, pl.ds(r, IN_TM // d, stride=d), :].astype(BF16))

    @pl.when(j >= FIRST_GATE_TILE)
    def _():
        gate_ref[...] = _sigmoid(acc + b_ref[...]).astype(BF16)


def _in_proj(x2, g, w, b, batch, seq):
    T = x2.shape[0]
    tiles_per_batch = seq // IN_TM

    def qkv_spec(d):
        return pl.BlockSpec((1, d, IN_TM // d, QKV_WIDTH),
                            lambda i, j: (i // tiles_per_batch, 0, i % tiles_per_batch, 0))

    return pl.pallas_call(
        _in_proj_kernel,
        grid=(T // IN_TM, FIRST_GATE_TILE + N_GATE_TILES),
        in_specs=[
            pl.BlockSpec((IN_TM, D_MODEL), lambda i, j: (i, 0)),
            pl.BlockSpec((1, D_MODEL), lambda i, j: (0, 0)),
            pl.BlockSpec((D_MODEL, IN_TN), lambda i, j: (0, j)),
            pl.BlockSpec((1, IN_TN), lambda i, j: (0, jnp.maximum(j - FIRST_GATE_TILE, 0))),
        ],
        out_specs=[
            pl.BlockSpec((IN_TM, IN_TN), lambda i, j: (i, jnp.minimum(j, N_MAIN_TILES - 1))),
            qkv_spec(DILATIONS[0]), qkv_spec(DILATIONS[1]), qkv_spec(DILATIONS[2]),
            pl.BlockSpec((IN_TM, IN_TN), lambda i, j: (i, jnp.maximum(j - FIRST_GATE_TILE, 0))),
        ],
        out_shape=[jax.ShapeDtypeStruct((T, MAIN_WIDTH), BF16)]
        + [jax.ShapeDtypeStruct((batch, d, seq // d, QKV_WIDTH), BF16) for d in DILATIONS]
        + [jax.ShapeDtypeStruct((T, GATE_WIDTH), BF16)],
        scratch_shapes=[pltpu.VMEM((IN_TM, D_MODEL), BF16),
                        pltpu.VMEM((IN_TN // LANES, IN_TM, LANES), F32)],
        compiler_params=pltpu.CompilerParams(
            dimension_semantics=("parallel", "arbitrary"), vmem_limit_bytes=VMEM_LIMIT),
        name="in_proj",
    )(x2, g, w, b)


CONV_TT = 256
CONV_HALO = 16
CONV_RC = 64
SUBLANES = 8
CONV_FIRST = CONV_HALO - CONV_PAD
CONV_SHIFT_ROWS = CONV_TT + (CONV_FIRST + CONV_KSIZE - 1) // SUBLANES * SUBLANES


def _conv_kernel(a_ref, gt_ref, ap_ref, gp_ref, an_ref, gn_ref, w_ref, b_ref, lg_ref, lb_ref,
                 out_ref, u_ref, us_ref):
    i = pl.program_id(1)
    n = pl.num_programs(1)

    def glu(a, g):
        return a.astype(F32) * _sigmoid(g.astype(F32))

    u_ref[CONV_HALO:CONV_HALO + CONV_TT, :] = glu(a_ref[0], gt_ref[0])
    u_ref[0:CONV_HALO, :] = jnp.where(i > 0, glu(ap_ref[0], gp_ref[0]), 0.0)
    u_ref[CONV_HALO + CONV_TT:, :] = jnp.where(i < n - 1, glu(an_ref[0], gn_ref[0]), 0.0)
    for s in range(SUBLANES):
        us_ref[s] = u_ref[s:s + CONV_SHIFT_ROWS, :]

    for c in range(CONV_TT // CONV_RC):
        acc = jnp.zeros((CONV_RC, CONV_WIDTH), F32)
        for k in range(CONV_KSIZE):
            off = CONV_FIRST + k
            r0 = c * CONV_RC + off // SUBLANES * SUBLANES
            acc = acc + us_ref[off % SUBLANES, r0:r0 + CONV_RC, :] * w_ref[k:k + 1, :]
        y = acc + b_ref[...]
        mu = jnp.mean(y, axis=-1, keepdims=True)
        yc = y - mu
        yn = yc * lax.rsqrt(jnp.mean(yc * yc, axis=-1, keepdims=True) + LN_EPS)
        yn = yn * lg_ref[...] + lb_ref[...]
        out_ref[0, c * CONV_RC:(c + 1) * CONV_RC, :] = (yn * _sigmoid(yn)).astype(BF16)


def _conv(main3, w_dw, b_dw, ln_g, ln_b):
    B, S, _ = main3.shape
    nh = CONV_TT // CONV_HALO
    last_h = S // CONV_HALO - 1

    def cur(col):
        return pl.BlockSpec((1, CONV_TT, CONV_WIDTH), lambda b, i: (b, i, col))

    def prev(col):
        return pl.BlockSpec((1, CONV_HALO, CONV_WIDTH),
                            lambda b, i: (b, jnp.maximum(i * nh - 1, 0), col))

    def nxt(col):
        return pl.BlockSpec((1, CONV_HALO, CONV_WIDTH),
                            lambda b, i: (b, jnp.minimum((i + 1) * nh, last_h), col))

    def vec(rows):
        return pl.BlockSpec((rows, CONV_WIDTH), lambda b, i: (0, 0))

    return pl.pallas_call(
        _conv_kernel,
        grid=(B, S // CONV_TT),
        in_specs=[cur(0), cur(1), prev(0), prev(1), nxt(0), nxt(1),
                  vec(CONV_KSIZE), vec(1), vec(1), vec(1)],
        out_specs=pl.BlockSpec((1, CONV_TT, CONV_WIDTH), lambda b, i: (b, i, 0)),
        out_shape=jax.ShapeDtypeStruct((B, S, CONV_WIDTH), BF16),
        scratch_shapes=[pltpu.VMEM((CONV_TT + 2 * CONV_HALO, CONV_WIDTH), F32),
                        pltpu.VMEM((SUBLANES, CONV_SHIFT_ROWS, CONV_WIDTH), F32)],
        compiler_params=pltpu.CompilerParams(
            dimension_semantics=("parallel", "parallel"), vmem_limit_bytes=VMEM_LIMIT),
        name="conv_module",
    )(main3, main3, main3, main3, main3, main3, w_dw, b_dw, ln_g, ln_b)


ATT_TILE = 2048
ATT_COMBINE_ROWS = 256


def _att_kernel(c0_ref, p0_ref, n0_ref, c1_ref, p1_ref, n1_ref, c2_ref, p2_ref, n2_ref,
                b0_ref, b1_ref, b2_ref, o_ref, onat_ref, lnat_ref):
    i = pl.program_id(1)
    n = pl.num_programs(1)
    head_of_lane = lax.broadcasted_iota(jnp.int32, (1, ATT_OUT), 1) // HEAD_DIM
    lse_head_of_lane = lax.broadcasted_iota(jnp.int32, (1, LSE_LANES), 1) // LSE_PER_HEAD
    key_col = lax.broadcasted_iota(jnp.int32, (1, SUBK), 1)
    head_masks = [head_of_lane == h for h in range(HEADS)]
    scale = jnp.asarray(HEAD_DIM ** -0.5, BF16)
    K0, V0 = ATT_OUT, 2 * ATT_OUT

    def subtile(g, d, q, k, v, bias_ref, first, last, row0):
        qs = jnp.concatenate([jnp.where(hm, q, jnp.zeros_like(q)) for hm in head_masks], axis=0) * scale
        s = lax.dot_general(qs, k, (((1,), (1,)), ((), ())), preferred_element_type=F32)
        s = s + bias_ref[...]
        if first:
            s = jnp.where(key_col >= jnp.where(i == 0, RADIUS, 0), s, NEG_INF)
        if last:
            s = jnp.where(key_col < jnp.where(i == n - 1, SUBK - RADIUS, SUBK), s, NEG_INF)
        m = jnp.max(s, axis=-1, keepdims=True)
        e = jnp.exp(s - m)
        l = jnp.sum(e, axis=-1, keepdims=True)
        p = (e * (1.0 / l)).astype(BF16)
        o_all = jnp.dot(p, v, preferred_element_type=F32)
        lse = m + jnp.log(l)
        o = jnp.zeros((SUBQ, ATT_OUT), F32)
        lse_tile = jnp.zeros((SUBQ, LSE_LANES), F32)
        for h in range(HEADS):
            rows = slice(h * SUBQ, (h + 1) * SUBQ)
            o = jnp.where(head_masks[h], o_all[rows], o)
            lse_tile = jnp.where(lse_head_of_lane == h, lse[rows], lse_tile)
        if d == 1:
            if not isinstance(row0, int):
                row0 = pl.multiple_of(row0, SUBQ)
            rows = pl.ds(row0, SUBQ)
        else:
            rows = pl.ds(row0, SUBQ, stride=d)
        for half in range(ATT_OUT // LANES):
            onat_ref[2 * g + half, rows, :] = o[:, half * LANES:(half + 1) * LANES]
        lnat_ref[g, rows, :] = lse_tile

    groups = ((c0_ref, p0_ref, n0_ref, b0_ref), (c1_ref, p1_ref, n1_ref, b1_ref),
              (c2_ref, p2_ref, n2_ref, b2_ref))
    for g, (d, (c_ref, p_ref, n_ref, bias_ref)) in enumerate(zip(DILATIONS, groups)):
        tq = ATT_TILE // d
        nsub = tq // SUBQ

        def residue(r, carry, g=g, d=d, c_ref=c_ref, p_ref=p_ref, n_ref=n_ref, bias_ref=bias_ref,
                    tq=tq, nsub=nsub):
            def kv_cat(parts, col):
                return jnp.concatenate([ref[0, r, rows, col:col + ATT_OUT] for ref, rows in parts], axis=0)

            if nsub == 1:
                parts = [(p_ref, slice(None)), (c_ref, slice(None)), (n_ref, slice(None))]
                subtile(g, d, c_ref[0, r, :, 0:ATT_OUT], kv_cat(parts, K0), kv_cat(parts, V0),
                        bias_ref, True, True, r)
                return carry
            parts = [(p_ref, slice(None)), (c_ref, slice(0, SUBK - RADIUS))]
            subtile(g, d, c_ref[0, r, 0:SUBQ, 0:ATT_OUT], kv_cat(parts, K0), kv_cat(parts, V0),
                    bias_ref, True, False, r)

            def interior(j, carry2):
                q0 = pl.multiple_of(j * SUBQ, SUBQ)
                k0 = pl.multiple_of(j * SUBQ - RADIUS, RADIUS)
                subtile(g, d, c_ref[0, r, pl.ds(q0, SUBQ), 0:ATT_OUT],
                        c_ref[0, r, pl.ds(k0, SUBK), K0:K0 + ATT_OUT],
                        c_ref[0, r, pl.ds(k0, SUBK), V0:V0 + ATT_OUT],
                        bias_ref, False, False, j * SUBQ * d + r)
                return carry2

            lax.fori_loop(1, nsub - 1, interior, 0)
            parts = [(c_ref, slice(tq - (SUBK - RADIUS), tq)), (n_ref, slice(None))]
            subtile(g, d, c_ref[0, r, tq - SUBQ:tq, 0:ATT_OUT], kv_cat(parts, K0), kv_cat(parts, V0),
                    bias_ref, False, True, (tq - SUBQ) * d + r)
            return carry

        if d == 1:
            residue(0, 0)
        else:
            lax.fori_loop(0, d, residue, 0)

    def expand(w, rows):
        full = jnp.broadcast_to(w[:, 0:1], (rows, ATT_OUT))
        for h in range(1, HEADS):
            col = w[:, h * LSE_PER_HEAD:h * LSE_PER_HEAD + 1]
            full = jnp.where(head_masks[h], jnp.broadcast_to(col, (rows, ATT_OUT)), full)
        return full

    def combine(t, carry):
        rows = pl.ds(pl.multiple_of(t * ATT_COMBINE_ROWS, ATT_COMBINE_ROWS), ATT_COMBINE_ROWS)
        l0, l1, l2 = lnat_ref[0, rows, :], lnat_ref[1, rows, :], lnat_ref[2, rows, :]
        mx = jnp.maximum(jnp.maximum(l0, l1), l2)
        e0, e1, e2 = jnp.exp(l0 - mx), jnp.exp(l1 - mx), jnp.exp(l2 - mx)
        inv = 1.0 / (e0 + e1 + e2)
        def o_nat(g):
            return jnp.concatenate([onat_ref[2 * g, rows, :], onat_ref[2 * g + 1, rows, :]], axis=1)

        o = (expand(e0 * inv, ATT_COMBINE_ROWS) * o_nat(0)
             + expand(e1 * inv, ATT_COMBINE_ROWS) * o_nat(1)
             + expand(e2 * inv, ATT_COMBINE_ROWS) * o_nat(2))
        o_ref[0, rows, :] = o.astype(BF16)
        return carry

    lax.fori_loop(0, ATT_TILE // ATT_COMBINE_ROWS, combine, 0)


def _attention(qkv, biases):
    B = qkv[0].shape[0]
    S = qkv[0].shape[2]
    in_specs = []
    for d in DILATIONS:
        tq = ATT_TILE // d
        nb = tq // RADIUS
        last_b = S // d // RADIUS - 1
        in_specs += [
            pl.BlockSpec((1, d, tq, QKV_WIDTH), lambda b, i: (b, 0, i, 0)),
            pl.BlockSpec((1, d, RADIUS, QKV_WIDTH),
                         lambda b, i, nb=nb: (b, 0, jnp.maximum(i * nb - 1, 0), 0)),
            pl.BlockSpec((1, d, RADIUS, QKV_WIDTH),
                         lambda b, i, nb=nb, last_b=last_b: (b, 0, jnp.minimum((i + 1) * nb, last_b), 0)),
        ]
    in_specs += [pl.BlockSpec((HEADS * SUBQ, SUBK), lambda b, i: (0, 0))] * N_GROUPS
    args = []
    for a in qkv:
        args += [a, a, a]
    return pl.pallas_call(
        _att_kernel,
        grid=(B, S // ATT_TILE),
        in_specs=in_specs,
        out_specs=pl.BlockSpec((1, ATT_TILE, ATT_OUT), lambda b, i: (b, i, 0)),
        out_shape=jax.ShapeDtypeStruct((B, S, ATT_OUT), BF16),
        scratch_shapes=[pltpu.VMEM((N_GROUPS * (ATT_OUT // LANES), ATT_TILE, LANES), F32),
                        pltpu.VMEM((N_GROUPS, ATT_TILE, LSE_LANES), F32)],
        compiler_params=pltpu.CompilerParams(
            dimension_semantics=("parallel", "parallel"), vmem_limit_bytes=VMEM_LIMIT),
        name="dilated_att",
    )(*args, *biases)


def _t5_bucket_np(rel):
    nb = NUM_BUCKETS // 2
    max_exact = nb // 2
    ret = np.where(rel > 0, nb, 0)
    n = np.abs(rel)
    nf = np.maximum(n, 1).astype(np.float32)
    ratio = np.log(nf / np.float32(max_exact)) / np.float32(math.log(MAX_DISTANCE / max_exact))
    large = max_exact + (ratio * np.float32(nb - max_exact)).astype(np.int32)
    large = np.minimum(large, nb - 1)
    return ret + np.where(n < max_exact, n, large)


def _band_bias(rel_bias, g, dilation):
    period = SUBQ + SUBK
    nband = 2 * RADIUS + 1
    bucket = _t5_bucket_np((np.arange(nband) - RADIUS) * dilation)
    onehot = np.zeros((period, NUM_BUCKETS), np.float32)
    onehot[np.arange(nband), bucket] = 1.0
    tab = rel_bias[:, g * HEADS:(g + 1) * HEADS].astype(F32)
    t = jnp.dot(jnp.asarray(onehot), tab, precision=lax.Precision.HIGHEST)
    t = jnp.where((np.arange(period) < nband)[:, None], t, NEG_INF).T
    skew = jnp.tile(t, (1, SUBQ))[:, :SUBQ * (period - 1)].reshape(HEADS, SUBQ, period - 1)
    return skew[:, :, :SUBK].reshape(HEADS * SUBQ, SUBK)


def _mem_kv_kernel(mem_ref, g_ref, w_ref, kv_ref):
    h = _rms(mem_ref[0], g_ref[...]).astype(BF16)
    kv_ref[0] = jnp.dot(h, w_ref[...], preferred_element_type=F32).astype(BF16)


def _mem_kv(mem, g, w):
    B, M, _ = mem.shape
    return pl.pallas_call(
        _mem_kv_kernel,
        grid=(B,),
        in_specs=[pl.BlockSpec((1, M, D_MODEL), lambda b: (b, 0, 0)),
                  pl.BlockSpec((1, D_MODEL), lambda b: (0, 0)),
                  pl.BlockSpec((D_MODEL, 2 * MEM_WIDTH), lambda b: (0, 0))],
        out_specs=pl.BlockSpec((1, M, 2 * MEM_WIDTH), lambda b: (b, 0, 0)),
        out_shape=jax.ShapeDtypeStruct((B, M, 2 * MEM_WIDTH), BF16),
        compiler_params=pltpu.CompilerParams(
            dimension_semantics=("parallel",), vmem_limit_bytes=VMEM_LIMIT),
        name="mem_kv",
    )(mem, g, w)


MIX_TM = 512


def _mix_kernel(x_ref, c_ref, qm_ref, kv_ref, oatt_ref, gate_ref, wc_ref, wa_ref, wm_ref, wo_ref,
                gpost_ref, out_ref):
    scale = MEM_HEAD_DIM ** -0.5
    heads = []
    for h in range(MEM_HEADS):
        lo = h * MEM_HEAD_DIM
        qh = qm_ref[:, lo:lo + MEM_HEAD_DIM]
        kh = kv_ref[0, :, lo:lo + MEM_HEAD_DIM]
        vh = kv_ref[0, :, MEM_WIDTH + lo:MEM_WIDTH + lo + MEM_HEAD_DIM]
        s = lax.dot_general(qh, kh, (((1,), (1,)), ((), ())), preferred_element_type=F32) * scale
        m = jnp.max(s, axis=-1, keepdims=True)
        e = jnp.exp(s - m)
        p = (e * (1.0 / jnp.sum(e, axis=-1, keepdims=True))).astype(BF16)
        heads.append(jnp.dot(p, vh, preferred_element_type=F32).astype(BF16))
    o_mem = jnp.concatenate(heads, axis=-1)
    y_mem = jnp.dot(o_mem, wm_ref[...], preferred_element_type=F32)
    y_conv = jnp.dot(c_ref[...], wc_ref[...], preferred_element_type=F32)
    y_att = jnp.dot(oatt_ref[...], wa_ref[...], preferred_element_type=F32)
    merged = (gate_ref[:, 0:D_MODEL].astype(F32) * y_conv
              + gate_ref[:, D_MODEL:2 * D_MODEL].astype(F32) * y_att
              + gate_ref[:, 2 * D_MODEL:].astype(F32) * y_mem)
    y = jnp.dot(merged.astype(BF16), wo_ref[...], preferred_element_type=F32)
    out_ref[...] = x_ref[...] + _rms(y, gpost_ref[...])


def _mix(x2, c2, main, kv, o_att, gates, wc, wa, wm, wo, gpost, tiles_per_batch):
    T = x2.shape[0]
    M = kv.shape[1]

    def rows(width, col=0):
        return pl.BlockSpec((MIX_TM, width), lambda i: (i, col))

    def whole(shape):
        return pl.BlockSpec(shape, lambda i: (0,) * len(shape))

    return pl.pallas_call(
        _mix_kernel,
        grid=(T // MIX_TM,),
        in_specs=[rows(D_MODEL), rows(CONV_WIDTH),
                  rows(MEM_WIDTH, 2 * CONV_WIDTH // MEM_WIDTH),
                  pl.BlockSpec((1, M, 2 * MEM_WIDTH), lambda i: (i // tiles_per_batch, 0, 0)),
                  rows(ATT_OUT), rows(GATE_WIDTH),
                  whole((CONV_WIDTH, D_MODEL)), whole((ATT_OUT, D_MODEL)),
                  whole((MEM_WIDTH, D_MODEL)), whole((D_MODEL, D_MODEL)), whole((1, D_MODEL))],
        out_specs=rows(D_MODEL),
        out_shape=jax.ShapeDtypeStruct((T, D_MODEL), F32),
        compiler_params=pltpu.CompilerParams(
            dimension_semantics=("parallel",), vmem_limit_bytes=VMEM_LIMIT),
        name="mix_out",
    )(x2, c2, main, kv, o_att, gates, wc, wa, wm, wo, gpost)


FFN_TM = 1024
FFN_TH = 256
FFN_CHUNKS = FFN_HIDDEN // FFN_TH


def _ffn_kernel(x_ref, gpre_ref, wg_ref, wu_ref, wd_ref, gpost_ref, out_ref, h_ref, acc_ref):
    j = pl.program_id(1)

    @pl.when(j == 0)
    def _():
        h_ref[...] = _rms(x_ref[...], gpre_ref[...]).astype(BF16)
        acc_ref[...] = jnp.zeros_like(acc_ref)

    h = h_ref[...]
    gv = jnp.dot(h, wg_ref[...], preferred_element_type=F32)
    uv = jnp.dot(h, wu_ref[...], preferred_element_type=F32)
    a = (gv * _sigmoid(gv) * uv).astype(BF16)
    acc_ref[...] += jnp.dot(a, wd_ref[...], preferred_element_type=F32)

    @pl.when(j == FFN_CHUNKS - 1)
    def _():
        out_ref[...] = x_ref[...] + _rms(acc_ref[...], gpost_ref[...])


def _ffn(x2, gpre, w_in, w_out, gpost):
    T = x2.shape[0]
    return pl.pallas_call(
        _ffn_kernel,
        grid=(T // FFN_TM, FFN_CHUNKS),
        in_specs=[pl.BlockSpec((FFN_TM, D_MODEL), lambda i, j: (i, 0)),
                  pl.BlockSpec((1, D_MODEL), lambda i, j: (0, 0)),
                  pl.BlockSpec((D_MODEL, FFN_TH), lambda i, j: (0, j)),
                  pl.BlockSpec((D_MODEL, FFN_TH), lambda i, j: (0, FFN_CHUNKS + j)),
                  pl.BlockSpec((FFN_TH, D_MODEL), lambda i, j: (j, 0)),
                  pl.BlockSpec((1, D_MODEL), lambda i, j: (0, 0))],
        out_specs=pl.BlockSpec((FFN_TM, D_MODEL), lambda i, j: (i, 0)),
        out_shape=jax.ShapeDtypeStruct((T, D_MODEL), F32),
        scratch_shapes=[pltpu.VMEM((FFN_TM, D_MODEL), BF16), pltpu.VMEM((FFN_TM, D_MODEL), F32)],
        compiler_params=pltpu.CompilerParams(
            dimension_semantics=("parallel", "arbitrary"), vmem_limit_bytes=VMEM_LIMIT),
        name="ffn",
    )(x2, gpre, w_in, w_in, w_out, gpost)


def _permute_in_proj(w):
    c1 = 2 * CONV_WIDTH
    c2 = c1 + N_GROUPS * QKV_WIDTH
    c3 = c2 + MEM_WIDTH
    group_cols = [w[:, c1 + which * N_GROUPS * ATT_OUT + g * ATT_OUT:
                    c1 + which * N_GROUPS * ATT_OUT + (g + 1) * ATT_OUT]
                  for g in range(N_GROUPS) for which in range(3)]
    return jnp.concatenate([w[:, :c1], w[:, c2:c3]] + group_cols + [w[:, c3:]], axis=1)


def kernel(x, mem, rel_bias, norm_mix_pre, w_in, b_gate, conv_dw, conv_dw_bias, conv_ln_g, conv_ln_b,
           w_conv_out, w_att_out, norm_mem, w_mem_kv, w_mem_out, w_out, norm_mix_post, norm_ffn_pre,
           w_ffn_in, w_ffn_out, norm_ffn_post):
    B, S, D = x.shape
    depth = w_in.shape[0]
    T = B * S
    biases = [_band_bias(rel_bias, g, d) for g, d in enumerate(DILATIONS)]

    def row(v):
        return v.reshape(1, -1)

    x2 = x.reshape(T, D)
    for l in range(depth):
        main, qkv0, qkv1, qkv2, gates = _in_proj(
            x2, row(norm_mix_pre[l]), _permute_in_proj(w_in[l]).astype(BF16), row(b_gate[l]), B, S)
        c = _conv(main.reshape(B, S, MAIN_WIDTH), conv_dw[l], row(conv_dw_bias[l]),
                  row(conv_ln_g[l]), row(conv_ln_b[l]))
        o_att = _attention((qkv0, qkv1, qkv2), biases)
        kv = _mem_kv(mem, row(norm_mem[l]), w_mem_kv[l].astype(BF16))
        x2 = _mix(x2, c.reshape(T, CONV_WIDTH), main, kv, o_att.reshape(T, ATT_OUT), gates,
                  w_conv_out[l].astype(BF16), w_att_out[l].astype(BF16), w_mem_out[l].astype(BF16),
                  w_out[l].astype(BF16), row(norm_mix_post[l]), S // MIX_TM)
        x2 = _ffn(x2, row(norm_ffn_pre[l]), w_ffn_in[l].astype(BF16), w_ffn_out[l].astype(BF16),
                  row(norm_ffn_post[l]))
    return x2.reshape(B, S, D)
```

```python
import functools
import math

import numpy as np
import jax
import jax.numpy as jnp
from jax import lax
from jax.experimental import pallas as pl
from jax.experimental.pallas import tpu as pltpu

F32 = jnp.float32
BF16 = jnp.bfloat16

D_MODEL = 1024
CONV_WIDTH = 512
CONV_KSIZE = 31
CONV_PAD = CONV_KSIZE // 2
DILATIONS = (1, 4, 16)
RADIUS = 64
N_GROUPS = 3
HEADS = 4
HEAD_DIM = 64
ATT_OUT = HEADS * HEAD_DIM
QKV_WIDTH = 3 * ATT_OUT
MEM_HEADS = 4
MEM_HEAD_DIM = 128
MEM_WIDTH = 512
FFN_HIDDEN = 2816
NUM_BUCKETS = 32
MAX_DISTANCE = 1024
RMS_EPS = 1e-6
LN_EPS = 1e-5
NEG_INF = -1e30

MAIN_WIDTH = 2 * CONV_WIDTH + MEM_WIDTH
GATE_WIDTH = 3 * D_MODEL
SUBQ = 128
SUBK = SUBQ + 2 * RADIUS
LANES = 128
LSE_LANES = 128
LSE_PER_HEAD = LSE_LANES // HEADS
STAT_LANES = LSE_PER_HEAD // 2
LOG2E = math.log2(math.e)

VMEM_LIMIT = 56 * 1024 * 1024


def _sigmoid(v):
    return 1.0 / (1.0 + jnp.exp(-v))


def _rms(v, g):
    return v * lax.rsqrt(jnp.mean(v * v, axis=-1, keepdims=True) + RMS_EPS) * g


IN_TM = 1024
IN_TN = 768
N_MAIN_TILES = MAIN_WIDTH // IN_TN
N_GATE_TILES = GATE_WIDTH // IN_TN
FIRST_GATE_TILE = N_MAIN_TILES + N_GROUPS
assert IN_TN == QKV_WIDTH


def _in_proj_kernel(x_ref, g_ref, w_ref, b_ref, main_ref, q0_ref, q1_ref, q2_ref, gate_ref,
                    h_ref, acc_ref):
    j = pl.program_id(1)

    @pl.when(j == 0)
    def _():
        h_ref[...] = _rms(x_ref[...], g_ref[...]).astype(BF16)

    def project():
        return jnp.dot(h_ref[...], w_ref[...], preferred_element_type=F32)

    @pl.when(j < N_MAIN_TILES)
    def _():
        main_ref[...] = project().astype(BF16)

    q_scale = HEAD_DIM ** -0.5 * LOG2E
    for g, (d, out_ref) in enumerate(zip(DILATIONS, (q0_ref, q1_ref, q2_ref))):
        @pl.when(j == N_MAIN_TILES + g)
        def _(d=d, out_ref=out_ref):
            acc = project()
            if d == 1:
                out_ref[0, 0, :, 0:ATT_OUT] = (acc[:, 0:ATT_OUT] * q_scale).astype(BF16)
                out_ref[0, 0, :, ATT_OUT:] = acc[:, ATT_OUT:].astype(BF16)
            else:
                for cb in range(IN_TN // LANES):
                    blk = acc[:, cb * LANES:(cb + 1) * LANES]
                    acc_ref[cb] = blk * q_scale if cb < ATT_OUT // LANES else blk
                for r in range(d):
                    for cb in range(IN_TN // LANES):
                        out_ref[0, r, :, cb * LANES:(cb + 1) * LANES] = (
                            acc_ref[cb, pl.ds(r, IN_TM // d, stride=d), :].astype(BF16))

    @pl.when(j >= FIRST_GATE_TILE)
    def _():
        gate_ref[...] = _sigmoid(project() + b_ref[...]).astype(BF16)


def _in_proj(x2, g, w, b, batch, seq):
    T = x2.shape[0]
    tiles_per_batch = seq // IN_TM

    def qkv_spec(d):
        return pl.BlockSpec((1, d, IN_TM // d, QKV_WIDTH),
                            lambda i, j: (i // tiles_per_batch, 0, i % tiles_per_batch, 0))

    return pl.pallas_call(
        _in_proj_kernel,
        grid=(T // IN_TM, FIRST_GATE_TILE + N_GATE_TILES),
        in_specs=[
            pl.BlockSpec((IN_TM, D_MODEL), lambda i, j: (i, 0)),
            pl.BlockSpec((1, D_MODEL), lambda i, j: (0, 0)),
            pl.BlockSpec((D_MODEL, IN_TN), lambda i, j: (0, j)),
            pl.BlockSpec((1, IN_TN), lambda i, j: (0, jnp.maximum(j - FIRST_GATE_TILE, 0))),
        ],
        out_specs=[
            pl.BlockSpec((IN_TM, IN_TN), lambda i, j: (i, jnp.minimum(j, N_MAIN_TILES - 1))),
            qkv_spec(DILATIONS[0]), qkv_spec(DILATIONS[1]), qkv_spec(DILATIONS[2]),
            pl.BlockSpec((IN_TM, IN_TN), lambda i, j: (i, jnp.maximum(j - FIRST_GATE_TILE, 0))),
        ],
        out_shape=[jax.ShapeDtypeStruct((T, MAIN_WIDTH), BF16)]
        + [jax.ShapeDtypeStruct((batch, d, seq // d, QKV_WIDTH), BF16) for d in DILATIONS]
        + [jax.ShapeDtypeStruct((T, GATE_WIDTH), BF16)],
        scratch_shapes=[pltpu.VMEM((IN_TM, D_MODEL), BF16),
                        pltpu.VMEM((IN_TN // LANES, IN_TM, LANES), F32)],
        compiler_params=pltpu.CompilerParams(
            dimension_semantics=("parallel", "arbitrary"), vmem_limit_bytes=VMEM_LIMIT),
        name="in_proj",
    )(x2, g, w, b)


CONV_TT = 256
CONV_HALO = 16
CONV_RC = 64
SUBLANES = 8
CONV_FIRST = CONV_HALO - CONV_PAD
CONV_SHIFT_ROWS = CONV_TT + (CONV_FIRST + CONV_KSIZE - 1) // SUBLANES * SUBLANES


def _conv_kernel(a_ref, gt_ref, ap_ref, gp_ref, an_ref, gn_ref, w_ref, b_ref, lg_ref, lb_ref,
                 out_ref, u_ref, us_ref):
    i = pl.program_id(1)
    n = pl.num_programs(1)

    def glu(a, g):
        return a.astype(F32) * _sigmoid(g.astype(F32))

    u_ref[CONV_HALO:CONV_HALO + CONV_TT, :] = glu(a_ref[0], gt_ref[0])
    u_ref[0:CONV_HALO, :] = jnp.where(i > 0, glu(ap_ref[0], gp_ref[0]), 0.0)
    u_ref[CONV_HALO + CONV_TT:, :] = jnp.where(i < n - 1, glu(an_ref[0], gn_ref[0]), 0.0)
    for s in range(SUBLANES):
        us_ref[s] = u_ref[s:s + CONV_SHIFT_ROWS, :]

    for c in range(CONV_TT // CONV_RC):
        acc = jnp.zeros((CONV_RC, CONV_WIDTH), F32)
        for k in range(CONV_KSIZE):
            off = CONV_FIRST + k
            r0 = c * CONV_RC + off // SUBLANES * SUBLANES
            acc = acc + us_ref[off % SUBLANES, r0:r0 + CONV_RC, :] * w_ref[k:k + 1, :]
        y = acc + b_ref[...]
        mu = jnp.mean(y, axis=-1, keepdims=True)
        yc = y - mu
        yn = yc * lax.rsqrt(jnp.mean(yc * yc, axis=-1, keepdims=True) + LN_EPS)
        yn = yn * lg_ref[...] + lb_ref[...]
        out_ref[0, c * CONV_RC:(c + 1) * CONV_RC, :] = (yn * _sigmoid(yn)).astype(BF16)


def _conv(main3, w_dw, b_dw, ln_g, ln_b):
    B, S, _ = main3.shape
    nh = CONV_TT // CONV_HALO
    last_h = S // CONV_HALO - 1

    def cur(col):
        return pl.BlockSpec((1, CONV_TT, CONV_WIDTH), lambda b, i: (b, i, col))

    def prev(col):
        return pl.BlockSpec((1, CONV_HALO, CONV_WIDTH),
                            lambda b, i: (b, jnp.maximum(i * nh - 1, 0), col))

    def nxt(col):
        return pl.BlockSpec((1, CONV_HALO, CONV_WIDTH),
                            lambda b, i: (b, jnp.minimum((i + 1) * nh, last_h), col))

    def vec(rows):
        return pl.BlockSpec((rows, CONV_WIDTH), lambda b, i: (0, 0))

    return pl.pallas_call(
        _conv_kernel,
        grid=(B, S // CONV_TT),
        in_specs=[cur(0), cur(1), prev(0), prev(1), nxt(0), nxt(1),
                  vec(CONV_KSIZE), vec(1), vec(1), vec(1)],
        out_specs=pl.BlockSpec((1, CONV_TT, CONV_WIDTH), lambda b, i: (b, i, 0)),
        out_shape=jax.ShapeDtypeStruct((B, S, CONV_WIDTH), BF16),
        scratch_shapes=[pltpu.VMEM((CONV_TT + 2 * CONV_HALO, CONV_WIDTH), F32),
                        pltpu.VMEM((SUBLANES, CONV_SHIFT_ROWS, CONV_WIDTH), F32)],
        compiler_params=pltpu.CompilerParams(
            dimension_semantics=("parallel", "parallel"), vmem_limit_bytes=VMEM_LIMIT),
        name="conv_module",
    )(main3, main3, main3, main3, main3, main3, w_dw, b_dw, ln_g, ln_b)


ATT_TILE = 2048
ATT_COMBINE_ROWS = 256
ATT_ILP = 4


def _att_kernel(c0_ref, p0_ref, n0_ref, c1_ref, p1_ref, n1_ref, c2_ref, p2_ref, n2_ref,
                b0_ref, b1_ref, b2_ref, o_ref, onat_ref, lnat_ref):
    i = pl.program_id(1)
    n = pl.num_programs(1)
    head_of_lane = lax.broadcasted_iota(jnp.int32, (1, ATT_OUT), 1) // HEAD_DIM
    stat_of_lane = lax.broadcasted_iota(jnp.int32, (1, LSE_LANES), 1) // STAT_LANES
    key_col = lax.broadcasted_iota(jnp.int32, (1, SUBK), 1)
    head_masks = [head_of_lane == h for h in range(HEADS)]
    K0, V0 = ATT_OUT, 2 * ATT_OUT

    def subtile(g, d, q, k, v, bias_ref, first, last, row0):
        qs = jnp.concatenate([jnp.where(hm, q, jnp.zeros_like(q)) for hm in head_masks], axis=0)
        s = lax.dot_general(qs, k, (((1,), (1,)), ((), ())), preferred_element_type=F32)
        s = s + bias_ref[...]
        if first:
            s = jnp.where(key_col >= jnp.where(i == 0, RADIUS, 0), s, NEG_INF)
        if last:
            s = jnp.where(key_col < jnp.where(i == n - 1, SUBK - RADIUS, SUBK), s, NEG_INF)
        m = jnp.max(s, axis=-1, keepdims=True)
        e = jnp.exp2(s - m)
        l = jnp.sum(e, axis=-1, keepdims=True)
        o_all = jnp.dot(e.astype(BF16), v, preferred_element_type=F32)
        o = o_all[(HEADS - 1) * SUBQ:]
        stats = jnp.broadcast_to(l[(HEADS - 1) * SUBQ:], (SUBQ, LSE_LANES))
        stats = jnp.where(stat_of_lane == 2 * (HEADS - 1), m[(HEADS - 1) * SUBQ:], stats)
        for h in range(HEADS - 1):
            rows = slice(h * SUBQ, (h + 1) * SUBQ)
            o = jnp.where(head_masks[h], o_all[rows], o)
            stats = jnp.where(stat_of_lane == 2 * h, m[rows], stats)
            stats = jnp.where(stat_of_lane == 2 * h + 1, l[rows], stats)
        lse_tile = stats
        if d == 1:
            if not isinstance(row0, int):
                row0 = pl.multiple_of(row0, SUBQ)
            rows = pl.ds(row0, SUBQ)
        else:
            rows = pl.ds(row0, SUBQ, stride=d)
        for half in range(ATT_OUT // LANES):
            onat_ref[2 * g + half, rows, :] = o[:, half * LANES:(half + 1) * LANES]
        lnat_ref[g, rows, :] = lse_tile

    groups = ((c0_ref, p0_ref, n0_ref, b0_ref), (c1_ref, p1_ref, n1_ref, b1_ref),
              (c2_ref, p2_ref, n2_ref, b2_ref))
    for g, (d, (c_ref, p_ref, n_ref, bias_ref)) in enumerate(zip(DILATIONS, groups)):
        tq = ATT_TILE // d
        nsub = tq // SUBQ

        def residue(r, carry, g=g, d=d, c_ref=c_ref, p_ref=p_ref, n_ref=n_ref, bias_ref=bias_ref,
                    tq=tq, nsub=nsub):
            def kv_cat(parts, col):
                return jnp.concatenate([ref[0, r, rows, col:col + ATT_OUT] for ref, rows in parts], axis=0)

            def sub(j):
                if isinstance(j, int) and (j == 0 or j == nsub - 1):
                    parts = [(c_ref, slice(max(j * SUBQ - RADIUS, 0), min(j * SUBQ + SUBQ + RADIUS, tq)))]
                    if j == 0:
                        parts = [(p_ref, slice(None))] + parts
                    if j == nsub - 1:
                        parts = parts + [(n_ref, slice(None))]
                    k, v = kv_cat(parts, K0), kv_cat(parts, V0)
                    q = c_ref[0, r, j * SUBQ:(j + 1) * SUBQ, 0:ATT_OUT]
                else:
                    q0 = j * SUBQ if isinstance(j, int) else pl.multiple_of(j * SUBQ, SUBQ)
                    k0 = j * SUBQ - RADIUS if isinstance(j, int) else pl.multiple_of(j * SUBQ - RADIUS, RADIUS)
                    q = c_ref[0, r, pl.ds(q0, SUBQ), 0:ATT_OUT]
                    k = c_ref[0, r, pl.ds(k0, SUBK), K0:K0 + ATT_OUT]
                    v = c_ref[0, r, pl.ds(k0, SUBK), V0:V0 + ATT_OUT]
                static = isinstance(j, int)
                subtile(g, d, q, k, v, bias_ref, static and j == 0, static and j == nsub - 1,
                        j * SUBQ * d + r)

            if nsub <= 2 * ATT_ILP:
                for j in range(nsub):
                    sub(j)
            else:
                for j in range(ATT_ILP):
                    sub(j)

                def interior(t, carry2):
                    for u in range(ATT_ILP):
                        sub(ATT_ILP + t * ATT_ILP + u)
                    return carry2

                lax.fori_loop(0, (nsub - 2 * ATT_ILP) // ATT_ILP, interior, 0)
                for j in range(nsub - ATT_ILP, nsub):
                    sub(j)
            return carry

        if d == 1:
            residue(0, 0)
        else:
            lax.fori_loop(0, d, residue, 0, unroll=max(1, ATT_ILP // nsub))

    def expand(w, rows):
        full = jnp.broadcast_to(w[:, 0:1], (rows, ATT_OUT))
        for h in range(1, HEADS):
            col = w[:, h * LSE_PER_HEAD:h * LSE_PER_HEAD + 1]
            full = jnp.where(head_masks[h], jnp.broadcast_to(col, (rows, ATT_OUT)), full)
        return full

    def combine(t, carry):
        rows = pl.ds(pl.multiple_of(t * ATT_COMBINE_ROWS, ATT_COMBINE_ROWS), ATT_COMBINE_ROWS)
        stats = [lnat_ref[g, rows, :] for g in range(N_GROUPS)]
        mx = jnp.maximum(jnp.maximum(stats[0], stats[1]), stats[2])
        a = [jnp.exp2(st - mx) for st in stats]
        sums = [pltpu.roll(st, LSE_LANES - STAT_LANES, 1) for st in stats]
        inv = 1.0 / (a[0] * sums[0] + a[1] * sums[1] + a[2] * sums[2])

        def o_nat(g):
            return jnp.concatenate([onat_ref[2 * g, rows, :], onat_ref[2 * g + 1, rows, :]], axis=1)

        o = (expand(a[0] * inv, ATT_COMBINE_ROWS) * o_nat(0)
             + expand(a[1] * inv, ATT_COMBINE_ROWS) * o_nat(1)
             + expand(a[2] * inv, ATT_COMBINE_ROWS) * o_nat(2))
        o_ref[0, rows, :] = o.astype(BF16)
        return carry

    lax.fori_loop(0, ATT_TILE // ATT_COMBINE_ROWS, combine, 0)


def _attention(qkv, biases):
    B = qkv[0].shape[0]
    S = qkv[0].shape[2]
    in_specs = []
    for d in DILATIONS:
        tq = ATT_TILE // d
        nb = tq // RADIUS
        last_b = S // d // RADIUS - 1
        in_specs += [
            pl.BlockSpec((1, d, tq, QKV_WIDTH), lambda b, i: (b, 0, i, 0)),
            pl.BlockSpec((1, d, RADIUS, QKV_WIDTH),
                         lambda b, i, nb=nb: (b, 0, jnp.maximum(i * nb - 1, 0), 0)),
            pl.BlockSpec((1, d, RADIUS, QKV_WIDTH),
                         lambda b, i, nb=nb, last_b=last_b: (b, 0, jnp.minimum((i + 1) * nb, last_b), 0)),
        ]
    in_specs += [pl.BlockSpec((HEADS * SUBQ, SUBK), lambda b, i: (0, 0))] * N_GROUPS
    args = []
    for a in qkv:
        args += [a, a, a]
    return pl.pallas_call(
        _att_kernel,
        grid=(B, S // ATT_TILE),
        in_specs=in_specs,
        out_specs=pl.BlockSpec((1, ATT_TILE, ATT_OUT), lambda b, i: (b, i, 0)),
        out_shape=jax.ShapeDtypeStruct((B, S, ATT_OUT), BF16),
        scratch_shapes=[pltpu.VMEM((N_GROUPS * (ATT_OUT // LANES), ATT_TILE, LANES), F32),
                        pltpu.VMEM((N_GROUPS, ATT_TILE, LSE_LANES), F32)],
        compiler_params=pltpu.CompilerParams(
            dimension_semantics=("parallel", "parallel"), vmem_limit_bytes=VMEM_LIMIT),
        name="dilated_att",
    )(*args, *biases)


def _t5_bucket_np(rel):
    nb = NUM_BUCKETS // 2
    max_exact = nb // 2
    ret = np.where(rel > 0, nb, 0)
    n = np.abs(rel)
    nf = np.maximum(n, 1).astype(np.float32)
    ratio = np.log(nf / np.float32(max_exact)) / np.float32(math.log(MAX_DISTANCE / max_exact))
    large = max_exact + (ratio * np.float32(nb - max_exact)).astype(np.int32)
    large = np.minimum(large, nb - 1)
    return ret + np.where(n < max_exact, n, large)


def _band_bias(rel_bias, g, dilation):
    period = SUBQ + SUBK
    nband = 2 * RADIUS + 1
    bucket = _t5_bucket_np((np.arange(nband) - RADIUS) * dilation)
    onehot = np.zeros((period, NUM_BUCKETS), np.float32)
    onehot[np.arange(nband), bucket] = 1.0
    tab = rel_bias[:, g * HEADS:(g + 1) * HEADS].astype(F32)
    t = jnp.dot(jnp.asarray(onehot), tab, precision=lax.Precision.HIGHEST)
    t = jnp.where((np.arange(period) < nband)[:, None], t, NEG_INF).T
    skew = jnp.tile(t, (1, SUBQ))[:, :SUBQ * (period - 1)].reshape(HEADS, SUBQ, period - 1)
    return skew[:, :, :SUBK].reshape(HEADS * SUBQ, SUBK) * LOG2E


def _mem_kv_kernel(mem_ref, g_ref, w_ref, kv_ref):
    h = _rms(mem_ref[0], g_ref[...]).astype(BF16)
    kv_ref[0] = jnp.dot(h, w_ref[...], preferred_element_type=F32).astype(BF16)


def _mem_kv(mem, g, w):
    B, M, _ = mem.shape
    return pl.pallas_call(
        _mem_kv_kernel,
        grid=(B,),
        in_specs=[pl.BlockSpec((1, M, D_MODEL), lambda b: (b, 0, 0)),
                  pl.BlockSpec((1, D_MODEL), lambda b: (0, 0)),
                  pl.BlockSpec((D_MODEL, 2 * MEM_WIDTH), lambda b: (0, 0))],
        out_specs=pl.BlockSpec((1, M, 2 * MEM_WIDTH), lambda b: (b, 0, 0)),
        out_shape=jax.ShapeDtypeStruct((B, M, 2 * MEM_WIDTH), BF16),
        compiler_params=pltpu.CompilerParams(
            dimension_semantics=("parallel",), vmem_limit_bytes=VMEM_LIMIT),
        name="mem_kv",
    )(mem, g, w)


MIX_TM = 512


def _mix_kernel(x_ref, c_ref, qm_ref, kv_ref, oatt_ref, gate_ref, wc_ref, wa_ref, wm_ref, wo_ref,
                gpost_ref, out_ref):
    scale = MEM_HEAD_DIM ** -0.5
    heads = []
    for h in range(MEM_HEADS):
        lo = h * MEM_HEAD_DIM
        qh = qm_ref[:, lo:lo + MEM_HEAD_DIM]
        kh = kv_ref[0, :, lo:lo + MEM_HEAD_DIM]
        vh = kv_ref[0, :, MEM_WIDTH + lo:MEM_WIDTH + lo + MEM_HEAD_DIM]
        s = lax.dot_general(qh, kh, (((1,), (1,)), ((), ())), preferred_element_type=F32) * scale
        m = jnp.max(s, axis=-1, keepdims=True)
        e = jnp.exp(s - m)
        p = (e * (1.0 / jnp.sum(e, axis=-1, keepdims=True))).astype(BF16)
        heads.append(jnp.dot(p, vh, preferred_element_type=F32).astype(BF16))
    o_mem = jnp.concatenate(heads, axis=-1)
    y_mem = jnp.dot(o_mem, wm_ref[...], preferred_element_type=F32)
    y_conv = jnp.dot(c_ref[...], wc_ref[...], preferred_element_type=F32)
    y_att = jnp.dot(oatt_ref[...], wa_ref[...], preferred_element_type=F32)
    merged = (gate_ref[:, 0:D_MODEL].astype(F32) * y_conv
              + gate_ref[:, D_MODEL:2 * D_MODEL].astype(F32) * y_att
              + gate_ref[:, 2 * D_MODEL:].astype(F32) * y_mem)
    y = jnp.dot(merged.astype(BF16), wo_ref[...], preferred_element_type=F32)
    out_ref[...] = x_ref[...] + _rms(y, gpost_ref[...])


def _mix(x2, c2, main, kv, o_att, gates, wc, wa, wm, wo, gpost, tiles_per_batch):
    T = x2.shape[0]
    M = kv.shape[1]

    def rows(width, col=0):
        return pl.BlockSpec((MIX_TM, width), lambda i: (i, col))

    def whole(shape):
        return pl.BlockSpec(shape, lambda i: (0,) * len(shape))

    return pl.pallas_call(
        _mix_kernel,
        grid=(T // MIX_TM,),
        in_specs=[rows(D_MODEL), rows(CONV_WIDTH),
                  rows(MEM_WIDTH, 2 * CONV_WIDTH // MEM_WIDTH),
                  pl.BlockSpec((1, M, 2 * MEM_WIDTH), lambda i: (i // tiles_per_batch, 0, 0)),
                  rows(ATT_OUT), rows(GATE_WIDTH),
                  whole((CONV_WIDTH, D_MODEL)), whole((ATT_OUT, D_MODEL)),
                  whole((MEM_WIDTH, D_MODEL)), whole((D_MODEL, D_MODEL)), whole((1, D_MODEL))],
        out_specs=rows(D_MODEL),
        out_shape=jax.ShapeDtypeStruct((T, D_MODEL), F32),
        compiler_params=pltpu.CompilerParams(
            dimension_semantics=("parallel",), vmem_limit_bytes=VMEM_LIMIT),
        name="mix_out",
    )(x2, c2, main, kv, o_att, gates, wc, wa, wm, wo, gpost)


FFN_TM = 1024
FFN_TH = 256
FFN_CHUNKS = FFN_HIDDEN // FFN_TH


def _ffn_kernel(x_ref, gpre_ref, wg_ref, wu_ref, wd_ref, gpost_ref, out_ref, h_ref, acc_ref):
    j = pl.program_id(1)

    @pl.when(j == 0)
    def _():
        h_ref[...] = _rms(x_ref[...], gpre_ref[...]).astype(BF16)
        acc_ref[...] = jnp.zeros_like(acc_ref)

    h = h_ref[...]
    gv = jnp.dot(h, wg_ref[...], preferred_element_type=F32)
    uv = jnp.dot(h, wu_ref[...], preferred_element_type=F32)
    a = (gv * _sigmoid(gv) * uv).astype(BF16)
    acc_ref[...] += jnp.dot(a, wd_ref[...], preferred_element_type=F32)

    @pl.when(j == FFN_CHUNKS - 1)
    def _():
        out_ref[...] = x_ref[...] + _rms(acc_ref[...], gpost_ref[...])


def _ffn(x2, gpre, w_in, w_out, gpost):
    T = x2.shape[0]
    return pl.pallas_call(
        _ffn_kernel,
        grid=(T // FFN_TM, FFN_CHUNKS),
        in_specs=[pl.BlockSpec((FFN_TM, D_MODEL), lambda i, j: (i, 0)),
                  pl.BlockSpec((1, D_MODEL), lambda i, j: (0, 0)),
                  pl.BlockSpec((D_MODEL, FFN_TH), lambda i, j: (0, j)),
                  pl.BlockSpec((D_MODEL, FFN_TH), lambda i, j: (0, FFN_CHUNKS + j)),
                  pl.BlockSpec((FFN_TH, D_MODEL), lambda i, j: (j, 0)),
                  pl.BlockSpec((1, D_MODEL), lambda i, j: (0, 0))],
        out_specs=pl.BlockSpec((FFN_TM, D_MODEL), lambda i, j: (i, 0)),
        out_shape=jax.ShapeDtypeStruct((T, D_MODEL), F32),
        scratch_shapes=[pltpu.VMEM((FFN_TM, D_MODEL), BF16), pltpu.VMEM((FFN_TM, D_MODEL), F32)],
        compiler_params=pltpu.CompilerParams(
            dimension_semantics=("parallel", "arbitrary"), vmem_limit_bytes=VMEM_LIMIT),
        name="ffn",
    )(x2, gpre, w_in, w_in, w_out, gpost)


def _permute_in_proj(w):
    c1 = 2 * CONV_WIDTH
    c2 = c1 + N_GROUPS * QKV_WIDTH
    c3 = c2 + MEM_WIDTH
    group_cols = [w[:, c1 + which * N_GROUPS * ATT_OUT + g * ATT_OUT:
                    c1 + which * N_GROUPS * ATT_OUT + (g + 1) * ATT_OUT]
                  for g in range(N_GROUPS) for which in range(3)]
    return jnp.concatenate([w[:, :c1], w[:, c2:c3]] + group_cols + [w[:, c3:]], axis=1)


def kernel(x, mem, rel_bias, norm_mix_pre, w_in, b_gate, conv_dw, conv_dw_bias, conv_ln_g, conv_ln_b,
           w_conv_out, w_att_out, norm_mem, w_mem_kv, w_mem_out, w_out, norm_mix_post, norm_ffn_pre,
           w_ffn_in, w_ffn_out, norm_ffn_post):
    B, S, D = x.shape
    depth = w_in.shape[0]
    T = B * S
    biases = [_band_bias(rel_bias, g, d) for g, d in enumerate(DILATIONS)]

    def row(v):
        return v.reshape(1, -1)

    x2 = x.reshape(T, D)
    for l in range(depth):
        main, qkv0, qkv1, qkv2, gates = _in_proj(
            x2, row(norm_mix_pre[l]), _permute_in_proj(w_in[l]).astype(BF16), row(b_gate[l]), B, S)
        c = _conv(main.reshape(B, S, MAIN_WIDTH), conv_dw[l], row(conv_dw_bias[l]),
                  row(conv_ln_g[l]), row(conv_ln_b[l]))
        o_att = _attention((qkv0, qkv1, qkv2), biases)
        kv = _mem_kv(mem, row(norm_mem[l]), w_mem_kv[l].astype(BF16))
        x2 = _mix(x2, c.reshape(T, CONV_WIDTH), main, kv, o_att.reshape(T, ATT_OUT), gates,
                  w_conv_out[l].astype(BF16), w_att_out[l].astype(BF16), w_mem_out[l].astype(BF16),
                  w_out[l].astype(BF16), row(norm_mix_post[l]), S // MIX_TM)
        x2 = _ffn(x2, row(norm_ffn_pre[l]), w_ffn_in[l].astype(BF16), w_ffn_out[l].astype(BF16),
                  row(norm_ffn_post[l]))
    return x2.reshape(B, S, D)
```

```python
import functools
import math

import numpy as np
import jax
import jax.numpy as jnp
from jax import lax
from jax.experimental import pallas as pl
from jax.experimental.pallas import tpu as pltpu

F32 = jnp.float32
BF16 = jnp.bfloat16

D_MODEL = 1024
CONV_WIDTH = 512
CONV_KSIZE = 31
CONV_PAD = CONV_KSIZE // 2
DILATIONS = (1, 4, 16)
RADIUS = 64
N_GROUPS = 3
HEADS = 4
HEAD_DIM = 64
ATT_OUT = HEADS * HEAD_DIM
QKV_WIDTH = 3 * ATT_OUT
MEM_HEADS = 4
MEM_HEAD_DIM = 128
MEM_WIDTH = 512
FFN_HIDDEN = 2816
NUM_BUCKETS = 32
MAX_DISTANCE = 1024
RMS_EPS = 1e-6
LN_EPS = 1e-5
NEG_INF = -1e30

MAIN_WIDTH = 2 * CONV_WIDTH + MEM_WIDTH
GATE_WIDTH = 3 * D_MODEL
SUBQ = 128
SUBK = SUBQ + 2 * RADIUS
LANES = 128
LSE_LANES = 128
LSE_PER_HEAD = LSE_LANES // HEADS
STAT_LANES = LSE_PER_HEAD // 2
LOG2E = math.log2(math.e)

VMEM_LIMIT = 56 * 1024 * 1024


def _sigmoid(v):
    return 1.0 / (1.0 + jnp.exp(-v))


def _rms(v, g):
    return v * lax.rsqrt(jnp.mean(v * v, axis=-1, keepdims=True) + RMS_EPS) * g


IN_TM = 512
IN_TN = 768
N_MAIN_TILES = MAIN_WIDTH // IN_TN
N_GATE_TILES = GATE_WIDTH // IN_TN
FIRST_GATE_TILE = N_MAIN_TILES + N_GROUPS
assert IN_TN == QKV_WIDTH


def _in_proj_kernel(x_ref, g_ref, w_ref, b_ref, main_ref, q0_ref, q1_ref, q2_ref, gate_ref, acc_ref):
    h = _rms(x_ref[...], g_ref[...]).astype(BF16)

    def project(t):
        return jnp.dot(h, w_ref[:, t * IN_TN:(t + 1) * IN_TN], preferred_element_type=F32)

    for t in range(N_MAIN_TILES):
        main_ref[:, t * IN_TN:(t + 1) * IN_TN] = project(t).astype(BF16)

    q_scale = HEAD_DIM ** -0.5 * LOG2E
    for g, (d, out_ref) in enumerate(zip(DILATIONS, (q0_ref, q1_ref, q2_ref))):
        acc = project(N_MAIN_TILES + g)
        if d == 1:
            out_ref[0, 0, :, 0:ATT_OUT] = (acc[:, 0:ATT_OUT] * q_scale).astype(BF16)
            out_ref[0, 0, :, ATT_OUT:] = acc[:, ATT_OUT:].astype(BF16)
        else:
            for cb in range(IN_TN // LANES):
                blk = acc[:, cb * LANES:(cb + 1) * LANES]
                acc_ref[g - 1, cb] = blk * q_scale if cb < ATT_OUT // LANES else blk
            for r in range(d):
                for cb in range(IN_TN // LANES):
                    out_ref[0, r, :, cb * LANES:(cb + 1) * LANES] = (
                        acc_ref[g - 1, cb, pl.ds(r, IN_TM // d, stride=d), :].astype(BF16))

    for t in range(N_GATE_TILES):
        cols = slice(t * IN_TN, (t + 1) * IN_TN)
        gate_ref[:, cols] = _sigmoid(project(FIRST_GATE_TILE + t) + b_ref[:, cols]).astype(BF16)


def _in_proj(x2, g, w, b, batch, seq):
    T = x2.shape[0]
    tiles_per_batch = seq // IN_TM
    n_strided = sum(d > 1 for d in DILATIONS)

    def qkv_spec(d):
        return pl.BlockSpec((1, d, IN_TM // d, QKV_WIDTH),
                            lambda i: (i // tiles_per_batch, 0, i % tiles_per_batch, 0))

    def resident(shape):
        return pl.BlockSpec(shape, lambda i: (0, 0), pipeline_mode=pl.Buffered(1))

    return pl.pallas_call(
        _in_proj_kernel,
        grid=(T // IN_TM,),
        in_specs=[
            pl.BlockSpec((IN_TM, D_MODEL), lambda i: (i, 0)),
            resident((1, D_MODEL)),
            resident((D_MODEL, w.shape[1])),
            resident((1, GATE_WIDTH)),
        ],
        out_specs=[
            pl.BlockSpec((IN_TM, MAIN_WIDTH), lambda i: (i, 0)),
            qkv_spec(DILATIONS[0]), qkv_spec(DILATIONS[1]), qkv_spec(DILATIONS[2]),
            pl.BlockSpec((IN_TM, GATE_WIDTH), lambda i: (i, 0)),
        ],
        out_shape=[jax.ShapeDtypeStruct((T, MAIN_WIDTH), BF16)]
        + [jax.ShapeDtypeStruct((batch, d, seq // d, QKV_WIDTH), BF16) for d in DILATIONS]
        + [jax.ShapeDtypeStruct((T, GATE_WIDTH), BF16)],
        scratch_shapes=[pltpu.VMEM((n_strided, IN_TN // LANES, IN_TM, LANES), F32)],
        compiler_params=pltpu.CompilerParams(
            dimension_semantics=("parallel",), vmem_limit_bytes=VMEM_LIMIT),
        name="in_proj",
    )(x2, g, w, b)


CONV_TT = 256
CONV_HALO = 16
CONV_RC = 64
SUBLANES = 8
CONV_FIRST = CONV_HALO - CONV_PAD
CONV_SHIFT_ROWS = CONV_TT + (CONV_FIRST + CONV_KSIZE - 1) // SUBLANES * SUBLANES


def _conv_kernel(a_ref, gt_ref, ap_ref, gp_ref, an_ref, gn_ref, w_ref, b_ref, lg_ref, lb_ref,
                 out_ref, u_ref, us_ref):
    i = pl.program_id(1)
    n = pl.num_programs(1)

    def glu(a, g):
        return a.astype(F32) * _sigmoid(g.astype(F32))

    u_ref[CONV_HALO:CONV_HALO + CONV_TT, :] = glu(a_ref[0], gt_ref[0])
    u_ref[0:CONV_HALO, :] = jnp.where(i > 0, glu(ap_ref[0], gp_ref[0]), 0.0)
    u_ref[CONV_HALO + CONV_TT:, :] = jnp.where(i < n - 1, glu(an_ref[0], gn_ref[0]), 0.0)
    for s in range(SUBLANES):
        us_ref[s] = u_ref[s:s + CONV_SHIFT_ROWS, :]

    for c in range(CONV_TT // CONV_RC):
        acc = jnp.zeros((CONV_RC, CONV_WIDTH), F32)
        for k in range(CONV_KSIZE):
            off = CONV_FIRST + k
            r0 = c * CONV_RC + off // SUBLANES * SUBLANES
            acc = acc + us_ref[off % SUBLANES, r0:r0 + CONV_RC, :] * w_ref[k:k + 1, :]
        y = acc + b_ref[...]
        mu = jnp.mean(y, axis=-1, keepdims=True)
        yc = y - mu
        yn = yc * lax.rsqrt(jnp.mean(yc * yc, axis=-1, keepdims=True) + LN_EPS)
        yn = yn * lg_ref[...] + lb_ref[...]
        out_ref[0, c * CONV_RC:(c + 1) * CONV_RC, :] = (yn * _sigmoid(yn)).astype(BF16)


def _conv(main3, w_dw, b_dw, ln_g, ln_b):
    B, S, _ = main3.shape
    nh = CONV_TT // CONV_HALO
    last_h = S // CONV_HALO - 1

    def cur(col):
        return pl.BlockSpec((1, CONV_TT, CONV_WIDTH), lambda b, i: (b, i, col))

    def prev(col):
        return pl.BlockSpec((1, CONV_HALO, CONV_WIDTH),
                            lambda b, i: (b, jnp.maximum(i * nh - 1, 0), col))

    def nxt(col):
        return pl.BlockSpec((1, CONV_HALO, CONV_WIDTH),
                            lambda b, i: (b, jnp.minimum((i + 1) * nh, last_h), col))

    def vec(rows):
        return pl.BlockSpec((rows, CONV_WIDTH), lambda b, i: (0, 0))

    return pl.pallas_call(
        _conv_kernel,
        grid=(B, S // CONV_TT),
        in_specs=[cur(0), cur(1), prev(0), prev(1), nxt(0), nxt(1),
                  vec(CONV_KSIZE), vec(1), vec(1), vec(1)],
        out_specs=pl.BlockSpec((1, CONV_TT, CONV_WIDTH), lambda b, i: (b, i, 0)),
        out_shape=jax.ShapeDtypeStruct((B, S, CONV_WIDTH), BF16),
        scratch_shapes=[pltpu.VMEM((CONV_TT + 2 * CONV_HALO, CONV_WIDTH), F32),
                        pltpu.VMEM((SUBLANES, CONV_SHIFT_ROWS, CONV_WIDTH), F32)],
        compiler_params=pltpu.CompilerParams(
            dimension_semantics=("parallel", "parallel"), vmem_limit_bytes=VMEM_LIMIT),
        name="conv_module",
    )(main3, main3, main3, main3, main3, main3, w_dw, b_dw, ln_g, ln_b)


ATT_TILE = 2048
ATT_COMBINE_ROWS = 256
ATT_ILP = 4


def _att_kernel(c0_ref, p0_ref, n0_ref, c1_ref, p1_ref, n1_ref, c2_ref, p2_ref, n2_ref,
                b0_ref, b1_ref, b2_ref, o_ref, onat_ref, lnat_ref):
    i = pl.program_id(1)
    n = pl.num_programs(1)
    head_of_lane = lax.broadcasted_iota(jnp.int32, (1, ATT_OUT), 1) // HEAD_DIM
    stat_of_lane = lax.broadcasted_iota(jnp.int32, (1, LSE_LANES), 1) // STAT_LANES
    key_col = lax.broadcasted_iota(jnp.int32, (1, SUBK), 1)
    head_masks = [head_of_lane == h for h in range(HEADS)]
    K0, V0 = ATT_OUT, 2 * ATT_OUT

    def subtile(g, d, q, k, v, bias_ref, first, last, row0):
        qs = jnp.concatenate([jnp.where(hm, q, jnp.zeros_like(q)) for hm in head_masks], axis=0)
        s = lax.dot_general(qs, k, (((1,), (1,)), ((), ())), preferred_element_type=F32)
        s = s + bias_ref[...]
        if first:
            s = jnp.where(key_col >= jnp.where(i == 0, RADIUS, 0), s, NEG_INF)
        if last:
            s = jnp.where(key_col < jnp.where(i == n - 1, SUBK - RADIUS, SUBK), s, NEG_INF)
        m = jnp.max(s, axis=-1, keepdims=True)
        e = jnp.exp2(s - m)
        l = jnp.sum(e, axis=-1, keepdims=True)
        o_all = jnp.dot(e.astype(BF16), v, preferred_element_type=F32)
        o = o_all[(HEADS - 1) * SUBQ:]
        stats = jnp.broadcast_to(l[(HEADS - 1) * SUBQ:], (SUBQ, LSE_LANES))
        stats = jnp.where(stat_of_lane == 2 * (HEADS - 1), m[(HEADS - 1) * SUBQ:], stats)
        for h in range(HEADS - 1):
            rows = slice(h * SUBQ, (h + 1) * SUBQ)
            o = jnp.where(head_masks[h], o_all[rows], o)
            stats = jnp.where(stat_of_lane == 2 * h, m[rows], stats)
            stats = jnp.where(stat_of_lane == 2 * h + 1, l[rows], stats)
        lse_tile = stats
        if d == 1:
            if not isinstance(row0, int):
                row0 = pl.multiple_of(row0, SUBQ)
            rows = pl.ds(row0, SUBQ)
        else:
            rows = pl.ds(row0, SUBQ, stride=d)
        for half in range(ATT_OUT // LANES):
            onat_ref[2 * g + half, rows, :] = o[:, half * LANES:(half + 1) * LANES]
        lnat_ref[g, rows, :] = lse_tile

    groups = ((c0_ref, p0_ref, n0_ref, b0_ref), (c1_ref, p1_ref, n1_ref, b1_ref),
              (c2_ref, p2_ref, n2_ref, b2_ref))
    for g, (d, (c_ref, p_ref, n_ref, bias_ref)) in enumerate(zip(DILATIONS, groups)):
        tq = ATT_TILE // d
        nsub = tq // SUBQ

        def residue(r, carry, g=g, d=d, c_ref=c_ref, p_ref=p_ref, n_ref=n_ref, bias_ref=bias_ref,
                    tq=tq, nsub=nsub):
            def kv_cat(parts, col):
                return jnp.concatenate([ref[0, r, rows, col:col + ATT_OUT] for ref, rows in parts], axis=0)

            def sub(j):
                if isinstance(j, int) and (j == 0 or j == nsub - 1):
                    parts = [(c_ref, slice(max(j * SUBQ - RADIUS, 0), min(j * SUBQ + SUBQ + RADIUS, tq)))]
                    if j == 0:
                        parts = [(p_ref, slice(None))] + parts
                    if j == nsub - 1:
                        parts = parts + [(n_ref, slice(None))]
                    k, v = kv_cat(parts, K0), kv_cat(parts, V0)
                    q = c_ref[0, r, j * SUBQ:(j + 1) * SUBQ, 0:ATT_OUT]
                else:
                    q0 = j * SUBQ if isinstance(j, int) else pl.multiple_of(j * SUBQ, SUBQ)
                    k0 = j * SUBQ - RADIUS if isinstance(j, int) else pl.multiple_of(j * SUBQ - RADIUS, RADIUS)
                    q = c_ref[0, r, pl.ds(q0, SUBQ), 0:ATT_OUT]
                    k = c_ref[0, r, pl.ds(k0, SUBK), K0:K0 + ATT_OUT]
                    v = c_ref[0, r, pl.ds(k0, SUBK), V0:V0 + ATT_OUT]
                static = isinstance(j, int)
                subtile(g, d, q, k, v, bias_ref, static and j == 0, static and j == nsub - 1,
                        j * SUBQ * d + r)

            if nsub <= 2 * ATT_ILP:
                for j in range(nsub):
                    sub(j)
            else:
                for j in range(ATT_ILP):
                    sub(j)

                def interior(t, carry2):
                    for u in range(ATT_ILP):
                        sub(ATT_ILP + t * ATT_ILP + u)
                    return carry2

                lax.fori_loop(0, (nsub - 2 * ATT_ILP) // ATT_ILP, interior, 0)
                for j in range(nsub - ATT_ILP, nsub):
                    sub(j)
            return carry

        if d == 1:
            residue(0, 0)
        else:
            lax.fori_loop(0, d, residue, 0, unroll=max(1, ATT_ILP // nsub))

    def expand(w, rows):
        full = jnp.broadcast_to(w[:, 0:1], (rows, ATT_OUT))
        for h in range(1, HEADS):
            col = w[:, h * LSE_PER_HEAD:h * LSE_PER_HEAD + 1]
            full = jnp.where(head_masks[h], jnp.broadcast_to(col, (rows, ATT_OUT)), full)
        return full

    def combine(t, carry):
        rows = pl.ds(pl.multiple_of(t * ATT_COMBINE_ROWS, ATT_COMBINE_ROWS), ATT_COMBINE_ROWS)
        stats = [lnat_ref[g, rows, :] for g in range(N_GROUPS)]
        mx = jnp.maximum(jnp.maximum(stats[0], stats[1]), stats[2])
        a = [jnp.exp2(st - mx) for st in stats]
        sums = [pltpu.roll(st, LSE_LANES - STAT_LANES, 1) for st in stats]
        inv = 1.0 / (a[0] * sums[0] + a[1] * sums[1] + a[2] * sums[2])

        def o_nat(g):
            return jnp.concatenate([onat_ref[2 * g, rows, :], onat_ref[2 * g + 1, rows, :]], axis=1)

        o = (expand(a[0] * inv, ATT_COMBINE_ROWS) * o_nat(0)
             + expand(a[1] * inv, ATT_COMBINE_ROWS) * o_nat(1)
             + expand(a[2] * inv, ATT_COMBINE_ROWS) * o_nat(2))
        o_ref[0, rows, :] = o.astype(BF16)
        return carry

    lax.fori_loop(0, ATT_TILE // ATT_COMBINE_ROWS, combine, 0)


def _attention(qkv, biases):
    B = qkv[0].shape[0]
    S = qkv[0].shape[2]
    in_specs = []
    for d in DILATIONS:
        tq = ATT_TILE // d
        nb = tq // RADIUS
        last_b = S // d // RADIUS - 1
        in_specs += [
            pl.BlockSpec((1, d, tq, QKV_WIDTH), lambda b, i: (b, 0, i, 0)),
            pl.BlockSpec((1, d, RADIUS, QKV_WIDTH),
                         lambda b, i, nb=nb: (b, 0, jnp.maximum(i * nb - 1, 0), 0)),
            pl.BlockSpec((1, d, RADIUS, QKV_WIDTH),
                         lambda b, i, nb=nb, last_b=last_b: (b, 0, jnp.minimum((i + 1) * nb, last_b), 0)),
        ]
    in_specs += [pl.BlockSpec((HEADS * SUBQ, SUBK), lambda b, i: (0, 0))] * N_GROUPS
    args = []
    for a in qkv:
        args += [a, a, a]
    return pl.pallas_call(
        _att_kernel,
        grid=(B, S // ATT_TILE),
        in_specs=in_specs,
        out_specs=pl.BlockSpec((1, ATT_TILE, ATT_OUT), lambda b, i: (b, i, 0)),
        out_shape=jax.ShapeDtypeStruct((B, S, ATT_OUT), BF16),
        scratch_shapes=[pltpu.VMEM((N_GROUPS * (ATT_OUT // LANES), ATT_TILE, LANES), F32),
                        pltpu.VMEM((N_GROUPS, ATT_TILE, LSE_LANES), F32)],
        compiler_params=pltpu.CompilerParams(
            dimension_semantics=("parallel", "parallel"), vmem_limit_bytes=VMEM_LIMIT),
        name="dilated_att",
    )(*args, *biases)


def _t5_bucket_np(rel):
    nb = NUM_BUCKETS // 2
    max_exact = nb // 2
    ret = np.where(rel > 0, nb, 0)
    n = np.abs(rel)
    nf = np.maximum(n, 1).astype(np.float32)
    ratio = np.log(nf / np.float32(max_exact)) / np.float32(math.log(MAX_DISTANCE / max_exact))
    large = max_exact + (ratio * np.float32(nb - max_exact)).astype(np.int32)
    large = np.minimum(large, nb - 1)
    return ret + np.where(n < max_exact, n, large)


def _band_bias(rel_bias, g, dilation):
    period = SUBQ + SUBK
    nband = 2 * RADIUS + 1
    bucket = _t5_bucket_np((np.arange(nband) - RADIUS) * dilation)
    onehot = np.zeros((period, NUM_BUCKETS), np.float32)
    onehot[np.arange(nband), bucket] = 1.0
    tab = rel_bias[:, g * HEADS:(g + 1) * HEADS].astype(F32)
    t = jnp.dot(jnp.asarray(onehot), tab, precision=lax.Precision.HIGHEST)
    t = jnp.where((np.arange(period) < nband)[:, None], t, NEG_INF).T
    skew = jnp.tile(t, (1, SUBQ))[:, :SUBQ * (period - 1)].reshape(HEADS, SUBQ, period - 1)
    return skew[:, :, :SUBK].reshape(HEADS * SUBQ, SUBK) * LOG2E


def _mem_kv_kernel(mem_ref, g_ref, w_ref, kv_ref):
    h = _rms(mem_ref[0], g_ref[...]).astype(BF16)
    kv_ref[0] = jnp.dot(h, w_ref[...], preferred_element_type=F32).astype(BF16)


def _mem_kv(mem, g, w):
    B, M, _ = mem.shape
    return pl.pallas_call(
        _mem_kv_kernel,
        grid=(B,),
        in_specs=[pl.BlockSpec((1, M, D_MODEL), lambda b: (b, 0, 0)),
                  pl.BlockSpec((1, D_MODEL), lambda b: (0, 0)),
                  pl.BlockSpec((D_MODEL, 2 * MEM_WIDTH), lambda b: (0, 0))],
        out_specs=pl.BlockSpec((1, M, 2 * MEM_WIDTH), lambda b: (b, 0, 0)),
        out_shape=jax.ShapeDtypeStruct((B, M, 2 * MEM_WIDTH), BF16),
        compiler_params=pltpu.CompilerParams(
            dimension_semantics=("parallel",), vmem_limit_bytes=VMEM_LIMIT),
        name="mem_kv",
    )(mem, g, w)


MIX_TM = 512


def _mix_kernel(x_ref, c_ref, qm_ref, kv_ref, oatt_ref, gate_ref, wc_ref, wa_ref, wm_ref, wo_ref,
                gpost_ref, out_ref):
    scale = MEM_HEAD_DIM ** -0.5
    heads = []
    for h in range(MEM_HEADS):
        lo = h * MEM_HEAD_DIM
        qh = qm_ref[:, lo:lo + MEM_HEAD_DIM]
        kh = kv_ref[0, :, lo:lo + MEM_HEAD_DIM]
        vh = kv_ref[0, :, MEM_WIDTH + lo:MEM_WIDTH + lo + MEM_HEAD_DIM]
        s = lax.dot_general(qh, kh, (((1,), (1,)), ((), ())), preferred_element_type=F32) * scale
        m = jnp.max(s, axis=-1, keepdims=True)
        e = jnp.exp(s - m)
        p = (e * (1.0 / jnp.sum(e, axis=-1, keepdims=True))).astype(BF16)
        heads.append(jnp.dot(p, vh, preferred_element_type=F32).astype(BF16))
    o_mem = jnp.concatenate(heads, axis=-1)
    y_mem = jnp.dot(o_mem, wm_ref[...], preferred_element_type=F32)
    y_conv = jnp.dot(c_ref[...], wc_ref[...], preferred_element_type=F32)
    y_att = jnp.dot(oatt_ref[...], wa_ref[...], preferred_element_type=F32)
    merged = (gate_ref[:, 0:D_MODEL].astype(F32) * y_conv
              + gate_ref[:, D_MODEL:2 * D_MODEL].astype(F32) * y_att
              + gate_ref[:, 2 * D_MODEL:].astype(F32) * y_mem)
    y = jnp.dot(merged.astype(BF16), wo_ref[...], preferred_element_type=F32)
    out_ref[...] = x_ref[...] + _rms(y, gpost_ref[...])


def _mix(x2, c2, main, kv, o_att, gates, wc, wa, wm, wo, gpost, tiles_per_batch):
    T = x2.shape[0]
    M = kv.shape[1]

    def rows(width, col=0):
        return pl.BlockSpec((MIX_TM, width), lambda i: (i, col))

    def whole(shape):
        return pl.BlockSpec(shape, lambda i: (0,) * len(shape))

    return pl.pallas_call(
        _mix_kernel,
        grid=(T // MIX_TM,),
        in_specs=[rows(D_MODEL), rows(CONV_WIDTH),
                  rows(MEM_WIDTH, 2 * CONV_WIDTH // MEM_WIDTH),
                  pl.BlockSpec((1, M, 2 * MEM_WIDTH), lambda i: (i // tiles_per_batch, 0, 0)),
                  rows(ATT_OUT), rows(GATE_WIDTH),
                  whole((CONV_WIDTH, D_MODEL)), whole((ATT_OUT, D_MODEL)),
                  whole((MEM_WIDTH, D_MODEL)), whole((D_MODEL, D_MODEL)), whole((1, D_MODEL))],
        out_specs=rows(D_MODEL),
        out_shape=jax.ShapeDtypeStruct((T, D_MODEL), F32),
        compiler_params=pltpu.CompilerParams(
            dimension_semantics=("parallel",), vmem_limit_bytes=VMEM_LIMIT),
        name="mix_out",
    )(x2, c2, main, kv, o_att, gates, wc, wa, wm, wo, gpost)


FFN_TM = 512
FFN_TH = 256
FFN_CHUNKS = FFN_HIDDEN // FFN_TH


def _ffn_kernel(x_ref, gpre_ref, win_ref, wout_ref, gpost_ref, out_ref):
    x = x_ref[...]
    h = _rms(x, gpre_ref[...]).astype(BF16)
    acc = None
    for c in range(FFN_CHUNKS):
        lo = c * FFN_TH
        gv = jnp.dot(h, win_ref[:, lo:lo + FFN_TH], preferred_element_type=F32)
        uv = jnp.dot(h, win_ref[:, FFN_HIDDEN + lo:FFN_HIDDEN + lo + FFN_TH], preferred_element_type=F32)
        a = (gv * _sigmoid(gv) * uv).astype(BF16)
        part = jnp.dot(a, wout_ref[lo:lo + FFN_TH, :], preferred_element_type=F32)
        acc = part if acc is None else acc + part
    out_ref[...] = x + _rms(acc, gpost_ref[...])


def _ffn(x2, gpre, w_in, w_out, gpost):
    T = x2.shape[0]

    def resident(shape):
        return pl.BlockSpec(shape, lambda i: (0, 0), pipeline_mode=pl.Buffered(1))

    return pl.pallas_call(
        _ffn_kernel,
        grid=(T // FFN_TM,),
        in_specs=[pl.BlockSpec((FFN_TM, D_MODEL), lambda i: (i, 0)),
                  resident((1, D_MODEL)),
                  resident((D_MODEL, 2 * FFN_HIDDEN)),
                  resident((FFN_HIDDEN, D_MODEL)),
                  resident((1, D_MODEL))],
        out_specs=pl.BlockSpec((FFN_TM, D_MODEL), lambda i: (i, 0)),
        out_shape=jax.ShapeDtypeStruct((T, D_MODEL), F32),
        compiler_params=pltpu.CompilerParams(
            dimension_semantics=("parallel",), vmem_limit_bytes=VMEM_LIMIT),
        name="ffn",
    )(x2, gpre, w_in, w_out, gpost)


def _permute_in_proj(w):
    c1 = 2 * CONV_WIDTH
    c2 = c1 + N_GROUPS * QKV_WIDTH
    c3 = c2 + MEM_WIDTH
    group_cols = [w[:, c1 + which * N_GROUPS * ATT_OUT + g * ATT_OUT:
                    c1 + which * N_GROUPS * ATT_OUT + (g + 1) * ATT_OUT]
                  for g in range(N_GROUPS) for which in range(3)]
    return jnp.concatenate([w[:, :c1], w[:, c2:c3]] + group_cols + [w[:, c3:]], axis=1)


def kernel(x, mem, rel_bias, norm_mix_pre, w_in, b_gate, conv_dw, conv_dw_bias, conv_ln_g, conv_ln_b,
           w_conv_out, w_att_out, norm_mem, w_mem_kv, w_mem_out, w_out, norm_mix_post, norm_ffn_pre,
           w_ffn_in, w_ffn_out, norm_ffn_post):
    B, S, D = x.shape
    depth = w_in.shape[0]
    T = B * S
    biases = [_band_bias(rel_bias, g, d) for g, d in enumerate(DILATIONS)]

    def row(v):
        return v.reshape(1, -1)

    x2 = x.reshape(T, D)
    for l in range(depth):
        main, qkv0, qkv1, qkv2, gates = _in_proj(
            x2, row(norm_mix_pre[l]), _permute_in_proj(w_in[l]).astype(BF16), row(b_gate[l]), B, S)
        c = _conv(main.reshape(B, S, MAIN_WIDTH), conv_dw[l], row(conv_dw_bias[l]),
                  row(conv_ln_g[l]), row(conv_ln_b[l]))
        o_att = _attention((qkv0, qkv1, qkv2), biases)
        kv = _mem_kv(mem, row(norm_mem[l]), w_mem_kv[l].astype(BF16))
        x2 = _mix(x2, c.reshape(T, CONV_WIDTH), main, kv, o_att.reshape(T, ATT_OUT), gates,
                  w_conv_out[l].astype(BF16), w_att_out[l].astype(BF16), w_mem_out[l].astype(BF16),
                  w_out[l].astype(BF16), row(norm_mix_post[l]), S // MIX_TM)
        x2 = _ffn(x2, row(norm_ffn_pre[l]), w_ffn_in[l].astype(BF16), w_ffn_out[l].astype(BF16),
                  row(norm_ffn_post[l]))
    return x2.reshape(B, S, D)
```

```python
import functools
import math

import numpy as np
import jax
import jax.numpy as jnp
from jax import lax
from jax.experimental import pallas as pl
from jax.experimental.pallas import tpu as pltpu

F32 = jnp.float32
BF16 = jnp.bfloat16

D_MODEL = 1024
CONV_WIDTH = 512
CONV_KSIZE = 31
CONV_PAD = CONV_KSIZE // 2
DILATIONS = (1, 4, 16)
RADIUS = 64
N_GROUPS = 3
HEADS = 4
HEAD_DIM = 64
ATT_OUT = HEADS * HEAD_DIM
QKV_WIDTH = 3 * ATT_OUT
MEM_HEADS = 4
MEM_HEAD_DIM = 128
MEM_WIDTH = 512
FFN_HIDDEN = 2816
NUM_BUCKETS = 32
MAX_DISTANCE = 1024
RMS_EPS = 1e-6
LN_EPS = 1e-5
NEG_INF = -1e30

GATE_WIDTH = 3 * D_MODEL
SUBQ = 128
SUBK = SUBQ + 2 * RADIUS
LANES = 128
LSE_LANES = 128
LSE_PER_HEAD = LSE_LANES // HEADS
STAT_LANES = LSE_PER_HEAD // 2
LOG2E = math.log2(math.e)

VMEM_LIMIT = 56 * 1024 * 1024


def _sigmoid(v):
    return 1.0 / (1.0 + jnp.exp(-v))


def _rms(v, g):
    return v * lax.rsqrt(jnp.mean(v * v, axis=-1, keepdims=True) + RMS_EPS) * g


IN_TM = 512
IN_TN = 768
AG_WIDTH = 2 * CONV_WIDTH
QM_COL = AG_WIDTH
QKV_COL = QM_COL + MEM_WIDTH
GATE_COL = QKV_COL + N_GROUPS * QKV_WIDTH
N_GATE_TILES = GATE_WIDTH // IN_TN
assert IN_TN == QKV_WIDTH
SUBLANES = 8
CONV_HALO = 16
CONV_HALF = 256
CONV_RC = 64
CONV_FIRST = CONV_HALO - CONV_PAD
CONV_SHIFT_ROWS = CONV_HALF + (CONV_FIRST + CONV_KSIZE - 1) // SUBLANES * SUBLANES
IN_EXT = IN_TM + 2 * CONV_HALO


def _in_proj_kernel(x_ref, xp_ref, xn_ref, g_ref, w_ref, b_ref, cw_ref, cb_ref, lg_ref, lb_ref,
                    c_ref, qm_ref, q0_ref, q1_ref, q2_ref, gate_ref, acc_ref, u_ref, us_ref,
                    *, tiles_per_batch):
    ib = pl.program_id(0) % tiles_per_batch
    x_ext = jnp.concatenate([xp_ref[...], x_ref[...], xn_ref[...]], axis=0)
    h_ext = _rms(x_ext, g_ref[...]).astype(BF16)
    h = h_ext[CONV_HALO:CONV_HALO + IN_TM]

    ag = jnp.dot(h_ext, w_ref[:, 0:AG_WIDTH], preferred_element_type=F32)
    u = ag[:, 0:CONV_WIDTH] * _sigmoid(ag[:, CONV_WIDTH:])
    u_ref[0:CONV_HALO, :] = jnp.where(ib > 0, u[0:CONV_HALO], 0.0)
    u_ref[CONV_HALO:CONV_HALO + IN_TM, :] = u[CONV_HALO:CONV_HALO + IN_TM]
    u_ref[CONV_HALO + IN_TM:, :] = jnp.where(ib < tiles_per_batch - 1, u[CONV_HALO + IN_TM:], 0.0)
    for half in range(IN_TM // CONV_HALF):
        for s in range(SUBLANES):
            us_ref[s] = u_ref[half * CONV_HALF + s:half * CONV_HALF + s + CONV_SHIFT_ROWS, :]
        for c in range(CONV_HALF // CONV_RC):
            acc = jnp.zeros((CONV_RC, CONV_WIDTH), F32)
            for k in range(CONV_KSIZE):
                off = CONV_FIRST + k
                r0 = c * CONV_RC + off // SUBLANES * SUBLANES
                acc = acc + us_ref[off % SUBLANES, r0:r0 + CONV_RC, :] * cw_ref[k:k + 1, :]
            y = acc + cb_ref[...]
            mu = jnp.mean(y, axis=-1, keepdims=True)
            yc = y - mu
            yn = yc * lax.rsqrt(jnp.mean(yc * yc, axis=-1, keepdims=True) + LN_EPS)
            yn = yn * lg_ref[...] + lb_ref[...]
            rows = slice(half * CONV_HALF + c * CONV_RC, half * CONV_HALF + (c + 1) * CONV_RC)
            c_ref[rows, :] = (yn * _sigmoid(yn)).astype(BF16)

    def project(col, width):
        return jnp.dot(h, w_ref[:, col:col + width], preferred_element_type=F32)

    qm_ref[...] = project(QM_COL, MEM_WIDTH).astype(BF16)

    q_scale = HEAD_DIM ** -0.5 * LOG2E
    for g, (d, out_ref) in enumerate(zip(DILATIONS, (q0_ref, q1_ref, q2_ref))):
        acc = project(QKV_COL + g * QKV_WIDTH, QKV_WIDTH)
        if d == 1:
            out_ref[0, 0, :, 0:ATT_OUT] = (acc[:, 0:ATT_OUT] * q_scale).astype(BF16)
            out_ref[0, 0, :, ATT_OUT:] = acc[:, ATT_OUT:].astype(BF16)
        else:
            for cb in range(QKV_WIDTH // LANES):
                blk = acc[:, cb * LANES:(cb + 1) * LANES]
                acc_ref[g - 1, cb] = blk * q_scale if cb < ATT_OUT // LANES else blk
            for r in range(d):
                for cb in range(QKV_WIDTH // LANES):
                    out_ref[0, r, :, cb * LANES:(cb + 1) * LANES] = (
                        acc_ref[g - 1, cb, pl.ds(r, IN_TM // d, stride=d), :].astype(BF16))

    for t in range(N_GATE_TILES):
        cols = slice(t * IN_TN, (t + 1) * IN_TN)
        gate_ref[:, cols] = _sigmoid(project(GATE_COL + t * IN_TN, IN_TN) + b_ref[:, cols]).astype(BF16)


def _in_proj(x2, g, w, b, conv_w, conv_b, ln_g, ln_b, batch, seq):
    T = x2.shape[0]
    tiles_per_batch = seq // IN_TM
    n_strided = sum(d > 1 for d in DILATIONS)
    halo_per_tile = IN_TM // CONV_HALO
    last_halo = T // CONV_HALO - 1

    def qkv_spec(d):
        return pl.BlockSpec((1, d, IN_TM // d, QKV_WIDTH),
                            lambda i: (i // tiles_per_batch, 0, i % tiles_per_batch, 0))

    def resident(shape):
        return pl.BlockSpec(shape, lambda i: (0, 0), pipeline_mode=pl.Buffered(1))

    def rows(width):
        return pl.BlockSpec((IN_TM, width), lambda i: (i, 0))

    return pl.pallas_call(
        functools.partial(_in_proj_kernel, tiles_per_batch=tiles_per_batch),
        grid=(T // IN_TM,),
        in_specs=[
            rows(D_MODEL),
            pl.BlockSpec((CONV_HALO, D_MODEL), lambda i: (jnp.maximum(i * halo_per_tile - 1, 0), 0)),
            pl.BlockSpec((CONV_HALO, D_MODEL), lambda i: (jnp.minimum((i + 1) * halo_per_tile, last_halo), 0)),
            resident((1, D_MODEL)),
            resident((D_MODEL, w.shape[1])),
            resident((1, GATE_WIDTH)),
            resident((CONV_KSIZE, CONV_WIDTH)), resident((1, CONV_WIDTH)),
            resident((1, CONV_WIDTH)), resident((1, CONV_WIDTH)),
        ],
        out_specs=[rows(CONV_WIDTH), rows(MEM_WIDTH),
                   qkv_spec(DILATIONS[0]), qkv_spec(DILATIONS[1]), qkv_spec(DILATIONS[2]),
                   rows(GATE_WIDTH)],
        out_shape=[jax.ShapeDtypeStruct((T, CONV_WIDTH), BF16), jax.ShapeDtypeStruct((T, MEM_WIDTH), BF16)]
        + [jax.ShapeDtypeStruct((batch, d, seq // d, QKV_WIDTH), BF16) for d in DILATIONS]
        + [jax.ShapeDtypeStruct((T, GATE_WIDTH), BF16)],
        scratch_shapes=[pltpu.VMEM((n_strided, QKV_WIDTH // LANES, IN_TM, LANES), F32),
                        pltpu.VMEM((IN_EXT, CONV_WIDTH), F32),
                        pltpu.VMEM((SUBLANES, CONV_SHIFT_ROWS, CONV_WIDTH), F32)],
        compiler_params=pltpu.CompilerParams(
            dimension_semantics=("parallel",), vmem_limit_bytes=VMEM_LIMIT),
        name="in_proj",
    )(x2, x2, x2, g, w, b, conv_w, conv_b, ln_g, ln_b)


ATT_TILE = 2048
ATT_COMBINE_ROWS = 256
ATT_ILP = 4


def _att_kernel(c0_ref, p0_ref, n0_ref, c1_ref, p1_ref, n1_ref, c2_ref, p2_ref, n2_ref,
                b0_ref, b1_ref, b2_ref, o_ref, onat_ref, lnat_ref):
    i = pl.program_id(1)
    n = pl.num_programs(1)
    head_of_lane = lax.broadcasted_iota(jnp.int32, (1, ATT_OUT), 1) // HEAD_DIM
    stat_of_lane = lax.broadcasted_iota(jnp.int32, (1, LSE_LANES), 1) // STAT_LANES
    key_col = lax.broadcasted_iota(jnp.int32, (1, SUBK), 1)
    head_masks = [head_of_lane == h for h in range(HEADS)]
    K0, V0 = ATT_OUT, 2 * ATT_OUT

    def subtile(g, d, q, k, v, bias_ref, first, last, row0):
        qs = jnp.concatenate([jnp.where(hm, q, jnp.zeros_like(q)) for hm in head_masks], axis=0)
        s = lax.dot_general(qs, k, (((1,), (1,)), ((), ())), preferred_element_type=F32)
        s = s + bias_ref[...]
        if first:
            s = jnp.where(key_col >= jnp.where(i == 0, RADIUS, 0), s, NEG_INF)
        if last:
            s = jnp.where(key_col < jnp.where(i == n - 1, SUBK - RADIUS, SUBK), s, NEG_INF)
        m = jnp.max(s, axis=-1, keepdims=True)
        e = jnp.exp2(s - m)
        l = jnp.sum(e, axis=-1, keepdims=True)
        o_all = jnp.dot(e.astype(BF16), v, preferred_element_type=F32)
        o = o_all[(HEADS - 1) * SUBQ:]
        stats = jnp.broadcast_to(l[(HEADS - 1) * SUBQ:], (SUBQ, LSE_LANES))
        stats = jnp.where(stat_of_lane == 2 * (HEADS - 1), m[(HEADS - 1) * SUBQ:], stats)
        for h in range(HEADS - 1):
            rows = slice(h * SUBQ, (h + 1) * SUBQ)
            o = jnp.where(head_masks[h], o_all[rows], o)
            stats = jnp.where(stat_of_lane == 2 * h, m[rows], stats)
            stats = jnp.where(stat_of_lane == 2 * h + 1, l[rows], stats)
        if d == 1:
            if not isinstance(row0, int):
                row0 = pl.multiple_of(row0, SUBQ)
            rows = pl.ds(row0, SUBQ)
        else:
            rows = pl.ds(row0, SUBQ, stride=d)
        for half in range(ATT_OUT // LANES):
            onat_ref[2 * g + half, rows, :] = o[:, half * LANES:(half + 1) * LANES]
        lnat_ref[g, rows, :] = stats

    groups = ((c0_ref, p0_ref, n0_ref, b0_ref), (c1_ref, p1_ref, n1_ref, b1_ref),
              (c2_ref, p2_ref, n2_ref, b2_ref))
    for g, (d, (c_ref, p_ref, n_ref, bias_ref)) in enumerate(zip(DILATIONS, groups)):
        tq = ATT_TILE // d
        nsub = tq // SUBQ

        def residue(r, carry, g=g, d=d, c_ref=c_ref, p_ref=p_ref, n_ref=n_ref, bias_ref=bias_ref,
                    tq=tq, nsub=nsub):
            def kv_cat(parts, col):
                return jnp.concatenate([ref[0, r, rows, col:col + ATT_OUT] for ref, rows in parts], axis=0)

            def sub(j):
                if isinstance(j, int) and (j == 0 or j == nsub - 1):
                    parts = [(c_ref, slice(max(j * SUBQ - RADIUS, 0), min(j * SUBQ + SUBQ + RADIUS, tq)))]
                    if j == 0:
                        parts = [(p_ref, slice(None))] + parts
                    if j == nsub - 1:
                        parts = parts + [(n_ref, slice(None))]
                    k, v = kv_cat(parts, K0), kv_cat(parts, V0)
                    q = c_ref[0, r, j * SUBQ:(j + 1) * SUBQ, 0:ATT_OUT]
                else:
                    q0 = j * SUBQ if isinstance(j, int) else pl.multiple_of(j * SUBQ, SUBQ)
                    k0 = j * SUBQ - RADIUS if isinstance(j, int) else pl.multiple_of(j * SUBQ - RADIUS, RADIUS)
                    q = c_ref[0, r, pl.ds(q0, SUBQ), 0:ATT_OUT]
                    k = c_ref[0, r, pl.ds(k0, SUBK), K0:K0 + ATT_OUT]
                    v = c_ref[0, r, pl.ds(k0, SUBK), V0:V0 + ATT_OUT]
                static = isinstance(j, int)
                subtile(g, d, q, k, v, bias_ref, static and j == 0, static and j == nsub - 1,
                        j * SUBQ * d + r)

            if nsub <= 2 * ATT_ILP:
                for j in range(nsub):
                    sub(j)
            else:
                for j in range(ATT_ILP):
                    sub(j)

                def interior(t, carry2):
                    for u in range(ATT_ILP):
                        sub(ATT_ILP + t * ATT_ILP + u)
                    return carry2

                lax.fori_loop(0, (nsub - 2 * ATT_ILP) // ATT_ILP, interior, 0)
                for j in range(nsub - ATT_ILP, nsub):
                    sub(j)
            return carry

        if d == 1:
            residue(0, 0)
        else:
            lax.fori_loop(0, d, residue, 0, unroll=max(1, ATT_ILP // nsub))

    def expand(w, rows):
        full = jnp.broadcast_to(w[:, 0:1], (rows, ATT_OUT))
        for h in range(1, HEADS):
            col = w[:, h * LSE_PER_HEAD:h * LSE_PER_HEAD + 1]
            full = jnp.where(head_masks[h], jnp.broadcast_to(col, (rows, ATT_OUT)), full)
        return full

    def combine(t, carry):
        rows = pl.ds(pl.multiple_of(t * ATT_COMBINE_ROWS, ATT_COMBINE_ROWS), ATT_COMBINE_ROWS)
        stats = [lnat_ref[g, rows, :] for g in range(N_GROUPS)]
        mx = jnp.maximum(jnp.maximum(stats[0], stats[1]), stats[2])
        a = [jnp.exp2(st - mx) for st in stats]
        sums = [pltpu.roll(st, LSE_LANES - STAT_LANES, 1) for st in stats]
        inv = 1.0 / (a[0] * sums[0] + a[1] * sums[1] + a[2] * sums[2])

        def o_nat(g):
            return jnp.concatenate([onat_ref[2 * g, rows, :], onat_ref[2 * g + 1, rows, :]], axis=1)

        o = (expand(a[0] * inv, ATT_COMBINE_ROWS) * o_nat(0)
             + expand(a[1] * inv, ATT_COMBINE_ROWS) * o_nat(1)
             + expand(a[2] * inv, ATT_COMBINE_ROWS) * o_nat(2))
        o_ref[0, rows, :] = o.astype(BF16)
        return carry

    lax.fori_loop(0, ATT_TILE // ATT_COMBINE_ROWS, combine, 0)


def _attention(qkv, biases):
    B = qkv[0].shape[0]
    S = qkv[0].shape[2]
    in_specs = []
    for d in DILATIONS:
        tq = ATT_TILE // d
        nb = tq // RADIUS
        last_b = S // d // RADIUS - 1
        in_specs += [
            pl.BlockSpec((1, d, tq, QKV_WIDTH), lambda b, i: (b, 0, i, 0)),
            pl.BlockSpec((1, d, RADIUS, QKV_WIDTH),
                         lambda b, i, nb=nb: (b, 0, jnp.maximum(i * nb - 1, 0), 0)),
            pl.BlockSpec((1, d, RADIUS, QKV_WIDTH),
                         lambda b, i, nb=nb, last_b=last_b: (b, 0, jnp.minimum((i + 1) * nb, last_b), 0)),
        ]
    in_specs += [pl.BlockSpec((HEADS * SUBQ, SUBK), lambda b, i: (0, 0))] * N_GROUPS
    args = []
    for a in qkv:
        args += [a, a, a]
    return pl.pallas_call(
        _att_kernel,
        grid=(B, S // ATT_TILE),
        in_specs=in_specs,
        out_specs=pl.BlockSpec((1, ATT_TILE, ATT_OUT), lambda b, i: (b, i, 0)),
        out_shape=jax.ShapeDtypeStruct((B, S, ATT_OUT), BF16),
        scratch_shapes=[pltpu.VMEM((N_GROUPS * (ATT_OUT // LANES), ATT_TILE, LANES), F32),
                        pltpu.VMEM((N_GROUPS, ATT_TILE, LSE_LANES), F32)],
        compiler_params=pltpu.CompilerParams(
            dimension_semantics=("parallel", "parallel"), vmem_limit_bytes=VMEM_LIMIT),
        name="dilated_att",
    )(*args, *biases)


def _t5_bucket_np(rel):
    nb = NUM_BUCKETS // 2
    max_exact = nb // 2
    ret = np.where(rel > 0, nb, 0)
    n = np.abs(rel)
    nf = np.maximum(n, 1).astype(np.float32)
    ratio = np.log(nf / np.float32(max_exact)) / np.float32(math.log(MAX_DISTANCE / max_exact))
    large = max_exact + (ratio * np.float32(nb - max_exact)).astype(np.int32)
    large = np.minimum(large, nb - 1)
    return ret + np.where(n < max_exact, n, large)


def _band_bias(rel_bias, g, dilation):
    period = SUBQ + SUBK
    nband = 2 * RADIUS + 1
    bucket = _t5_bucket_np((np.arange(nband) - RADIUS) * dilation)
    onehot = np.zeros((period, NUM_BUCKETS), np.float32)
    onehot[np.arange(nband), bucket] = 1.0
    tab = rel_bias[:, g * HEADS:(g + 1) * HEADS].astype(F32)
    t = jnp.dot(jnp.asarray(onehot), tab, precision=lax.Precision.HIGHEST)
    t = jnp.where((np.arange(period) < nband)[:, None], t, NEG_INF).T
    skew = jnp.tile(t, (1, SUBQ))[:, :SUBQ * (period - 1)].reshape(HEADS, SUBQ, period - 1)
    return skew[:, :, :SUBK].reshape(HEADS * SUBQ, SUBK) * LOG2E


def _mem_kv_kernel(mem_ref, g_ref, w_ref, kv_ref):
    h = _rms(mem_ref[0], g_ref[...]).astype(BF16)
    kv_ref[0] = jnp.dot(h, w_ref[...], preferred_element_type=F32).astype(BF16)


def _mem_kv(mem, g, w):
    B, M, _ = mem.shape
    return pl.pallas_call(
        _mem_kv_kernel,
        grid=(B,),
        in_specs=[pl.BlockSpec((1, M, D_MODEL), lambda b: (b, 0, 0)),
                  pl.BlockSpec((1, D_MODEL), lambda b: (0, 0)),
                  pl.BlockSpec((D_MODEL, 2 * MEM_WIDTH), lambda b: (0, 0))],
        out_specs=pl.BlockSpec((1, M, 2 * MEM_WIDTH), lambda b: (b, 0, 0)),
        out_shape=jax.ShapeDtypeStruct((B, M, 2 * MEM_WIDTH), BF16),
        compiler_params=pltpu.CompilerParams(
            dimension_semantics=("parallel",), vmem_limit_bytes=VMEM_LIMIT),
        name="mem_kv",
    )(mem, g, w)


MIX_TM = 512


def _mix_kernel(x_ref, c_ref, qm_ref, kv_ref, oatt_ref, gate_ref, wc_ref, wa_ref, wm_ref, wo_ref,
                gpost_ref, out_ref):
    scale = MEM_HEAD_DIM ** -0.5
    heads = []
    for h in range(MEM_HEADS):
        lo = h * MEM_HEAD_DIM
        qh = qm_ref[:, lo:lo + MEM_HEAD_DIM]
        kh = kv_ref[0, :, lo:lo + MEM_HEAD_DIM]
        vh = kv_ref[0, :, MEM_WIDTH + lo:MEM_WIDTH + lo + MEM_HEAD_DIM]
        s = lax.dot_general(qh, kh, (((1,), (1,)), ((), ())), preferred_element_type=F32) * scale
        m = jnp.max(s, axis=-1, keepdims=True)
        e = jnp.exp(s - m)
        p = (e * (1.0 / jnp.sum(e, axis=-1, keepdims=True))).astype(BF16)
        heads.append(jnp.dot(p, vh, preferred_element_type=F32).astype(BF16))
    o_mem = jnp.concatenate(heads, axis=-1)
    y_mem = jnp.dot(o_mem, wm_ref[...], preferred_element_type=F32)
    y_conv = jnp.dot(c_ref[...], wc_ref[...], preferred_element_type=F32)
    y_att = jnp.dot(oatt_ref[...], wa_ref[...], preferred_element_type=F32)
    merged = (gate_ref[:, 0:D_MODEL].astype(F32) * y_conv
              + gate_ref[:, D_MODEL:2 * D_MODEL].astype(F32) * y_att
              + gate_ref[:, 2 * D_MODEL:].astype(F32) * y_mem)
    y = jnp.dot(merged.astype(BF16), wo_ref[...], preferred_element_type=F32)
    out_ref[...] = x_ref[...] + _rms(y, gpost_ref[...])


def _mix(x2, c2, qm, kv, o_att, gates, wc, wa, wm, wo, gpost, tiles_per_batch):
    T = x2.shape[0]
    M = kv.shape[1]

    def rows(width):
        return pl.BlockSpec((MIX_TM, width), lambda i: (i, 0))

    def whole(shape):
        return pl.BlockSpec(shape, lambda i: (0,) * len(shape))

    return pl.pallas_call(
        _mix_kernel,
        grid=(T // MIX_TM,),
        in_specs=[rows(D_MODEL), rows(CONV_WIDTH), rows(MEM_WIDTH),
                  pl.BlockSpec((1, M, 2 * MEM_WIDTH), lambda i: (i // tiles_per_batch, 0, 0)),
                  rows(ATT_OUT), rows(GATE_WIDTH),
                  whole((CONV_WIDTH, D_MODEL)), whole((ATT_OUT, D_MODEL)),
                  whole((MEM_WIDTH, D_MODEL)), whole((D_MODEL, D_MODEL)), whole((1, D_MODEL))],
        out_specs=rows(D_MODEL),
        out_shape=jax.ShapeDtypeStruct((T, D_MODEL), F32),
        compiler_params=pltpu.CompilerParams(
            dimension_semantics=("parallel",), vmem_limit_bytes=VMEM_LIMIT),
        name="mix_out",
    )(x2, c2, qm, kv, o_att, gates, wc, wa, wm, wo, gpost)


FFN_TM = 512
FFN_TH = 256
FFN_CHUNKS = FFN_HIDDEN // FFN_TH


def _ffn_kernel(x_ref, gpre_ref, win_ref, wout_ref, gpost_ref, out_ref):
    x = x_ref[...]
    h = _rms(x, gpre_ref[...]).astype(BF16)
    acc = None
    for c in range(FFN_CHUNKS):
        lo = c * FFN_TH
        gv = jnp.dot(h, win_ref[:, lo:lo + FFN_TH], preferred_element_type=F32)
        uv = jnp.dot(h, win_ref[:, FFN_HIDDEN + lo:FFN_HIDDEN + lo + FFN_TH], preferred_element_type=F32)
        a = (gv * _sigmoid(gv) * uv).astype(BF16)
        part = jnp.dot(a, wout_ref[lo:lo + FFN_TH, :], preferred_element_type=F32)
        acc = part if acc is None else acc + part
    out_ref[...] = x + _rms(acc, gpost_ref[...])


def _ffn(x2, gpre, w_in, w_out, gpost):
    T = x2.shape[0]

    def resident(shape):
        return pl.BlockSpec(shape, lambda i: (0, 0), pipeline_mode=pl.Buffered(1))

    return pl.pallas_call(
        _ffn_kernel,
        grid=(T // FFN_TM,),
        in_specs=[pl.BlockSpec((FFN_TM, D_MODEL), lambda i: (i, 0)),
                  resident((1, D_MODEL)),
                  resident((D_MODEL, 2 * FFN_HIDDEN)),
                  resident((FFN_HIDDEN, D_MODEL)),
                  resident((1, D_MODEL))],
        out_specs=pl.BlockSpec((FFN_TM, D_MODEL), lambda i: (i, 0)),
        out_shape=jax.ShapeDtypeStruct((T, D_MODEL), F32),
        compiler_params=pltpu.CompilerParams(
            dimension_semantics=("parallel",), vmem_limit_bytes=VMEM_LIMIT),
        name="ffn",
    )(x2, gpre, w_in, w_out, gpost)


def _permute_in_proj(w):
    c1 = 2 * CONV_WIDTH
    c2 = c1 + N_GROUPS * QKV_WIDTH
    c3 = c2 + MEM_WIDTH
    group_cols = [w[:, c1 + which * N_GROUPS * ATT_OUT + g * ATT_OUT:
                    c1 + which * N_GROUPS * ATT_OUT + (g + 1) * ATT_OUT]
                  for g in range(N_GROUPS) for which in range(3)]
    return jnp.concatenate([w[:, :c1], w[:, c2:c3]] + group_cols + [w[:, c3:]], axis=1)


def kernel(x, mem, rel_bias, norm_mix_pre, w_in, b_gate, conv_dw, conv_dw_bias, conv_ln_g, conv_ln_b,
           w_conv_out, w_att_out, norm_mem, w_mem_kv, w_mem_out, w_out, norm_mix_post, norm_ffn_pre,
           w_ffn_in, w_ffn_out, norm_ffn_post):
    B, S, D = x.shape
    depth = w_in.shape[0]
    T = B * S
    biases = [_band_bias(rel_bias, g, d) for g, d in enumerate(DILATIONS)]

    def row(v):
        return v.reshape(1, -1)

    x2 = x.reshape(T, D)
    for l in range(depth):
        c, qm, qkv0, qkv1, qkv2, gates = _in_proj(
            x2, row(norm_mix_pre[l]), _permute_in_proj(w_in[l]).astype(BF16), row(b_gate[l]),
            conv_dw[l], row(conv_dw_bias[l]), row(conv_ln_g[l]), row(conv_ln_b[l]), B, S)
        o_att = _attention((qkv0, qkv1, qkv2), biases)
        kv = _mem_kv(mem, row(norm_mem[l]), w_mem_kv[l].astype(BF16))
        x2 = _mix(x2, c, qm, kv, o_att.reshape(T, ATT_OUT), gates,
                  w_conv_out[l].astype(BF16), w_att_out[l].astype(BF16), w_mem_out[l].astype(BF16),
                  w_out[l].astype(BF16), row(norm_mix_post[l]), S // MIX_TM)
        x2 = _ffn(x2, row(norm_ffn_pre[l]), w_ffn_in[l].astype(BF16), w_ffn_out[l].astype(BF16),
                  row(norm_ffn_post[l]))
    return x2.reshape(B, S, D)
```

```python
import functools
import math

import numpy as np
import jax
import jax.numpy as jnp
from jax import lax
from jax.experimental import pallas as pl
from jax.experimental.pallas import tpu as pltpu

F32 = jnp.float32
BF16 = jnp.bfloat16

D_MODEL = 1024
CONV_WIDTH = 512
CONV_KSIZE = 31
CONV_PAD = CONV_KSIZE // 2
DILATIONS = (1, 4, 16)
RADIUS = 64
N_GROUPS = 3
HEADS = 4
HEAD_DIM = 64
ATT_OUT = HEADS * HEAD_DIM
QKV_WIDTH = 3 * ATT_OUT
MEM_HEADS = 4
MEM_HEAD_DIM = 128
MEM_WIDTH = 512
FFN_HIDDEN = 2816
NUM_BUCKETS = 32
MAX_DISTANCE = 1024
RMS_EPS = 1e-6
LN_EPS = 1e-5
NEG_INF = -1e30

GATE_WIDTH = 3 * D_MODEL
SUBQ = 128
SUBK = SUBQ + 2 * RADIUS
LANES = 128
LSE_LANES = 128
LSE_PER_HEAD = LSE_LANES // HEADS
STAT_LANES = LSE_PER_HEAD // 2
LOG2E = math.log2(math.e)

VMEM_LIMIT = 56 * 1024 * 1024


def _sigmoid(v):
    return 1.0 / (1.0 + jnp.exp(-v))


def _rms(v, g):
    return v * lax.rsqrt(jnp.mean(v * v, axis=-1, keepdims=True) + RMS_EPS) * g


IN_TM = 512
IN_TN = 768
AG_WIDTH = 2 * CONV_WIDTH
QKV_COL = AG_WIDTH
QM_COL = QKV_COL + N_GROUPS * QKV_WIDTH
GATE_COL = QM_COL + MEM_WIDTH
N_GATE_TILES = GATE_WIDTH // IN_TN
assert IN_TN == QKV_WIDTH
SUBLANES = 8
CONV_HALO = 16
CONV_HALF = 256
CONV_RC = 64
CONV_FIRST = CONV_HALO - CONV_PAD
CONV_SHIFT_ROWS = CONV_HALF + (CONV_FIRST + CONV_KSIZE - 1) // SUBLANES * SUBLANES
IN_EXT = IN_TM + 2 * CONV_HALO


def _in_proj_kernel(x_ref, xp_ref, xn_ref, g_ref, w_ref, b_ref, cw_ref, cb_ref, lg_ref, lb_ref,
                    c_ref, qm_ref, q0_ref, q1_ref, q2_ref, gate_ref, acc_ref, u_ref, us_ref,
                    *, tiles_per_batch):
    ib = pl.program_id(0) % tiles_per_batch
    x_ext = jnp.concatenate([xp_ref[...], x_ref[...], xn_ref[...]], axis=0)
    h_ext = _rms(x_ext, g_ref[...]).astype(BF16)
    h = h_ext[CONV_HALO:CONV_HALO + IN_TM]

    ag = jnp.dot(h_ext, w_ref[:, 0:AG_WIDTH], preferred_element_type=F32)
    u = ag[:, 0:CONV_WIDTH] * _sigmoid(ag[:, CONV_WIDTH:])
    u_ref[0:CONV_HALO, :] = jnp.where(ib > 0, u[0:CONV_HALO], 0.0)
    u_ref[CONV_HALO:CONV_HALO + IN_TM, :] = u[CONV_HALO:CONV_HALO + IN_TM]
    u_ref[CONV_HALO + IN_TM:, :] = jnp.where(ib < tiles_per_batch - 1, u[CONV_HALO + IN_TM:], 0.0)
    for half in range(IN_TM // CONV_HALF):
        for s in range(SUBLANES):
            us_ref[s] = u_ref[half * CONV_HALF + s:half * CONV_HALF + s + CONV_SHIFT_ROWS, :]
        for c in range(CONV_HALF // CONV_RC):
            acc = jnp.zeros((CONV_RC, CONV_WIDTH), F32)
            for k in range(CONV_KSIZE):
                off = CONV_FIRST + k
                r0 = c * CONV_RC + off // SUBLANES * SUBLANES
                acc = acc + us_ref[off % SUBLANES, r0:r0 + CONV_RC, :] * cw_ref[k:k + 1, :]
            y = acc + cb_ref[...]
            mu = jnp.mean(y, axis=-1, keepdims=True)
            yc = y - mu
            yn = yc * lax.rsqrt(jnp.mean(yc * yc, axis=-1, keepdims=True) + LN_EPS)
            yn = yn * lg_ref[...] + lb_ref[...]
            rows = slice(half * CONV_HALF + c * CONV_RC, half * CONV_HALF + (c + 1) * CONV_RC)
            c_ref[rows, :] = (yn * _sigmoid(yn)).astype(BF16)

    def project(col, width):
        return jnp.dot(h, w_ref[:, col:col + width], preferred_element_type=F32)

    qm_ref[...] = project(QM_COL, MEM_WIDTH).astype(BF16)

    q_scale = HEAD_DIM ** -0.5 * LOG2E
    for g, (d, out_ref) in enumerate(zip(DILATIONS, (q0_ref, q1_ref, q2_ref))):
        for which in range(3):
            acc = project(QKV_COL + (which * N_GROUPS + g) * ATT_OUT, ATT_OUT)
            if which == 0:
                acc = acc * q_scale
            cols = slice(which * ATT_OUT, (which + 1) * ATT_OUT)
            if d == 1:
                out_ref[0, 0, :, cols] = acc.astype(BF16)
                continue
            for half in range(ATT_OUT // LANES):
                acc_ref[g - 1, 2 * which + half] = acc[:, half * LANES:(half + 1) * LANES]
        if d > 1:
            for r in range(d):
                for cb in range(QKV_WIDTH // LANES):
                    out_ref[0, r, :, cb * LANES:(cb + 1) * LANES] = (
                        acc_ref[g - 1, cb, pl.ds(r, IN_TM // d, stride=d), :].astype(BF16))

    for t in range(N_GATE_TILES):
        cols = slice(t * IN_TN, (t + 1) * IN_TN)
        gate_ref[:, cols] = _sigmoid(project(GATE_COL + t * IN_TN, IN_TN) + b_ref[:, cols]).astype(BF16)


def _in_proj(x2, g, w, b, conv_w, conv_b, ln_g, ln_b, batch, seq):
    T = x2.shape[0]
    tiles_per_batch = seq // IN_TM
    n_strided = sum(d > 1 for d in DILATIONS)
    halo_per_tile = IN_TM // CONV_HALO
    last_halo = T // CONV_HALO - 1

    def qkv_spec(d):
        return pl.BlockSpec((1, d, IN_TM // d, QKV_WIDTH),
                            lambda i: (i // tiles_per_batch, 0, i % tiles_per_batch, 0))

    def resident(shape):
        return pl.BlockSpec(shape, lambda i: (0, 0), pipeline_mode=pl.Buffered(1))

    def rows(width):
        return pl.BlockSpec((IN_TM, width), lambda i: (i, 0))

    return pl.pallas_call(
        functools.partial(_in_proj_kernel, tiles_per_batch=tiles_per_batch),
        grid=(T // IN_TM,),
        in_specs=[
            rows(D_MODEL),
            pl.BlockSpec((CONV_HALO, D_MODEL), lambda i: (jnp.maximum(i * halo_per_tile - 1, 0), 0)),
            pl.BlockSpec((CONV_HALO, D_MODEL), lambda i: (jnp.minimum((i + 1) * halo_per_tile, last_halo), 0)),
            resident((1, D_MODEL)),
            resident((D_MODEL, w.shape[1])),
            resident((1, GATE_WIDTH)),
            resident((CONV_KSIZE, CONV_WIDTH)), resident((1, CONV_WIDTH)),
            resident((1, CONV_WIDTH)), resident((1, CONV_WIDTH)),
        ],
        out_specs=[rows(CONV_WIDTH), rows(MEM_WIDTH),
                   qkv_spec(DILATIONS[0]), qkv_spec(DILATIONS[1]), qkv_spec(DILATIONS[2]),
                   rows(GATE_WIDTH)],
        out_shape=[jax.ShapeDtypeStruct((T, CONV_WIDTH), BF16), jax.ShapeDtypeStruct((T, MEM_WIDTH), BF16)]
        + [jax.ShapeDtypeStruct((batch, d, seq // d, QKV_WIDTH), BF16) for d in DILATIONS]
        + [jax.ShapeDtypeStruct((T, GATE_WIDTH), BF16)],
        scratch_shapes=[pltpu.VMEM((n_strided, QKV_WIDTH // LANES, IN_TM, LANES), F32),
                        pltpu.VMEM((IN_EXT, CONV_WIDTH), F32),
                        pltpu.VMEM((SUBLANES, CONV_SHIFT_ROWS, CONV_WIDTH), F32)],
        compiler_params=pltpu.CompilerParams(
            dimension_semantics=("parallel",), vmem_limit_bytes=VMEM_LIMIT),
        name="in_proj",
    )(x2, x2, x2, g, w, b, conv_w, conv_b, ln_g, ln_b)


ATT_TILE = 2048
ATT_ILP = 4
ATT_COMBINE_ROWS = 256


def _att_kernel(c0_ref, p0_ref, n0_ref, c1_ref, p1_ref, n1_ref, c2_ref, p2_ref, n2_ref,
                b0_ref, b1_ref, b2_ref, o_ref, onat_ref, lnat_ref):
    i = pl.program_id(1)
    n = pl.num_programs(1)
    head_of_lane = lax.broadcasted_iota(jnp.int32, (1, ATT_OUT), 1) // HEAD_DIM
    stat_of_lane = lax.broadcasted_iota(jnp.int32, (1, LSE_LANES), 1) // STAT_LANES
    key_col = lax.broadcasted_iota(jnp.int32, (1, SUBK), 1)
    head_masks = [head_of_lane == h for h in range(HEADS)]
    K0, V0 = ATT_OUT, 2 * ATT_OUT

    def subtile(g, d, q, k, v, bias_ref, first, last, row0):
        qs = jnp.concatenate([jnp.where(hm, q, jnp.zeros_like(q)) for hm in head_masks], axis=0)
        s = lax.dot_general(qs, k, (((1,), (1,)), ((), ())), preferred_element_type=F32)
        s = s + bias_ref[...]
        if first:
            s = jnp.where(key_col >= jnp.where(i == 0, RADIUS, 0), s, NEG_INF)
        if last:
            s = jnp.where(key_col < jnp.where(i == n - 1, SUBK - RADIUS, SUBK), s, NEG_INF)
        m = jnp.max(s, axis=-1, keepdims=True)
        e = jnp.exp2(s - m)
        l = jnp.sum(e, axis=-1, keepdims=True)
        o_all = jnp.dot(e.astype(BF16), v, preferred_element_type=F32)
        o = o_all[(HEADS - 1) * SUBQ:]
        stats = jnp.broadcast_to(l[(HEADS - 1) * SUBQ:], (SUBQ, LSE_LANES))
        stats = jnp.where(stat_of_lane == 2 * (HEADS - 1), m[(HEADS - 1) * SUBQ:], stats)
        for h in range(HEADS - 1):
            rows = slice(h * SUBQ, (h + 1) * SUBQ)
            o = jnp.where(head_masks[h], o_all[rows], o)
            stats = jnp.where(stat_of_lane == 2 * h, m[rows], stats)
            stats = jnp.where(stat_of_lane == 2 * h + 1, l[rows], stats)
        if d == 1:
            if not isinstance(row0, int):
                row0 = pl.multiple_of(row0, SUBQ)
            rows = pl.ds(row0, SUBQ)
        else:
            rows = pl.ds(row0, SUBQ, stride=d)
        for half in range(ATT_OUT // LANES):
            onat_ref[2 * g + half, rows, :] = o[:, half * LANES:(half + 1) * LANES]
        lnat_ref[g, rows, :] = stats

    groups = ((c0_ref, p0_ref, n0_ref, b0_ref), (c1_ref, p1_ref, n1_ref, b1_ref),
              (c2_ref, p2_ref, n2_ref, b2_ref))
    for g, (d, (c_ref, p_ref, n_ref, bias_ref)) in enumerate(zip(DILATIONS, groups)):
        tq = ATT_TILE // d
        nsub = tq // SUBQ

        def residue(r, carry, g=g, d=d, c_ref=c_ref, p_ref=p_ref, n_ref=n_ref, bias_ref=bias_ref,
                    tq=tq, nsub=nsub):
            def kv_cat(parts, col):
                return jnp.concatenate([ref[0, r, rows, col:col + ATT_OUT] for ref, rows in parts], axis=0)

            def sub(j):
                if isinstance(j, int) and (j == 0 or j == nsub - 1):
                    parts = [(c_ref, slice(max(j * SUBQ - RADIUS, 0), min(j * SUBQ + SUBQ + RADIUS, tq)))]
                    if j == 0:
                        parts = [(p_ref, slice(None))] + parts
                    if j == nsub - 1:
                        parts = parts + [(n_ref, slice(None))]
                    k, v = kv_cat(parts, K0), kv_cat(parts, V0)
                    q = c_ref[0, r, j * SUBQ:(j + 1) * SUBQ, 0:ATT_OUT]
                else:
                    q0 = j * SUBQ if isinstance(j, int) else pl.multiple_of(j * SUBQ, SUBQ)
                    k0 = j * SUBQ - RADIUS if isinstance(j, int) else pl.multiple_of(j * SUBQ - RADIUS, RADIUS)
                    q = c_ref[0, r, pl.ds(q0, SUBQ), 0:ATT_OUT]
                    k = c_ref[0, r, pl.ds(k0, SUBK), K0:K0 + ATT_OUT]
                    v = c_ref[0, r, pl.ds(k0, SUBK), V0:V0 + ATT_OUT]
                static = isinstance(j, int)
                subtile(g, d, q, k, v, bias_ref, static and j == 0, static and j == nsub - 1,
                        j * SUBQ * d + r)

            if nsub <= 2 * ATT_ILP:
                for j in range(nsub):
                    sub(j)
            else:
                for j in range(ATT_ILP):
                    sub(j)

                def interior(t, carry2):
                    for u in range(ATT_ILP):
                        sub(ATT_ILP + t * ATT_ILP + u)
                    return carry2

                lax.fori_loop(0, (nsub - 2 * ATT_ILP) // ATT_ILP, interior, 0)
                for j in range(nsub - ATT_ILP, nsub):
                    sub(j)
            return carry

        if d == 1:
            residue(0, 0)
        else:
            lax.fori_loop(0, d, residue, 0, unroll=max(1, ATT_ILP // nsub))

    w_lane = lax.broadcasted_iota(jnp.int32, (2 * LSE_LANES, ATT_OUT), 0) % LSE_LANES
    o_lane = lax.broadcasted_iota(jnp.int32, (2 * LSE_LANES, ATT_OUT), 1)
    spread = (w_lane == o_lane // HEAD_DIM * LSE_PER_HEAD).astype(BF16)
    weight_lane = lax.broadcasted_iota(jnp.int32, (1, LSE_LANES), 1) % LSE_PER_HEAD == 0

    def expand(w):
        w = jnp.where(weight_lane, w, 0.0)
        hi = w.astype(BF16)
        lo = (w - hi.astype(F32)).astype(BF16)
        return jnp.dot(jnp.concatenate([hi, lo], axis=1), spread, preferred_element_type=F32)

    def combine(t, carry):
        rows = pl.ds(pl.multiple_of(t * ATT_COMBINE_ROWS, ATT_COMBINE_ROWS), ATT_COMBINE_ROWS)
        stats = [lnat_ref[g, rows, :] for g in range(N_GROUPS)]
        mx = jnp.maximum(jnp.maximum(stats[0], stats[1]), stats[2])
        a = [jnp.exp2(st - mx) for st in stats]
        sums = [pltpu.roll(st, LSE_LANES - STAT_LANES, 1) for st in stats]
        inv = 1.0 / (a[0] * sums[0] + a[1] * sums[1] + a[2] * sums[2])
        o = None
        for g in range(N_GROUPS):
            acc = jnp.concatenate([onat_ref[2 * g, rows, :], onat_ref[2 * g + 1, rows, :]], axis=1)
            term = expand(a[g] * inv) * acc
            o = term if o is None else o + term
        o_ref[0, rows, :] = o.astype(BF16)
        return carry

    lax.fori_loop(0, ATT_TILE // ATT_COMBINE_ROWS, combine, 0)


def _attention(qkv, biases):
    B = qkv[0].shape[0]
    S = qkv[0].shape[2]
    in_specs = []
    for d in DILATIONS:
        tq = ATT_TILE // d
        nb = tq // RADIUS
        last_b = S // d // RADIUS - 1
        in_specs += [
            pl.BlockSpec((1, d, tq, QKV_WIDTH), lambda b, i: (b, 0, i, 0)),
            pl.BlockSpec((1, d, RADIUS, QKV_WIDTH),
                         lambda b, i, nb=nb: (b, 0, jnp.maximum(i * nb - 1, 0), 0)),
            pl.BlockSpec((1, d, RADIUS, QKV_WIDTH),
                         lambda b, i, nb=nb, last_b=last_b: (b, 0, jnp.minimum((i + 1) * nb, last_b), 0)),
        ]
    in_specs += [pl.BlockSpec((HEADS * SUBQ, SUBK), lambda b, i: (0, 0))] * N_GROUPS
    args = []
    for a in qkv:
        args += [a, a, a]
    return pl.pallas_call(
        _att_kernel,
        grid=(B, S // ATT_TILE),
        in_specs=in_specs,
        out_specs=pl.BlockSpec((1, ATT_TILE, ATT_OUT), lambda b, i: (b, i, 0)),
        out_shape=jax.ShapeDtypeStruct((B, S, ATT_OUT), BF16),
        scratch_shapes=[pltpu.VMEM((N_GROUPS * (ATT_OUT // LANES), ATT_TILE, LANES), F32),
                        pltpu.VMEM((N_GROUPS, ATT_TILE, LSE_LANES), F32)],
        compiler_params=pltpu.CompilerParams(
            dimension_semantics=("parallel", "parallel"), vmem_limit_bytes=VMEM_LIMIT),
        name="dilated_att",
    )(*args, *biases)


def _t5_bucket_np(rel):
    nb = NUM_BUCKETS // 2
    max_exact = nb // 2
    ret = np.where(rel > 0, nb, 0)
    n = np.abs(rel)
    nf = np.maximum(n, 1).astype(np.float32)
    ratio = np.log(nf / np.float32(max_exact)) / np.float32(math.log(MAX_DISTANCE / max_exact))
    large = max_exact + (ratio * np.float32(nb - max_exact)).astype(np.int32)
    large = np.minimum(large, nb - 1)
    return ret + np.where(n < max_exact, n, large)


def _band_bias(rel_bias, g, dilation):
    period = SUBQ + SUBK
    nband = 2 * RADIUS + 1
    bucket = _t5_bucket_np((np.arange(nband) - RADIUS) * dilation)
    onehot = np.zeros((period, NUM_BUCKETS), np.float32)
    onehot[np.arange(nband), bucket] = 1.0
    tab = rel_bias[:, g * HEADS:(g + 1) * HEADS].astype(F32)
    t = jnp.dot(jnp.asarray(onehot), tab, precision=lax.Precision.HIGHEST)
    t = jnp.where((np.arange(period) < nband)[:, None], t, NEG_INF).T
    skew = jnp.tile(t, (1, SUBQ))[:, :SUBQ * (period - 1)].reshape(HEADS, SUBQ, period - 1)
    return skew[:, :, :SUBK].reshape(HEADS * SUBQ, SUBK) * LOG2E


def _mem_kv_kernel(mem_ref, g_ref, w_ref, kv_ref):
    h = _rms(mem_ref[0], g_ref[...]).astype(BF16)
    kv_ref[0] = jnp.dot(h, w_ref[...], preferred_element_type=F32).astype(BF16)


def _mem_kv(mem, g, w):
    B, M, _ = mem.shape
    return pl.pallas_call(
        _mem_kv_kernel,
        grid=(B,),
        in_specs=[pl.BlockSpec((1, M, D_MODEL), lambda b: (b, 0, 0)),
                  pl.BlockSpec((1, D_MODEL), lambda b: (0, 0)),
                  pl.BlockSpec((D_MODEL, 2 * MEM_WIDTH), lambda b: (0, 0))],
        out_specs=pl.BlockSpec((1, M, 2 * MEM_WIDTH), lambda b: (b, 0, 0)),
        out_shape=jax.ShapeDtypeStruct((B, M, 2 * MEM_WIDTH), BF16),
        compiler_params=pltpu.CompilerParams(
            dimension_semantics=("parallel",), vmem_limit_bytes=VMEM_LIMIT),
        name="mem_kv",
    )(mem, g, w)


MIX_TM = 512


def _mix_kernel(x_ref, c_ref, qm_ref, kv_ref, oatt_ref, gate_ref, wc_ref, wa_ref, wm_ref, wo_ref,
                gpost_ref, out_ref):
    scale = MEM_HEAD_DIM ** -0.5
    heads = []
    for h in range(MEM_HEADS):
        lo = h * MEM_HEAD_DIM
        qh = qm_ref[:, lo:lo + MEM_HEAD_DIM]
        kh = kv_ref[0, :, lo:lo + MEM_HEAD_DIM]
        vh = kv_ref[0, :, MEM_WIDTH + lo:MEM_WIDTH + lo + MEM_HEAD_DIM]
        s = lax.dot_general(qh, kh, (((1,), (1,)), ((), ())), preferred_element_type=F32) * scale
        m = jnp.max(s, axis=-1, keepdims=True)
        e = jnp.exp(s - m)
        p = (e * (1.0 / jnp.sum(e, axis=-1, keepdims=True))).astype(BF16)
        heads.append(jnp.dot(p, vh, preferred_element_type=F32).astype(BF16))
    o_mem = jnp.concatenate(heads, axis=-1)
    y_mem = jnp.dot(o_mem, wm_ref[...], preferred_element_type=F32)
    y_conv = jnp.dot(c_ref[...], wc_ref[...], preferred_element_type=F32)
    y_att = jnp.dot(oatt_ref[...], wa_ref[...], preferred_element_type=F32)
    merged = (gate_ref[:, 0:D_MODEL].astype(F32) * y_conv
              + gate_ref[:, D_MODEL:2 * D_MODEL].astype(F32) * y_att
              + gate_ref[:, 2 * D_MODEL:].astype(F32) * y_mem)
    y = jnp.dot(merged.astype(BF16), wo_ref[...], preferred_element_type=F32)
    out_ref[...] = x_ref[...] + _rms(y, gpost_ref[...])


def _mix(x2, c2, qm, kv, o_att, gates, wc, wa, wm, wo, gpost, tiles_per_batch):
    T = x2.shape[0]
    M = kv.shape[1]

    def rows(width):
        return pl.BlockSpec((MIX_TM, width), lambda i: (i, 0))

    def whole(shape):
        return pl.BlockSpec(shape, lambda i: (0,) * len(shape))

    return pl.pallas_call(
        _mix_kernel,
        grid=(T // MIX_TM,),
        in_specs=[rows(D_MODEL), rows(CONV_WIDTH), rows(MEM_WIDTH),
                  pl.BlockSpec((1, M, 2 * MEM_WIDTH), lambda i: (i // tiles_per_batch, 0, 0)),
                  rows(ATT_OUT), rows(GATE_WIDTH),
                  whole((CONV_WIDTH, D_MODEL)), whole((ATT_OUT, D_MODEL)),
                  whole((MEM_WIDTH, D_MODEL)), whole((D_MODEL, D_MODEL)), whole((1, D_MODEL))],
        out_specs=rows(D_MODEL),
        out_shape=jax.ShapeDtypeStruct((T, D_MODEL), F32),
        compiler_params=pltpu.CompilerParams(
            dimension_semantics=("parallel",), vmem_limit_bytes=VMEM_LIMIT),
        name="mix_out",
    )(x2, c2, qm, kv, o_att, gates, wc, wa, wm, wo, gpost)


FFN_TM = 512
FFN_TH = 256
FFN_CHUNKS = FFN_HIDDEN // FFN_TH


def _ffn_kernel(x_ref, gpre_ref, win_ref, wout_ref, gpost_ref, out_ref):
    x = x_ref[...]
    h = _rms(x, gpre_ref[...]).astype(BF16)
    acc = None
    for c in range(FFN_CHUNKS):
        lo = c * FFN_TH
        gv = jnp.dot(h, win_ref[:, lo:lo + FFN_TH], preferred_element_type=F32)
        uv = jnp.dot(h, win_ref[:, FFN_HIDDEN + lo:FFN_HIDDEN + lo + FFN_TH], preferred_element_type=F32)
        a = (gv * _sigmoid(gv) * uv).astype(BF16)
        part = jnp.dot(a, wout_ref[lo:lo + FFN_TH, :], preferred_element_type=F32)
        acc = part if acc is None else acc + part
    out_ref[...] = x + _rms(acc, gpost_ref[...])


def _ffn(x2, gpre, w_in, w_out, gpost):
    T = x2.shape[0]

    def resident(shape):
        return pl.BlockSpec(shape, lambda i: (0, 0), pipeline_mode=pl.Buffered(1))

    return pl.pallas_call(
        _ffn_kernel,
        grid=(T // FFN_TM,),
        in_specs=[pl.BlockSpec((FFN_TM, D_MODEL), lambda i: (i, 0)),
                  resident((1, D_MODEL)),
                  resident((D_MODEL, 2 * FFN_HIDDEN)),
                  resident((FFN_HIDDEN, D_MODEL)),
                  resident((1, D_MODEL))],
        out_specs=pl.BlockSpec((FFN_TM, D_MODEL), lambda i: (i, 0)),
        out_shape=jax.ShapeDtypeStruct((T, D_MODEL), F32),
        compiler_params=pltpu.CompilerParams(
            dimension_semantics=("parallel",), vmem_limit_bytes=VMEM_LIMIT),
        name="ffn",
    )(x2, gpre, w_in, w_out, gpost)


def kernel(x, mem, rel_bias, norm_mix_pre, w_in, b_gate, conv_dw, conv_dw_bias, conv_ln_g, conv_ln_b,
           w_conv_out, w_att_out, norm_mem, w_mem_kv, w_mem_out, w_out, norm_mix_post, norm_ffn_pre,
           w_ffn_in, w_ffn_out, norm_ffn_post):
    B, S, D = x.shape
    depth = w_in.shape[0]
    T = B * S
    biases = [_band_bias(rel_bias, g, d) for g, d in enumerate(DILATIONS)]

    def row(v):
        return v.reshape(1, -1)

    x2 = x.reshape(T, D)
    for l in range(depth):
        c, qm, qkv0, qkv1, qkv2, gates = _in_proj(
            x2, row(norm_mix_pre[l]), w_in[l].astype(BF16), row(b_gate[l]),
            conv_dw[l], row(conv_dw_bias[l]), row(conv_ln_g[l]), row(conv_ln_b[l]), B, S)
        o_att = _attention((qkv0, qkv1, qkv2), biases)
        kv = _mem_kv(mem, row(norm_mem[l]), w_mem_kv[l].astype(BF16))
        x2 = _mix(x2, c, qm, kv, o_att.reshape(T, ATT_OUT), gates,
                  w_conv_out[l].astype(BF16), w_att_out[l].astype(BF16), w_mem_out[l].astype(BF16),
                  w_out[l].astype(BF16), row(norm_mix_post[l]), S // MIX_TM)
        x2 = _ffn(x2, row(norm_ffn_pre[l]), w_ffn_in[l].astype(BF16), w_ffn_out[l].astype(BF16),
                  row(norm_ffn_post[l]))
    return x2.reshape(B, S, D)
```

```python
import functools
import math

import numpy as np
import jax
import jax.numpy as jnp
from jax import lax
from jax.experimental import pallas as pl
from jax.experimental.pallas import tpu as pltpu

F32 = jnp.float32
BF16 = jnp.bfloat16

D_MODEL = 1024
CONV_WIDTH = 512
CONV_KSIZE = 31
CONV_PAD = CONV_KSIZE // 2
DILATIONS = (1, 4, 16)
RADIUS = 64
N_GROUPS = 3
HEADS = 4
HEAD_DIM = 64
ATT_OUT = HEADS * HEAD_DIM
QKV_WIDTH = 3 * ATT_OUT
MEM_HEADS = 4
MEM_HEAD_DIM = 128
MEM_WIDTH = 512
FFN_HIDDEN = 2816
NUM_BUCKETS = 32
MAX_DISTANCE = 1024
RMS_EPS = 1e-6
LN_EPS = 1e-5
NEG_INF = -1e30

GATE_WIDTH = 3 * D_MODEL
SUBQ = 128
SUBK = SUBQ + 2 * RADIUS
LANES = 128
LSE_LANES = 128
LSE_PER_HEAD = LSE_LANES // HEADS
STAT_LANES = LSE_PER_HEAD // 2
LOG2E = math.log2(math.e)

VMEM_LIMIT = 56 * 1024 * 1024


def _sigmoid(v):
    return 1.0 / (1.0 + jnp.exp(-v))


def _rms(v, g):
    return v * lax.rsqrt(jnp.mean(v * v, axis=-1, keepdims=True) + RMS_EPS) * g


def _layer_weight(w, layer):
    return pl.BlockSpec((1,) + w.shape[1:], lambda *_: (layer, 0, 0), pipeline_mode=pl.Buffered(1))


IN_TM = 512
IN_TN = 768
AG_WIDTH = 2 * CONV_WIDTH
QKV_COL = AG_WIDTH
QM_COL = QKV_COL + N_GROUPS * QKV_WIDTH
GATE_COL = QM_COL + MEM_WIDTH
N_GATE_TILES = GATE_WIDTH // IN_TN
assert IN_TN == QKV_WIDTH
SUBLANES = 8
CONV_HALO = 16
CONV_HALF = 256
CONV_RC = 64
CONV_FIRST = CONV_HALO - CONV_PAD
CONV_SHIFT_ROWS = CONV_HALF + (CONV_FIRST + CONV_KSIZE - 1) // SUBLANES * SUBLANES
IN_EXT = IN_TM + 2 * CONV_HALO


def _in_proj_kernel(x_ref, xp_ref, xn_ref, g_ref, w_ref, b_ref, cw_ref, cb_ref, lg_ref, lb_ref,
                    c_ref, qm_ref, q0_ref, q1_ref, q2_ref, gate_ref, acc_ref, u_ref, us_ref,
                    *, tiles_per_batch):
    ib = pl.program_id(0) % tiles_per_batch
    x_ext = jnp.concatenate([xp_ref[...], x_ref[...], xn_ref[...]], axis=0)
    h_ext = _rms(x_ext, g_ref[...]).astype(BF16)
    h = h_ext[CONV_HALO:CONV_HALO + IN_TM]

    ag = jnp.dot(h_ext, w_ref[0, :, 0:AG_WIDTH], preferred_element_type=F32)
    u = ag[:, 0:CONV_WIDTH] * _sigmoid(ag[:, CONV_WIDTH:])
    u_ref[0:CONV_HALO, :] = jnp.where(ib > 0, u[0:CONV_HALO], 0.0)
    u_ref[CONV_HALO:CONV_HALO + IN_TM, :] = u[CONV_HALO:CONV_HALO + IN_TM]
    u_ref[CONV_HALO + IN_TM:, :] = jnp.where(ib < tiles_per_batch - 1, u[CONV_HALO + IN_TM:], 0.0)
    for half in range(IN_TM // CONV_HALF):
        for s in range(SUBLANES):
            us_ref[s] = u_ref[half * CONV_HALF + s:half * CONV_HALF + s + CONV_SHIFT_ROWS, :]
        for c in range(CONV_HALF // CONV_RC):
            acc = jnp.zeros((CONV_RC, CONV_WIDTH), F32)
            for k in range(CONV_KSIZE):
                off = CONV_FIRST + k
                r0 = c * CONV_RC + off // SUBLANES * SUBLANES
                acc = acc + us_ref[off % SUBLANES, r0:r0 + CONV_RC, :] * cw_ref[k:k + 1, :]
            y = acc + cb_ref[...]
            mu = jnp.mean(y, axis=-1, keepdims=True)
            yc = y - mu
            yn = yc * lax.rsqrt(jnp.mean(yc * yc, axis=-1, keepdims=True) + LN_EPS)
            yn = yn * lg_ref[...] + lb_ref[...]
            rows = slice(half * CONV_HALF + c * CONV_RC, half * CONV_HALF + (c + 1) * CONV_RC)
            c_ref[rows, :] = (yn * _sigmoid(yn)).astype(BF16)

    def project(col, width):
        return jnp.dot(h, w_ref[0, :, col:col + width], preferred_element_type=F32)

    qm_ref[...] = project(QM_COL, MEM_WIDTH).astype(BF16)

    q_scale = HEAD_DIM ** -0.5 * LOG2E
    for g, (d, out_ref) in enumerate(zip(DILATIONS, (q0_ref, q1_ref, q2_ref))):
        for which in range(3):
            acc = project(QKV_COL + (which * N_GROUPS + g) * ATT_OUT, ATT_OUT)
            if which == 0:
                acc = acc * q_scale
            cols = slice(which * ATT_OUT, (which + 1) * ATT_OUT)
            if d == 1:
                out_ref[0, 0, :, cols] = acc.astype(BF16)
                continue
            for half in range(ATT_OUT // LANES):
                acc_ref[g - 1, 2 * which + half] = acc[:, half * LANES:(half + 1) * LANES]
        if d > 1:
            for r in range(d):
                for cb in range(QKV_WIDTH // LANES):
                    out_ref[0, r, :, cb * LANES:(cb + 1) * LANES] = (
                        acc_ref[g - 1, cb, pl.ds(r, IN_TM // d, stride=d), :].astype(BF16))

    for t in range(N_GATE_TILES):
        cols = slice(t * IN_TN, (t + 1) * IN_TN)
        gate_ref[:, cols] = _sigmoid(project(GATE_COL + t * IN_TN, IN_TN) + b_ref[:, cols]).astype(BF16)


def _in_proj(x2, g, w, layer, b, conv_w, conv_b, ln_g, ln_b, batch, seq):
    T = x2.shape[0]
    tiles_per_batch = seq // IN_TM
    n_strided = sum(d > 1 for d in DILATIONS)
    halo_per_tile = IN_TM // CONV_HALO
    last_halo = T // CONV_HALO - 1

    def qkv_spec(d):
        return pl.BlockSpec((1, d, IN_TM // d, QKV_WIDTH),
                            lambda i: (i // tiles_per_batch, 0, i % tiles_per_batch, 0))

    def resident(shape):
        return pl.BlockSpec(shape, lambda i: (0, 0), pipeline_mode=pl.Buffered(1))

    def rows(width):
        return pl.BlockSpec((IN_TM, width), lambda i: (i, 0))

    return pl.pallas_call(
        functools.partial(_in_proj_kernel, tiles_per_batch=tiles_per_batch),
        grid=(T // IN_TM,),
        in_specs=[
            rows(D_MODEL),
            pl.BlockSpec((CONV_HALO, D_MODEL), lambda i: (jnp.maximum(i * halo_per_tile - 1, 0), 0)),
            pl.BlockSpec((CONV_HALO, D_MODEL), lambda i: (jnp.minimum((i + 1) * halo_per_tile, last_halo), 0)),
            resident((1, D_MODEL)),
            _layer_weight(w, layer),
            resident((1, GATE_WIDTH)),
            resident((CONV_KSIZE, CONV_WIDTH)), resident((1, CONV_WIDTH)),
            resident((1, CONV_WIDTH)), resident((1, CONV_WIDTH)),
        ],
        out_specs=[rows(CONV_WIDTH), rows(MEM_WIDTH),
                   qkv_spec(DILATIONS[0]), qkv_spec(DILATIONS[1]), qkv_spec(DILATIONS[2]),
                   rows(GATE_WIDTH)],
        out_shape=[jax.ShapeDtypeStruct((T, CONV_WIDTH), BF16), jax.ShapeDtypeStruct((T, MEM_WIDTH), BF16)]
        + [jax.ShapeDtypeStruct((batch, d, seq // d, QKV_WIDTH), BF16) for d in DILATIONS]
        + [jax.ShapeDtypeStruct((T, GATE_WIDTH), BF16)],
        scratch_shapes=[pltpu.VMEM((n_strided, QKV_WIDTH // LANES, IN_TM, LANES), F32),
                        pltpu.VMEM((IN_EXT, CONV_WIDTH), F32),
                        pltpu.VMEM((SUBLANES, CONV_SHIFT_ROWS, CONV_WIDTH), F32)],
        compiler_params=pltpu.CompilerParams(
            dimension_semantics=("parallel",), vmem_limit_bytes=VMEM_LIMIT),
        name="in_proj",
    )(x2, x2, x2, g, w, b, conv_w, conv_b, ln_g, ln_b)


ATT_TILE = 2048
ATT_ILP = 4
ATT_COMBINE_ROWS = 256


def _att_kernel(c0_ref, p0_ref, n0_ref, c1_ref, p1_ref, n1_ref, c2_ref, p2_ref, n2_ref,
                b0_ref, b1_ref, b2_ref, o_ref, onat_ref, lnat_ref):
    i = pl.program_id(1)
    n = pl.num_programs(1)
    head_of_lane = lax.broadcasted_iota(jnp.int32, (1, ATT_OUT), 1) // HEAD_DIM
    stat_of_lane = lax.broadcasted_iota(jnp.int32, (1, LSE_LANES), 1) // STAT_LANES
    key_col = lax.broadcasted_iota(jnp.int32, (1, SUBK), 1)
    head_masks = [head_of_lane == h for h in range(HEADS)]
    K0, V0 = ATT_OUT, 2 * ATT_OUT

    def subtile(g, d, q, k, v, bias_ref, first, last, row0):
        qs = jnp.concatenate([jnp.where(hm, q, jnp.zeros_like(q)) for hm in head_masks], axis=0)
        s = lax.dot_general(qs, k, (((1,), (1,)), ((), ())), preferred_element_type=F32)
        s = s + bias_ref[...]
        if first:
            s = jnp.where(key_col >= jnp.where(i == 0, RADIUS, 0), s, NEG_INF)
        if last:
            s = jnp.where(key_col < jnp.where(i == n - 1, SUBK - RADIUS, SUBK), s, NEG_INF)
        m = jnp.max(s, axis=-1, keepdims=True)
        e = jnp.exp2(s - m)
        l = jnp.sum(e, axis=-1, keepdims=True)
        o_all = jnp.dot(e.astype(BF16), v, preferred_element_type=F32)
        o = o_all[(HEADS - 1) * SUBQ:]
        stats = jnp.broadcast_to(l[(HEADS - 1) * SUBQ:], (SUBQ, LSE_LANES))
        stats = jnp.where(stat_of_lane == 2 * (HEADS - 1), m[(HEADS - 1) * SUBQ:], stats)
        for h in range(HEADS - 1):
            rows = slice(h * SUBQ, (h + 1) * SUBQ)
            o = jnp.where(head_masks[h], o_all[rows], o)
            stats = jnp.where(stat_of_lane == 2 * h, m[rows], stats)
            stats = jnp.where(stat_of_lane == 2 * h + 1, l[rows], stats)
        if d == 1:
            if not isinstance(row0, int):
                row0 = pl.multiple_of(row0, SUBQ)
            rows = pl.ds(row0, SUBQ)
        else:
            rows = pl.ds(row0, SUBQ, stride=d)
        for half in range(ATT_OUT // LANES):
            onat_ref[2 * g + half, rows, :] = o[:, half * LANES:(half + 1) * LANES]
        lnat_ref[g, rows, :] = stats

    groups = ((c0_ref, p0_ref, n0_ref, b0_ref), (c1_ref, p1_ref, n1_ref, b1_ref),
              (c2_ref, p2_ref, n2_ref, b2_ref))
    for g, (d, (c_ref, p_ref, n_ref, bias_ref)) in enumerate(zip(DILATIONS, groups)):
        tq = ATT_TILE // d
        nsub = tq // SUBQ

        def residue(r, carry, g=g, d=d, c_ref=c_ref, p_ref=p_ref, n_ref=n_ref, bias_ref=bias_ref,
                    tq=tq, nsub=nsub):
            def kv_cat(parts, col):
                return jnp.concatenate([ref[0, r, rows, col:col + ATT_OUT] for ref, rows in parts], axis=0)

            def sub(j):
                if isinstance(j, int) and (j == 0 or j == nsub - 1):
                    parts = [(c_ref, slice(max(j * SUBQ - RADIUS, 0), min(j * SUBQ + SUBQ + RADIUS, tq)))]
                    if j == 0:
                        parts = [(p_ref, slice(None))] + parts
                    if j == nsub - 1:
                        parts = parts + [(n_ref, slice(None))]
                    k, v = kv_cat(parts, K0), kv_cat(parts, V0)
                    q = c_ref[0, r, j * SUBQ:(j + 1) * SUBQ, 0:ATT_OUT]
                else:
                    q0 = j * SUBQ if isinstance(j, int) else pl.multiple_of(j * SUBQ, SUBQ)
                    k0 = j * SUBQ - RADIUS if isinstance(j, int) else pl.multiple_of(j * SUBQ - RADIUS, RADIUS)
                    q = c_ref[0, r, pl.ds(q0, SUBQ), 0:ATT_OUT]
                    k = c_ref[0, r, pl.ds(k0, SUBK), K0:K0 + ATT_OUT]
                    v = c_ref[0, r, pl.ds(k0, SUBK), V0:V0 + ATT_OUT]
                static = isinstance(j, int)
                subtile(g, d, q, k, v, bias_ref, static and j == 0, static and j == nsub - 1,
                        j * SUBQ * d + r)

            if nsub <= 2 * ATT_ILP:
                for j in range(nsub):
                    sub(j)
            else:
                for j in range(ATT_ILP):
                    sub(j)

                def interior(t, carry2):
                    for u in range(ATT_ILP):
                        sub(ATT_ILP + t * ATT_ILP + u)
                    return carry2

                lax.fori_loop(0, (nsub - 2 * ATT_ILP) // ATT_ILP, interior, 0)
                for j in range(nsub - ATT_ILP, nsub):
                    sub(j)
            return carry

        if d == 1:
            residue(0, 0)
        else:
            lax.fori_loop(0, d, residue, 0, unroll=max(1, ATT_ILP // nsub))

    w_lane = lax.broadcasted_iota(jnp.int32, (2 * LSE_LANES, ATT_OUT), 0) % LSE_LANES
    o_lane = lax.broadcasted_iota(jnp.int32, (2 * LSE_LANES, ATT_OUT), 1)
    spread = (w_lane == o_lane // HEAD_DIM * LSE_PER_HEAD).astype(BF16)
    weight_lane = lax.broadcasted_iota(jnp.int32, (1, LSE_LANES), 1) % LSE_PER_HEAD == 0

    def expand(w):
        w = jnp.where(weight_lane, w, 0.0)
        hi = w.astype(BF16)
        lo = (w - hi.astype(F32)).astype(BF16)
        return jnp.dot(jnp.concatenate([hi, lo], axis=1), spread, preferred_element_type=F32)

    def combine(t, carry):
        rows = pl.ds(pl.multiple_of(t * ATT_COMBINE_ROWS, ATT_COMBINE_ROWS), ATT_COMBINE_ROWS)
        stats = [lnat_ref[g, rows, :] for g in range(N_GROUPS)]
        mx = jnp.maximum(jnp.maximum(stats[0], stats[1]), stats[2])
        a = [jnp.exp2(st - mx) for st in stats]
        sums = [pltpu.roll(st, LSE_LANES - STAT_LANES, 1) for st in stats]
        inv = 1.0 / (a[0] * sums[0] + a[1] * sums[1] + a[2] * sums[2])
        o = None
        for g in range(N_GROUPS):
            acc = jnp.concatenate([onat_ref[2 * g, rows, :], onat_ref[2 * g + 1, rows, :]], axis=1)
            term = expand(a[g] * inv) * acc
            o = term if o is None else o + term
        o_ref[0, rows, :] = o.astype(BF16)
        return carry

    lax.fori_loop(0, ATT_TILE // ATT_COMBINE_ROWS, combine, 0)


def _attention(qkv, biases):
    B = qkv[0].shape[0]
    S = qkv[0].shape[2]
    in_specs = []
    for d in DILATIONS:
        tq = ATT_TILE // d
        nb = tq // RADIUS
        last_b = S // d // RADIUS - 1
        in_specs += [
            pl.BlockSpec((1, d, tq, QKV_WIDTH), lambda b, i: (b, 0, i, 0)),
            pl.BlockSpec((1, d, RADIUS, QKV_WIDTH),
                         lambda b, i, nb=nb: (b, 0, jnp.maximum(i * nb - 1, 0), 0)),
            pl.BlockSpec((1, d, RADIUS, QKV_WIDTH),
                         lambda b, i, nb=nb, last_b=last_b: (b, 0, jnp.minimum((i + 1) * nb, last_b), 0)),
        ]
    in_specs += [pl.BlockSpec((HEADS * SUBQ, SUBK), lambda b, i: (0, 0))] * N_GROUPS
    args = []
    for a in qkv:
        args += [a, a, a]
    return pl.pallas_call(
        _att_kernel,
        grid=(B, S // ATT_TILE),
        in_specs=in_specs,
        out_specs=pl.BlockSpec((1, ATT_TILE, ATT_OUT), lambda b, i: (b, i, 0)),
        out_shape=jax.ShapeDtypeStruct((B, S, ATT_OUT), BF16),
        scratch_shapes=[pltpu.VMEM((N_GROUPS * (ATT_OUT // LANES), ATT_TILE, LANES), F32),
                        pltpu.VMEM((N_GROUPS, ATT_TILE, LSE_LANES), F32)],
        compiler_params=pltpu.CompilerParams(
            dimension_semantics=("parallel", "parallel"), vmem_limit_bytes=VMEM_LIMIT),
        name="dilated_att",
    )(*args, *biases)


def _t5_bucket_np(rel):
    nb = NUM_BUCKETS // 2
    max_exact = nb // 2
    ret = np.where(rel > 0, nb, 0)
    n = np.abs(rel)
    nf = np.maximum(n, 1).astype(np.float32)
    ratio = np.log(nf / np.float32(max_exact)) / np.float32(math.log(MAX_DISTANCE / max_exact))
    large = max_exact + (ratio * np.float32(nb - max_exact)).astype(np.int32)
    large = np.minimum(large, nb - 1)
    return ret + np.where(n < max_exact, n, large)


def _band_bias(rel_bias, g, dilation):
    period = SUBQ + SUBK
    nband = 2 * RADIUS + 1
    bucket = _t5_bucket_np((np.arange(nband) - RADIUS) * dilation)
    onehot = np.zeros((period, NUM_BUCKETS), np.float32)
    onehot[np.arange(nband), bucket] = 1.0
    tab = rel_bias[:, g * HEADS:(g + 1) * HEADS].astype(F32)
    t = jnp.dot(jnp.asarray(onehot), tab, precision=lax.Precision.HIGHEST)
    t = jnp.where((np.arange(period) < nband)[:, None], t, NEG_INF).T
    skew = jnp.tile(t, (1, SUBQ))[:, :SUBQ * (period - 1)].reshape(HEADS, SUBQ, period - 1)
    return skew[:, :, :SUBK].reshape(HEADS * SUBQ, SUBK) * LOG2E


def _mem_kv_kernel(mem_ref, g_ref, w_ref, kv_ref):
    h = _rms(mem_ref[0], g_ref[...]).astype(BF16)
    kv_ref[0] = jnp.dot(h, w_ref[0], preferred_element_type=F32).astype(BF16)


def _mem_kv(mem, g, w, layer):
    B, M, _ = mem.shape
    return pl.pallas_call(
        _mem_kv_kernel,
        grid=(B,),
        in_specs=[pl.BlockSpec((1, M, D_MODEL), lambda b: (b, 0, 0)),
                  pl.BlockSpec((1, D_MODEL), lambda b: (0, 0)),
                  _layer_weight(w, layer)],
        out_specs=pl.BlockSpec((1, M, 2 * MEM_WIDTH), lambda b: (b, 0, 0)),
        out_shape=jax.ShapeDtypeStruct((B, M, 2 * MEM_WIDTH), BF16),
        compiler_params=pltpu.CompilerParams(
            dimension_semantics=("parallel",), vmem_limit_bytes=VMEM_LIMIT),
        name="mem_kv",
    )(mem, g, w)


MIX_TM = 512


def _mix_kernel(x_ref, c_ref, qm_ref, kv_ref, oatt_ref, gate_ref, wc_ref, wa_ref, wm_ref, wo_ref,
                gpost_ref, out_ref):
    scale = MEM_HEAD_DIM ** -0.5
    heads = []
    for h in range(MEM_HEADS):
        lo = h * MEM_HEAD_DIM
        qh = qm_ref[:, lo:lo + MEM_HEAD_DIM]
        kh = kv_ref[0, :, lo:lo + MEM_HEAD_DIM]
        vh = kv_ref[0, :, MEM_WIDTH + lo:MEM_WIDTH + lo + MEM_HEAD_DIM]
        s = lax.dot_general(qh, kh, (((1,), (1,)), ((), ())), preferred_element_type=F32) * scale
        m = jnp.max(s, axis=-1, keepdims=True)
        e = jnp.exp(s - m)
        p = (e * (1.0 / jnp.sum(e, axis=-1, keepdims=True))).astype(BF16)
        heads.append(jnp.dot(p, vh, preferred_element_type=F32).astype(BF16))
    o_mem = jnp.concatenate(heads, axis=-1)
    y_mem = jnp.dot(o_mem, wm_ref[0], preferred_element_type=F32)
    y_conv = jnp.dot(c_ref[...], wc_ref[0], preferred_element_type=F32)
    y_att = jnp.dot(oatt_ref[...], wa_ref[0], preferred_element_type=F32)
    merged = (gate_ref[:, 0:D_MODEL].astype(F32) * y_conv
              + gate_ref[:, D_MODEL:2 * D_MODEL].astype(F32) * y_att
              + gate_ref[:, 2 * D_MODEL:].astype(F32) * y_mem)
    y = jnp.dot(merged.astype(BF16), wo_ref[0], preferred_element_type=F32)
    out_ref[...] = x_ref[...] + _rms(y, gpost_ref[...])


def _mix(x2, c2, qm, kv, o_att, gates, wc, wa, wm, wo, layer, gpost, tiles_per_batch):
    T = x2.shape[0]
    M = kv.shape[1]

    def rows(width):
        return pl.BlockSpec((MIX_TM, width), lambda i: (i, 0))

    def whole(shape):
        return pl.BlockSpec(shape, lambda i: (0,) * len(shape))

    return pl.pallas_call(
        _mix_kernel,
        grid=(T // MIX_TM,),
        in_specs=[rows(D_MODEL), rows(CONV_WIDTH), rows(MEM_WIDTH),
                  pl.BlockSpec((1, M, 2 * MEM_WIDTH), lambda i: (i // tiles_per_batch, 0, 0)),
                  rows(ATT_OUT), rows(GATE_WIDTH),
                  _layer_weight(wc, layer), _layer_weight(wa, layer),
                  _layer_weight(wm, layer), _layer_weight(wo, layer), whole((1, D_MODEL))],
        out_specs=rows(D_MODEL),
        out_shape=jax.ShapeDtypeStruct((T, D_MODEL), F32),
        compiler_params=pltpu.CompilerParams(
            dimension_semantics=("parallel",), vmem_limit_bytes=VMEM_LIMIT),
        name="mix_out",
    )(x2, c2, qm, kv, o_att, gates, wc, wa, wm, wo, gpost)


FFN_TM = 512
FFN_TH = 256
FFN_CHUNKS = FFN_HIDDEN // FFN_TH


def _ffn_kernel(x_ref, gpre_ref, win_ref, wout_ref, gpost_ref, out_ref):
    x = x_ref[...]
    h = _rms(x, gpre_ref[...]).astype(BF16)
    acc = None
    for c in range(FFN_CHUNKS):
        lo = c * FFN_TH
        gv = jnp.dot(h, win_ref[0, :, lo:lo + FFN_TH], preferred_element_type=F32)
        uv = jnp.dot(h, win_ref[0, :, FFN_HIDDEN + lo:FFN_HIDDEN + lo + FFN_TH], preferred_element_type=F32)
        a = (gv * _sigmoid(gv) * uv).astype(BF16)
        part = jnp.dot(a, wout_ref[0, lo:lo + FFN_TH, :], preferred_element_type=F32)
        acc = part if acc is None else acc + part
    out_ref[...] = x + _rms(acc, gpost_ref[...])


def _ffn(x2, gpre, w_in, w_out, layer, gpost):
    T = x2.shape[0]

    def resident(shape):
        return pl.BlockSpec(shape, lambda i: (0, 0), pipeline_mode=pl.Buffered(1))

    return pl.pallas_call(
        _ffn_kernel,
        grid=(T // FFN_TM,),
        in_specs=[pl.BlockSpec((FFN_TM, D_MODEL), lambda i: (i, 0)),
                  resident((1, D_MODEL)),
                  _layer_weight(w_in, layer), _layer_weight(w_out, layer),
                  resident((1, D_MODEL))],
        out_specs=pl.BlockSpec((FFN_TM, D_MODEL), lambda i: (i, 0)),
        out_shape=jax.ShapeDtypeStruct((T, D_MODEL), F32),
        compiler_params=pltpu.CompilerParams(
            dimension_semantics=("parallel",), vmem_limit_bytes=VMEM_LIMIT),
        name="ffn",
    )(x2, gpre, w_in, w_out, gpost)


def kernel(x, mem, rel_bias, norm_mix_pre, w_in, b_gate, conv_dw, conv_dw_bias, conv_ln_g, conv_ln_b,
           w_conv_out, w_att_out, norm_mem, w_mem_kv, w_mem_out, w_out, norm_mix_post, norm_ffn_pre,
           w_ffn_in, w_ffn_out, norm_ffn_post):
    B, S, D = x.shape
    depth = w_in.shape[0]
    T = B * S
    biases = [_band_bias(rel_bias, g, d) for g, d in enumerate(DILATIONS)]

    def row(v):
        return v.reshape(1, -1)

    bf16 = {name: w.astype(BF16) for name, w in dict(
        w_in=w_in, w_conv_out=w_conv_out, w_att_out=w_att_out, w_mem_kv=w_mem_kv, w_mem_out=w_mem_out,
        w_out=w_out, w_ffn_in=w_ffn_in, w_ffn_out=w_ffn_out).items()}
    x2 = x.reshape(T, D)
    for l in range(depth):
        c, qm, qkv0, qkv1, qkv2, gates = _in_proj(
            x2, row(norm_mix_pre[l]), bf16["w_in"], l, row(b_gate[l]),
            conv_dw[l], row(conv_dw_bias[l]), row(conv_ln_g[l]), row(conv_ln_b[l]), B, S)
        o_att = _attention((qkv0, qkv1, qkv2), biases)
        kv = _mem_kv(mem, row(norm_mem[l]), bf16["w_mem_kv"], l)
        x2 = _mix(x2, c, qm, kv, o_att.reshape(T, ATT_OUT), gates,
                  bf16["w_conv_out"], bf16["w_att_out"], bf16["w_mem_out"], bf16["w_out"], l,
                  row(norm_mix_post[l]), S // MIX_TM)
        x2 = _ffn(x2, row(norm_ffn_pre[l]), bf16["w_ffn_in"], bf16["w_ffn_out"], l, row(norm_ffn_post[l]))
    return x2.reshape(B, S, D)
```

```python
import functools
import math

import numpy as np
import jax
import jax.numpy as jnp
from jax import lax
from jax.experimental import pallas as pl
from jax.experimental.pallas import tpu as pltpu

F32 = jnp.float32
BF16 = jnp.bfloat16

D_MODEL = 1024
CONV_WIDTH = 512
CONV_KSIZE = 31
CONV_PAD = CONV_KSIZE // 2
DILATIONS = (1, 4, 16)
RADIUS = 64
N_GROUPS = 3
HEADS = 4
HEAD_DIM = 64
ATT_OUT = HEADS * HEAD_DIM
QKV_WIDTH = 3 * ATT_OUT
MEM_HEADS = 4
MEM_HEAD_DIM = 128
MEM_WIDTH = 512
FFN_HIDDEN = 2816
NUM_BUCKETS = 32
MAX_DISTANCE = 1024
RMS_EPS = 1e-6
LN_EPS = 1e-5
NEG_INF = -1e30

GATE_WIDTH = 3 * D_MODEL
SUBQ = 128
SUBK = SUBQ + 2 * RADIUS
LANES = 128
LSE_LANES = 128
LSE_PER_HEAD = LSE_LANES // HEADS
STAT_LANES = LSE_PER_HEAD // 2
LOG2E = math.log2(math.e)

VMEM_LIMIT = 56 * 1024 * 1024


def _sigmoid(v):
    return 1.0 / (1.0 + jnp.exp(-v))


def _rms(v, g):
    return v * lax.rsqrt(jnp.mean(v * v, axis=-1, keepdims=True) + RMS_EPS) * g


def _layer_weight(w, layer):
    return pl.BlockSpec((1,) + w.shape[1:], lambda *_: (layer, 0, 0), pipeline_mode=pl.Buffered(1))


IN_TM = 512
IN_TN = 768
AG_WIDTH = 2 * CONV_WIDTH
QKV_COL = AG_WIDTH
QM_COL = QKV_COL + N_GROUPS * QKV_WIDTH
GATE_COL = QM_COL + MEM_WIDTH
N_GATE_TILES = GATE_WIDTH // IN_TN
assert IN_TN == QKV_WIDTH
SUBLANES = 8
CONV_HALO = 16
CONV_HALF = 256
CONV_RC = 64
CONV_FIRST = CONV_HALO - CONV_PAD
CONV_SHIFT_ROWS = CONV_HALF + (CONV_FIRST + CONV_KSIZE - 1) // SUBLANES * SUBLANES
IN_EXT = IN_TM + 2 * CONV_HALO


def _in_proj_kernel(x_ref, xp_ref, xn_ref, g_ref, w_ref, b_ref, cw_ref, cb_ref, lg_ref, lb_ref,
                    c_ref, qm_ref, q0_ref, q1_ref, q2_ref, gate_ref, acc_ref, u_ref, us_ref,
                    *, tiles_per_batch):
    ib = pl.program_id(0) % tiles_per_batch
    x_ext = jnp.concatenate([xp_ref[...], x_ref[...], xn_ref[...]], axis=0)
    h_ext = _rms(x_ext, g_ref[...]).astype(BF16)
    h = h_ext[CONV_HALO:CONV_HALO + IN_TM]

    ag = jnp.dot(h_ext, w_ref[0, :, 0:AG_WIDTH], preferred_element_type=F32)
    u = ag[:, 0:CONV_WIDTH] * _sigmoid(ag[:, CONV_WIDTH:])
    u_ref[0:CONV_HALO, :] = jnp.where(ib > 0, u[0:CONV_HALO], 0.0)
    u_ref[CONV_HALO:CONV_HALO + IN_TM, :] = u[CONV_HALO:CONV_HALO + IN_TM]
    u_ref[CONV_HALO + IN_TM:, :] = jnp.where(ib < tiles_per_batch - 1, u[CONV_HALO + IN_TM:], 0.0)
    for half in range(IN_TM // CONV_HALF):
        for s in range(SUBLANES):
            us_ref[s] = u_ref[half * CONV_HALF + s:half * CONV_HALF + s + CONV_SHIFT_ROWS, :]
        for c in range(CONV_HALF // CONV_RC):
            acc = jnp.zeros((CONV_RC, CONV_WIDTH), F32)
            for k in range(CONV_KSIZE):
                off = CONV_FIRST + k
                r0 = c * CONV_RC + off // SUBLANES * SUBLANES
                acc = acc + us_ref[off % SUBLANES, r0:r0 + CONV_RC, :] * cw_ref[k:k + 1, :]
            y = acc + cb_ref[...]
            mu = jnp.mean(y, axis=-1, keepdims=True)
            yc = y - mu
            yn = yc * lax.rsqrt(jnp.mean(yc * yc, axis=-1, keepdims=True) + LN_EPS)
            yn = yn * lg_ref[...] + lb_ref[...]
            rows = slice(half * CONV_HALF + c * CONV_RC, half * CONV_HALF + (c + 1) * CONV_RC)
            c_ref[rows, :] = (yn * _sigmoid(yn)).astype(BF16)

    def project(col, width):
        return jnp.dot(h, w_ref[0, :, col:col + width], preferred_element_type=F32)

    qm_ref[...] = project(QM_COL, MEM_WIDTH).astype(BF16)

    q_scale = HEAD_DIM ** -0.5 * LOG2E
    for g, (d, out_ref) in enumerate(zip(DILATIONS, (q0_ref, q1_ref, q2_ref))):
        for which in range(3):
            acc = project(QKV_COL + (which * N_GROUPS + g) * ATT_OUT, ATT_OUT)
            if which == 0:
                acc = acc * q_scale
            cols = slice(which * ATT_OUT, (which + 1) * ATT_OUT)
            if d == 1:
                out_ref[0, 0, :, cols] = acc.astype(BF16)
                continue
            for half in range(ATT_OUT // LANES):
                acc_ref[g - 1, 2 * which + half] = acc[:, half * LANES:(half + 1) * LANES]
        if d > 1:
            for r in range(d):
                for cb in range(QKV_WIDTH // LANES):
                    out_ref[0, r, :, cb * LANES:(cb + 1) * LANES] = (
                        acc_ref[g - 1, cb, pl.ds(r, IN_TM // d, stride=d), :].astype(BF16))

    for t in range(N_GATE_TILES):
        cols = slice(t * IN_TN, (t + 1) * IN_TN)
        gate_ref[:, cols] = _sigmoid(project(GATE_COL + t * IN_TN, IN_TN) + b_ref[:, cols]).astype(BF16)


def _in_proj(x2, g, w, layer, b, conv_w, conv_b, ln_g, ln_b, batch, seq):
    T = x2.shape[0]
    tiles_per_batch = seq // IN_TM
    n_strided = sum(d > 1 for d in DILATIONS)
    halo_per_tile = IN_TM // CONV_HALO
    last_halo = T // CONV_HALO - 1

    def qkv_spec(d):
        return pl.BlockSpec((1, d, IN_TM // d, QKV_WIDTH),
                            lambda i: (i // tiles_per_batch, 0, i % tiles_per_batch, 0))

    def resident(shape):
        return pl.BlockSpec(shape, lambda i: (0, 0), pipeline_mode=pl.Buffered(1))

    def rows(width):
        return pl.BlockSpec((IN_TM, width), lambda i: (i, 0))

    return pl.pallas_call(
        functools.partial(_in_proj_kernel, tiles_per_batch=tiles_per_batch),
        grid=(T // IN_TM,),
        in_specs=[
            rows(D_MODEL),
            pl.BlockSpec((CONV_HALO, D_MODEL), lambda i: (jnp.maximum(i * halo_per_tile - 1, 0), 0)),
            pl.BlockSpec((CONV_HALO, D_MODEL), lambda i: (jnp.minimum((i + 1) * halo_per_tile, last_halo), 0)),
            resident((1, D_MODEL)),
            _layer_weight(w, layer),
            resident((1, GATE_WIDTH)),
            resident((CONV_KSIZE, CONV_WIDTH)), resident((1, CONV_WIDTH)),
            resident((1, CONV_WIDTH)), resident((1, CONV_WIDTH)),
        ],
        out_specs=[rows(CONV_WIDTH), rows(MEM_WIDTH),
                   qkv_spec(DILATIONS[0]), qkv_spec(DILATIONS[1]), qkv_spec(DILATIONS[2]),
                   rows(GATE_WIDTH)],
        out_shape=[jax.ShapeDtypeStruct((T, CONV_WIDTH), BF16), jax.ShapeDtypeStruct((T, MEM_WIDTH), BF16)]
        + [jax.ShapeDtypeStruct((batch, d, seq // d, QKV_WIDTH), BF16) for d in DILATIONS]
        + [jax.ShapeDtypeStruct((T, GATE_WIDTH), BF16)],
        scratch_shapes=[pltpu.VMEM((n_strided, QKV_WIDTH // LANES, IN_TM, LANES), F32),
                        pltpu.VMEM((IN_EXT, CONV_WIDTH), F32),
                        pltpu.VMEM((SUBLANES, CONV_SHIFT_ROWS, CONV_WIDTH), F32)],
        compiler_params=pltpu.CompilerParams(
            dimension_semantics=("parallel",), vmem_limit_bytes=VMEM_LIMIT),
        name="in_proj",
    )(x2, x2, x2, g, w, b, conv_w, conv_b, ln_g, ln_b)


ATT_TILE = 2048
ATT_ILP = 16
ATT_COMBINE_ROWS = 256


def _att_kernel(c0_ref, p0_ref, n0_ref, c1_ref, p1_ref, n1_ref, c2_ref, p2_ref, n2_ref,
                b0_ref, b1_ref, b2_ref, o_ref, onat_ref, lnat_ref, edge_ref):
    i = pl.program_id(1)
    n = pl.num_programs(1)
    stat_of_lane = lax.broadcasted_iota(jnp.int32, (1, LSE_LANES), 1) // STAT_LANES
    key_col = lax.broadcasted_iota(jnp.int32, (1, SUBK), 1)
    K0, V0 = ATT_OUT, 2 * ATT_OUT
    even_head_lanes = lax.broadcasted_iota(jnp.int32, (1, LANES), 1) < HEAD_DIM

    before_start = jnp.where(key_col >= jnp.where(i == 0, RADIUS, 0), 0.0, NEG_INF)
    past_end = jnp.where(key_col < jnp.where(i == n - 1, SUBK - RADIUS, SUBK), 0.0, NEG_INF)
    for g, (d, b_ref) in enumerate(zip(DILATIONS, (b0_ref, b1_ref, b2_ref))):
        if ATT_TILE // d == SUBQ:
            edge_ref[2 * g] = b_ref[...] + before_start + past_end
        else:
            edge_ref[2 * g] = b_ref[...] + before_start
            edge_ref[2 * g + 1] = b_ref[...] + past_end

    def subtile(g, d, q, k, v, bias, row0):
        zeros = jnp.zeros((SUBQ, LANES), BF16)
        blocks = []
        for h in range(HEADS):
            tile = q[:, h // 2 * LANES:(h // 2 + 1) * LANES]
            tile = jnp.where(even_head_lanes, tile, zeros) if h % 2 == 0 else jnp.where(even_head_lanes, zeros, tile)
            blocks.append(jnp.concatenate([tile, zeros] if h < 2 else [zeros, tile], axis=1))
        qs = jnp.concatenate(blocks, axis=0)
        s = lax.dot_general(qs, k, (((1,), (1,)), ((), ())), preferred_element_type=F32) + bias
        m = jnp.max(s, axis=-1, keepdims=True)
        e = jnp.exp2(s - m)
        l = jnp.sum(e, axis=-1, keepdims=True)
        o_all = jnp.dot(e.astype(BF16), v, preferred_element_type=F32)
        o = jnp.concatenate(
            [jnp.where(even_head_lanes,
                       o_all[2 * t * SUBQ:(2 * t + 1) * SUBQ, t * LANES:(t + 1) * LANES],
                       o_all[(2 * t + 1) * SUBQ:(2 * t + 2) * SUBQ, t * LANES:(t + 1) * LANES])
             for t in range(ATT_OUT // LANES)], axis=1)
        stats = jnp.broadcast_to(l[(HEADS - 1) * SUBQ:], (SUBQ, LSE_LANES))
        stats = jnp.where(stat_of_lane == 2 * (HEADS - 1), m[(HEADS - 1) * SUBQ:], stats)
        for h in range(HEADS - 1):
            rows = slice(h * SUBQ, (h + 1) * SUBQ)
            stats = jnp.where(stat_of_lane == 2 * h, m[rows], stats)
            stats = jnp.where(stat_of_lane == 2 * h + 1, l[rows], stats)
        if d == 1:
            if not isinstance(row0, int):
                row0 = pl.multiple_of(row0, SUBQ)
            rows = pl.ds(row0, SUBQ)
        else:
            rows = pl.ds(row0, SUBQ, stride=d)
        for half in range(ATT_OUT // LANES):
            onat_ref[2 * g + half, rows, :] = o[:, half * LANES:(half + 1) * LANES]
        lnat_ref[g, rows, :] = stats

    groups = ((c0_ref, p0_ref, n0_ref, b0_ref), (c1_ref, p1_ref, n1_ref, b1_ref),
              (c2_ref, p2_ref, n2_ref, b2_ref))
    for g, (d, (c_ref, p_ref, n_ref, bias_ref)) in enumerate(zip(DILATIONS, groups)):
        tq = ATT_TILE // d
        nsub = tq // SUBQ

        def residue(r, carry, g=g, d=d, c_ref=c_ref, p_ref=p_ref, n_ref=n_ref, bias_ref=bias_ref,
                    tq=tq, nsub=nsub):
            def kv_cat(parts, col):
                return jnp.concatenate([ref[0, r, rows, col:col + ATT_OUT] for ref, rows in parts], axis=0)

            def sub(j):
                if isinstance(j, int) and (j == 0 or j == nsub - 1):
                    parts = [(c_ref, slice(max(j * SUBQ - RADIUS, 0), min(j * SUBQ + SUBQ + RADIUS, tq)))]
                    if j == 0:
                        parts = [(p_ref, slice(None))] + parts
                    if j == nsub - 1:
                        parts = parts + [(n_ref, slice(None))]
                    k, v = kv_cat(parts, K0), kv_cat(parts, V0)
                    q = c_ref[0, r, j * SUBQ:(j + 1) * SUBQ, 0:ATT_OUT]
                else:
                    q0 = j * SUBQ if isinstance(j, int) else pl.multiple_of(j * SUBQ, SUBQ)
                    k0 = j * SUBQ - RADIUS if isinstance(j, int) else pl.multiple_of(j * SUBQ - RADIUS, RADIUS)
                    q = c_ref[0, r, pl.ds(q0, SUBQ), 0:ATT_OUT]
                    k = c_ref[0, r, pl.ds(k0, SUBK), K0:K0 + ATT_OUT]
                    v = c_ref[0, r, pl.ds(k0, SUBK), V0:V0 + ATT_OUT]
                if isinstance(j, int) and j == 0:
                    bias = edge_ref[2 * g]
                elif isinstance(j, int) and j == nsub - 1:
                    bias = edge_ref[2 * g + 1]
                else:
                    bias = bias_ref[...]
                subtile(g, d, q, k, v, bias, j * SUBQ * d + r)

            if nsub <= 2 * ATT_ILP:
                for j in range(nsub):
                    sub(j)
            else:
                for j in range(ATT_ILP):
                    sub(j)

                def interior(t, carry2):
                    for u in range(ATT_ILP):
                        sub(ATT_ILP + t * ATT_ILP + u)
                    return carry2

                lax.fori_loop(0, (nsub - 2 * ATT_ILP) // ATT_ILP, interior, 0)
                for j in range(nsub - ATT_ILP, nsub):
                    sub(j)
            return carry

        if d == 1:
            residue(0, 0)
        else:
            lax.fori_loop(0, d, residue, 0, unroll=max(1, ATT_ILP // nsub))

    w_lane = lax.broadcasted_iota(jnp.int32, (2 * LSE_LANES, ATT_OUT), 0) % LSE_LANES
    o_lane = lax.broadcasted_iota(jnp.int32, (2 * LSE_LANES, ATT_OUT), 1)
    spread = (w_lane == o_lane // HEAD_DIM * LSE_PER_HEAD).astype(BF16)
    weight_lane = lax.broadcasted_iota(jnp.int32, (1, LSE_LANES), 1) % LSE_PER_HEAD == 0

    def expand(w):
        w = jnp.where(weight_lane, w, 0.0)
        hi = w.astype(BF16)
        lo = (w - hi.astype(F32)).astype(BF16)
        return jnp.dot(jnp.concatenate([hi, lo], axis=1), spread, preferred_element_type=F32)

    def combine(t, carry):
        rows = pl.ds(pl.multiple_of(t * ATT_COMBINE_ROWS, ATT_COMBINE_ROWS), ATT_COMBINE_ROWS)
        stats = [lnat_ref[g, rows, :] for g in range(N_GROUPS)]
        mx = jnp.maximum(jnp.maximum(stats[0], stats[1]), stats[2])
        a = [jnp.exp2(st - mx) for st in stats]
        sums = [pltpu.roll(st, LSE_LANES - STAT_LANES, 1) for st in stats]
        inv = 1.0 / (a[0] * sums[0] + a[1] * sums[1] + a[2] * sums[2])
        o = None
        for g in range(N_GROUPS):
            acc = jnp.concatenate([onat_ref[2 * g, rows, :], onat_ref[2 * g + 1, rows, :]], axis=1)
            term = expand(a[g] * inv) * acc
            o = term if o is None else o + term
        o_ref[0, rows, :] = o.astype(BF16)
        return carry

    lax.fori_loop(0, ATT_TILE // ATT_COMBINE_ROWS, combine, 0)


def _attention(qkv, biases):
    B = qkv[0].shape[0]
    S = qkv[0].shape[2]
    in_specs = []
    for d in DILATIONS:
        tq = ATT_TILE // d
        nb = tq // RADIUS
        last_b = S // d // RADIUS - 1
        in_specs += [
            pl.BlockSpec((1, d, tq, QKV_WIDTH), lambda b, i: (b, 0, i, 0)),
            pl.BlockSpec((1, d, RADIUS, QKV_WIDTH),
                         lambda b, i, nb=nb: (b, 0, jnp.maximum(i * nb - 1, 0), 0)),
            pl.BlockSpec((1, d, RADIUS, QKV_WIDTH),
                         lambda b, i, nb=nb, last_b=last_b: (b, 0, jnp.minimum((i + 1) * nb, last_b), 0)),
        ]
    in_specs += [pl.BlockSpec((HEADS * SUBQ, SUBK), lambda b, i: (0, 0))] * N_GROUPS
    args = []
    for a in qkv:
        args += [a, a, a]
    return pl.pallas_call(
        _att_kernel,
        grid=(B, S // ATT_TILE),
        in_specs=in_specs,
        out_specs=pl.BlockSpec((1, ATT_TILE, ATT_OUT), lambda b, i: (b, i, 0)),
        out_shape=jax.ShapeDtypeStruct((B, S, ATT_OUT), BF16),
        scratch_shapes=[pltpu.VMEM((N_GROUPS * (ATT_OUT // LANES), ATT_TILE, LANES), F32),
                        pltpu.VMEM((N_GROUPS, ATT_TILE, LSE_LANES), F32),
                        pltpu.VMEM((2 * N_GROUPS, HEADS * SUBQ, SUBK), F32)],
        compiler_params=pltpu.CompilerParams(
            dimension_semantics=("parallel", "parallel"), vmem_limit_bytes=VMEM_LIMIT),
        name="dilated_att",
    )(*args, *biases)


def _t5_bucket_np(rel):
    nb = NUM_BUCKETS // 2
    max_exact = nb // 2
    ret = np.where(rel > 0, nb, 0)
    n = np.abs(rel)
    nf = np.maximum(n, 1).astype(np.float32)
    ratio = np.log(nf / np.float32(max_exact)) / np.float32(math.log(MAX_DISTANCE / max_exact))
    large = max_exact + (ratio * np.float32(nb - max_exact)).astype(np.int32)
    large = np.minimum(large, nb - 1)
    return ret + np.where(n < max_exact, n, large)


def _band_bias(rel_bias, g, dilation):
    period = SUBQ + SUBK
    nband = 2 * RADIUS + 1
    bucket = _t5_bucket_np((np.arange(nband) - RADIUS) * dilation)
    onehot = np.zeros((period, NUM_BUCKETS), np.float32)
    onehot[np.arange(nband), bucket] = 1.0
    tab = rel_bias[:, g * HEADS:(g + 1) * HEADS].astype(F32)
    t = jnp.dot(jnp.asarray(onehot), tab, precision=lax.Precision.HIGHEST)
    t = jnp.where((np.arange(period) < nband)[:, None], t, NEG_INF).T
    skew = jnp.tile(t, (1, SUBQ))[:, :SUBQ * (period - 1)].reshape(HEADS, SUBQ, period - 1)
    return skew[:, :, :SUBK].reshape(HEADS * SUBQ, SUBK) * LOG2E


def _mem_kv_kernel(mem_ref, g_ref, w_ref, kv_ref):
    h = _rms(mem_ref[0], g_ref[...]).astype(BF16)
    kv_ref[0] = jnp.dot(h, w_ref[0], preferred_element_type=F32).astype(BF16)


def _mem_kv(mem, g, w, layer):
    B, M, _ = mem.shape
    return pl.pallas_call(
        _mem_kv_kernel,
        grid=(B,),
        in_specs=[pl.BlockSpec((1, M, D_MODEL), lambda b: (b, 0, 0)),
                  pl.BlockSpec((1, D_MODEL), lambda b: (0, 0)),
                  _layer_weight(w, layer)],
        out_specs=pl.BlockSpec((1, M, 2 * MEM_WIDTH), lambda b: (b, 0, 0)),
        out_shape=jax.ShapeDtypeStruct((B, M, 2 * MEM_WIDTH), BF16),
        compiler_params=pltpu.CompilerParams(
            dimension_semantics=("parallel",), vmem_limit_bytes=VMEM_LIMIT),
        name="mem_kv",
    )(mem, g, w)


MIX_TM = 512


def _mix_kernel(x_ref, c_ref, qm_ref, kv_ref, oatt_ref, gate_ref, wc_ref, wa_ref, wm_ref, wo_ref,
                gpost_ref, out_ref):
    scale = MEM_HEAD_DIM ** -0.5
    heads = []
    for h in range(MEM_HEADS):
        lo = h * MEM_HEAD_DIM
        qh = qm_ref[:, lo:lo + MEM_HEAD_DIM]
        kh = kv_ref[0, :, lo:lo + MEM_HEAD_DIM]
        vh = kv_ref[0, :, MEM_WIDTH + lo:MEM_WIDTH + lo + MEM_HEAD_DIM]
        s = lax.dot_general(qh, kh, (((1,), (1,)), ((), ())), preferred_element_type=F32) * scale
        m = jnp.max(s, axis=-1, keepdims=True)
        e = jnp.exp(s - m)
        p = (e * (1.0 / jnp.sum(e, axis=-1, keepdims=True))).astype(BF16)
        heads.append(jnp.dot(p, vh, preferred_element_type=F32).astype(BF16))
    o_mem = jnp.concatenate(heads, axis=-1)
    y_mem = jnp.dot(o_mem, wm_ref[0], preferred_element_type=F32)
    y_conv = jnp.dot(c_ref[...], wc_ref[0], preferred_element_type=F32)
    y_att = jnp.dot(oatt_ref[...], wa_ref[0], preferred_element_type=F32)
    merged = (gate_ref[:, 0:D_MODEL].astype(F32) * y_conv
              + gate_ref[:, D_MODEL:2 * D_MODEL].astype(F32) * y_att
              + gate_ref[:, 2 * D_MODEL:].astype(F32) * y_mem)
    y = jnp.dot(merged.astype(BF16), wo_ref[0], preferred_element_type=F32)
    out_ref[...] = x_ref[...] + _rms(y, gpost_ref[...])


def _mix(x2, c2, qm, kv, o_att, gates, wc, wa, wm, wo, layer, gpost, tiles_per_batch):
    T = x2.shape[0]
    M = kv.shape[1]

    def rows(width):
        return pl.BlockSpec((MIX_TM, width), lambda i: (i, 0))

    def whole(shape):
        return pl.BlockSpec(shape, lambda i: (0,) * len(shape))

    return pl.pallas_call(
        _mix_kernel,
        grid=(T // MIX_TM,),
        in_specs=[rows(D_MODEL), rows(CONV_WIDTH), rows(MEM_WIDTH),
                  pl.BlockSpec((1, M, 2 * MEM_WIDTH), lambda i: (i // tiles_per_batch, 0, 0)),
                  rows(ATT_OUT), rows(GATE_WIDTH),
                  _layer_weight(wc, layer), _layer_weight(wa, layer),
                  _layer_weight(wm, layer), _layer_weight(wo, layer), whole((1, D_MODEL))],
        out_specs=rows(D_MODEL),
        out_shape=jax.ShapeDtypeStruct((T, D_MODEL), F32),
        compiler_params=pltpu.CompilerParams(
            dimension_semantics=("parallel",), vmem_limit_bytes=VMEM_LIMIT),
        name="mix_out",
    )(x2, c2, qm, kv, o_att, gates, wc, wa, wm, wo, gpost)


FFN_TM = 512
FFN_TH = 256
FFN_CHUNKS = FFN_HIDDEN // FFN_TH


def _ffn_kernel(x_ref, gpre_ref, win_ref, wout_ref, gpost_ref, out_ref):
    x = x_ref[...]
    h = _rms(x, gpre_ref[...]).astype(BF16)
    acc = None
    for c in range(FFN_CHUNKS):
        lo = c * FFN_TH
        gv = jnp.dot(h, win_ref[0, :, lo:lo + FFN_TH], preferred_element_type=F32)
        uv = jnp.dot(h, win_ref[0, :, FFN_HIDDEN + lo:FFN_HIDDEN + lo + FFN_TH], preferred_element_type=F32)
        a = (gv * _sigmoid(gv) * uv).astype(BF16)
        part = jnp.dot(a, wout_ref[0, lo:lo + FFN_TH, :], preferred_element_type=F32)
        acc = part if acc is None else acc + part
    out_ref[...] = x + _rms(acc, gpost_ref[...])


def _ffn(x2, gpre, w_in, w_out, layer, gpost):
    T = x2.shape[0]

    def resident(shape):
        return pl.BlockSpec(shape, lambda i: (0, 0), pipeline_mode=pl.Buffered(1))

    return pl.pallas_call(
        _ffn_kernel,
        grid=(T // FFN_TM,),
        in_specs=[pl.BlockSpec((FFN_TM, D_MODEL), lambda i: (i, 0)),
                  resident((1, D_MODEL)),
                  _layer_weight(w_in, layer), _layer_weight(w_out, layer),
                  resident((1, D_MODEL))],
        out_specs=pl.BlockSpec((FFN_TM, D_MODEL), lambda i: (i, 0)),
        out_shape=jax.ShapeDtypeStruct((T, D_MODEL), F32),
        compiler_params=pltpu.CompilerParams(
            dimension_semantics=("parallel",), vmem_limit_bytes=VMEM_LIMIT),
        name="ffn",
    )(x2, gpre, w_in, w_out, gpost)


def kernel(x, mem, rel_bias, norm_mix_pre, w_in, b_gate, conv_dw, conv_dw_bias, conv_ln_g, conv_ln_b,
           w_conv_out, w_att_out, norm_mem, w_mem_kv, w_mem_out, w_out, norm_mix_post, norm_ffn_pre,
           w_ffn_in, w_ffn_out, norm_ffn_post):
    B, S, D = x.shape
    depth = w_in.shape[0]
    T = B * S
    biases = [_band_bias(rel_bias, g, d) for g, d in enumerate(DILATIONS)]

    def row(v):
        return v.reshape(1, -1)

    bf16 = {name: w.astype(BF16) for name, w in dict(
        w_in=w_in, w_conv_out=w_conv_out, w_att_out=w_att_out, w_mem_kv=w_mem_kv, w_mem_out=w_mem_out,
        w_out=w_out, w_ffn_in=w_ffn_in, w_ffn_out=w_ffn_out).items()}
    x2 = x.reshape(T, D)
    for l in range(depth):
        c, qm, qkv0, qkv1, qkv2, gates = _in_proj(
            x2, row(norm_mix_pre[l]), bf16["w_in"], l, row(b_gate[l]),
            conv_dw[l], row(conv_dw_bias[l]), row(conv_ln_g[l]), row(conv_ln_b[l]), B, S)
        o_att = _attention((qkv0, qkv1, qkv2), biases)
        kv = _mem_kv(mem, row(norm_mem[l]), bf16["w_mem_kv"], l)
        x2 = _mix(x2, c, qm, kv, o_att.reshape(T, ATT_OUT), gates,
                  bf16["w_conv_out"], bf16["w_att_out"], bf16["w_mem_out"], bf16["w_out"], l,
                  row(norm_mix_post[l]), S // MIX_TM)
        x2 = _ffn(x2, row(norm_ffn_pre[l]), bf16["w_ffn_in"], bf16["w_ffn_out"], l, row(norm_ffn_post[l]))
    return x2.reshape(B, S, D)
```

```python
import functools
import math

import numpy as np
import jax
import jax.numpy as jnp
from jax import lax
from jax.experimental import pallas as pl
from jax.experimental.pallas import tpu as pltpu

F32 = jnp.float32
BF16 = jnp.bfloat16

D_MODEL = 1024
CONV_WIDTH = 512
CONV_KSIZE = 31
CONV_PAD = CONV_KSIZE // 2
DILATIONS = (1, 4, 16)
RADIUS = 64
N_GROUPS = 3
HEADS = 4
HEAD_DIM = 64
ATT_OUT = HEADS * HEAD_DIM
QKV_WIDTH = 3 * ATT_OUT
MEM_HEADS = 4
MEM_HEAD_DIM = 128
MEM_WIDTH = 512
FFN_HIDDEN = 2816
NUM_BUCKETS = 32
MAX_DISTANCE = 1024
RMS_EPS = 1e-6
LN_EPS = 1e-5
NEG_INF = -1e30

GATE_WIDTH = 3 * D_MODEL
SUBQ = 128
SUBK = SUBQ + 2 * RADIUS
LANES = 128
LSE_LANES = 128
LSE_PER_HEAD = LSE_LANES // HEADS
STAT_LANES = LSE_PER_HEAD // 2
LOG2E = math.log2(math.e)

VMEM_LIMIT = 56 * 1024 * 1024


def _sigmoid(v):
    return 1.0 / (1.0 + jnp.exp(-v))


def _rms(v, g):
    return v * lax.rsqrt(jnp.mean(v * v, axis=-1, keepdims=True) + RMS_EPS) * g


def _layer_weight(w, layer):
    return pl.BlockSpec((1,) + w.shape[1:], lambda *_: (layer, 0, 0), pipeline_mode=pl.Buffered(1))


IN_TM = 512
IN_TN = 768
AG_WIDTH = 2 * CONV_WIDTH
QKV_COL = AG_WIDTH
QM_COL = QKV_COL + N_GROUPS * QKV_WIDTH
GATE_COL = QM_COL + MEM_WIDTH
N_GATE_TILES = GATE_WIDTH // IN_TN
assert IN_TN == QKV_WIDTH
SUBLANES = 8
CONV_HALO = 16
CONV_HALF = 256
CONV_RC = 64
CONV_FIRST = CONV_HALO - CONV_PAD
CONV_SHIFT_ROWS = CONV_HALF + (CONV_FIRST + CONV_KSIZE - 1) // SUBLANES * SUBLANES
IN_EXT = IN_TM + 2 * CONV_HALO


def _in_proj_kernel(x_ref, xp_ref, xn_ref, g_ref, w_ref, b_ref, cw_ref, cb_ref, lg_ref, lb_ref,
                    c_ref, qm_ref, q0_ref, q1_ref, q2_ref, gate_ref, acc_ref, u_ref, us_ref,
                    *, tiles_per_batch):
    ib = pl.program_id(0) % tiles_per_batch
    x_ext = jnp.concatenate([xp_ref[...], x_ref[...], xn_ref[...]], axis=0)
    h_ext = _rms(x_ext, g_ref[...]).astype(BF16)
    h = h_ext[CONV_HALO:CONV_HALO + IN_TM]

    ag = jnp.dot(h_ext, w_ref[0, :, 0:AG_WIDTH], preferred_element_type=F32)
    u = ag[:, 0:CONV_WIDTH] * _sigmoid(ag[:, CONV_WIDTH:])
    u_ref[0:CONV_HALO, :] = jnp.where(ib > 0, u[0:CONV_HALO], 0.0)
    u_ref[CONV_HALO:CONV_HALO + IN_TM, :] = u[CONV_HALO:CONV_HALO + IN_TM]
    u_ref[CONV_HALO + IN_TM:, :] = jnp.where(ib < tiles_per_batch - 1, u[CONV_HALO + IN_TM:], 0.0)
    def shift_copies(half):
        for s in range(SUBLANES):
            us_ref[s] = u_ref[half * CONV_HALF + s:half * CONV_HALF + s + CONV_SHIFT_ROWS, :]

    def conv_chunk(half, c):
        acc = jnp.zeros((CONV_RC, CONV_WIDTH), F32)
        for k in range(CONV_KSIZE):
            off = CONV_FIRST + k
            r0 = c * CONV_RC + off // SUBLANES * SUBLANES
            acc = acc + us_ref[off % SUBLANES, r0:r0 + CONV_RC, :] * cw_ref[k:k + 1, :]
        y = acc + cb_ref[...]
        mu = jnp.mean(y, axis=-1, keepdims=True)
        yc = y - mu
        yn = yc * lax.rsqrt(jnp.mean(yc * yc, axis=-1, keepdims=True) + LN_EPS)
        yn = yn * lg_ref[...] + lb_ref[...]
        rows = slice(half * CONV_HALF + c * CONV_RC, half * CONV_HALF + (c + 1) * CONV_RC)
        c_ref[rows, :] = (yn * _sigmoid(yn)).astype(BF16)

    def project(col, width):
        return jnp.dot(h, w_ref[0, :, col:col + width], preferred_element_type=F32)

    def mem_queries():
        qm_ref[...] = project(QM_COL, MEM_WIDTH).astype(BF16)

    q_scale = HEAD_DIM ** -0.5 * LOG2E

    def qkv_group(g):
        d, out_ref = DILATIONS[g], (q0_ref, q1_ref, q2_ref)[g]
        for which in range(3):
            acc = project(QKV_COL + (which * N_GROUPS + g) * ATT_OUT, ATT_OUT)
            if which == 0:
                acc = acc * q_scale
            cols = slice(which * ATT_OUT, (which + 1) * ATT_OUT)
            if d == 1:
                out_ref[0, 0, :, cols] = acc.astype(BF16)
                continue
            for half in range(ATT_OUT // LANES):
                acc_ref[g - 1, 2 * which + half] = acc[:, half * LANES:(half + 1) * LANES]
        if d > 1:
            for r in range(d):
                for cb in range(QKV_WIDTH // LANES):
                    out_ref[0, r, :, cb * LANES:(cb + 1) * LANES] = (
                        acc_ref[g - 1, cb, pl.ds(r, IN_TM // d, stride=d), :].astype(BF16))

    def gate_tile(t):
        cols = slice(t * IN_TN, (t + 1) * IN_TN)
        gate_ref[:, cols] = _sigmoid(project(GATE_COL + t * IN_TN, IN_TN) + b_ref[:, cols]).astype(BF16)

    mxu_units = ([mem_queries] + [functools.partial(qkv_group, g) for g in range(N_GROUPS)]
                 + [functools.partial(gate_tile, t) for t in range(N_GATE_TILES)])
    chunks = [(half, c) for half in range(IN_TM // CONV_HALF) for c in range(CONV_HALF // CONV_RC)]
    for step in range(max(len(mxu_units), len(chunks))):
        if step < len(chunks):
            half, c = chunks[step]
            if c == 0:
                shift_copies(half)
            conv_chunk(half, c)
        if step < len(mxu_units):
            mxu_units[step]()


def _in_proj(x2, g, w, layer, b, conv_w, conv_b, ln_g, ln_b, batch, seq):
    T = x2.shape[0]
    tiles_per_batch = seq // IN_TM
    n_strided = sum(d > 1 for d in DILATIONS)
    halo_per_tile = IN_TM // CONV_HALO
    last_halo = T // CONV_HALO - 1

    def qkv_spec(d):
        return pl.BlockSpec((1, d, IN_TM // d, QKV_WIDTH),
                            lambda i: (i // tiles_per_batch, 0, i % tiles_per_batch, 0))

    def resident(shape):
        return pl.BlockSpec(shape, lambda i: (0, 0), pipeline_mode=pl.Buffered(1))

    def rows(width):
        return pl.BlockSpec((IN_TM, width), lambda i: (i, 0))

    return pl.pallas_call(
        functools.partial(_in_proj_kernel, tiles_per_batch=tiles_per_batch),
        grid=(T // IN_TM,),
        in_specs=[
            rows(D_MODEL),
            pl.BlockSpec((CONV_HALO, D_MODEL), lambda i: (jnp.maximum(i * halo_per_tile - 1, 0), 0)),
            pl.BlockSpec((CONV_HALO, D_MODEL), lambda i: (jnp.minimum((i + 1) * halo_per_tile, last_halo), 0)),
            resident((1, D_MODEL)),
            _layer_weight(w, layer),
            resident((1, GATE_WIDTH)),
            resident((CONV_KSIZE, CONV_WIDTH)), resident((1, CONV_WIDTH)),
            resident((1, CONV_WIDTH)), resident((1, CONV_WIDTH)),
        ],
        out_specs=[rows(CONV_WIDTH), rows(MEM_WIDTH),
                   qkv_spec(DILATIONS[0]), qkv_spec(DILATIONS[1]), qkv_spec(DILATIONS[2]),
                   rows(GATE_WIDTH)],
        out_shape=[jax.ShapeDtypeStruct((T, CONV_WIDTH), BF16), jax.ShapeDtypeStruct((T, MEM_WIDTH), BF16)]
        + [jax.ShapeDtypeStruct((batch, d, seq // d, QKV_WIDTH), BF16) for d in DILATIONS]
        + [jax.ShapeDtypeStruct((T, GATE_WIDTH), BF16)],
        scratch_shapes=[pltpu.VMEM((n_strided, QKV_WIDTH // LANES, IN_TM, LANES), F32),
                        pltpu.VMEM((IN_EXT, CONV_WIDTH), F32),
                        pltpu.VMEM((SUBLANES, CONV_SHIFT_ROWS, CONV_WIDTH), F32)],
        compiler_params=pltpu.CompilerParams(
            dimension_semantics=("parallel",), vmem_limit_bytes=VMEM_LIMIT),
        name="in_proj",
    )(x2, x2, x2, g, w, b, conv_w, conv_b, ln_g, ln_b)


ATT_TILE = 2048
ATT_ILP = 16
ATT_COMBINE_ROWS = 256


def _att_kernel(c0_ref, p0_ref, n0_ref, c1_ref, p1_ref, n1_ref, c2_ref, p2_ref, n2_ref,
                b0_ref, b1_ref, b2_ref, o_ref, onat_ref, lnat_ref, edge_ref):
    i = pl.program_id(1)
    n = pl.num_programs(1)
    stat_of_lane = lax.broadcasted_iota(jnp.int32, (1, LSE_LANES), 1) // STAT_LANES
    key_col = lax.broadcasted_iota(jnp.int32, (1, SUBK), 1)
    K0, V0 = ATT_OUT, 2 * ATT_OUT
    even_head_lanes = lax.broadcasted_iota(jnp.int32, (1, LANES), 1) < HEAD_DIM

    before_start = jnp.where(key_col >= jnp.where(i == 0, RADIUS, 0), 0.0, NEG_INF)
    past_end = jnp.where(key_col < jnp.where(i == n - 1, SUBK - RADIUS, SUBK), 0.0, NEG_INF)
    for g, (d, b_ref) in enumerate(zip(DILATIONS, (b0_ref, b1_ref, b2_ref))):
        if ATT_TILE // d == SUBQ:
            edge_ref[2 * g] = b_ref[...] + before_start + past_end
        else:
            edge_ref[2 * g] = b_ref[...] + before_start
            edge_ref[2 * g + 1] = b_ref[...] + past_end

    def subtile(g, d, q, k, v, bias, row0):
        zeros = jnp.zeros((SUBQ, LANES), BF16)
        blocks = []
        for h in range(HEADS):
            tile = q[:, h // 2 * LANES:(h // 2 + 1) * LANES]
            tile = jnp.where(even_head_lanes, tile, zeros) if h % 2 == 0 else jnp.where(even_head_lanes, zeros, tile)
            blocks.append(jnp.concatenate([tile, zeros] if h < 2 else [zeros, tile], axis=1))
        qs = jnp.concatenate(blocks, axis=0)
        s = lax.dot_general(qs, k, (((1,), (1,)), ((), ())), preferred_element_type=F32) + bias
        m = jnp.max(s, axis=-1, keepdims=True)
        e = jnp.exp2(s - m)
        l = jnp.sum(e, axis=-1, keepdims=True)
        o_all = jnp.dot(e.astype(BF16), v, preferred_element_type=F32)
        o = jnp.concatenate(
            [jnp.where(even_head_lanes,
                       o_all[2 * t * SUBQ:(2 * t + 1) * SUBQ, t * LANES:(t + 1) * LANES],
                       o_all[(2 * t + 1) * SUBQ:(2 * t + 2) * SUBQ, t * LANES:(t + 1) * LANES])
             for t in range(ATT_OUT // LANES)], axis=1)
        stats = jnp.broadcast_to(l[(HEADS - 1) * SUBQ:], (SUBQ, LSE_LANES))
        stats = jnp.where(stat_of_lane == 2 * (HEADS - 1), m[(HEADS - 1) * SUBQ:], stats)
        for h in range(HEADS - 1):
            rows = slice(h * SUBQ, (h + 1) * SUBQ)
            stats = jnp.where(stat_of_lane == 2 * h, m[rows], stats)
            stats = jnp.where(stat_of_lane == 2 * h + 1, l[rows], stats)
        if d == 1:
            if not isinstance(row0, int):
                row0 = pl.multiple_of(row0, SUBQ)
            rows = pl.ds(row0, SUBQ)
        else:
            rows = pl.ds(row0, SUBQ, stride=d)
        for half in range(ATT_OUT // LANES):
            onat_ref[2 * g + half, rows, :] = o[:, half * LANES:(half + 1) * LANES]
        lnat_ref[g, rows, :] = stats

    groups = ((c0_ref, p0_ref, n0_ref, b0_ref), (c1_ref, p1_ref, n1_ref, b1_ref),
              (c2_ref, p2_ref, n2_ref, b2_ref))
    for g, (d, (c_ref, p_ref, n_ref, bias_ref)) in enumerate(zip(DILATIONS, groups)):
        tq = ATT_TILE // d
        nsub = tq // SUBQ

        def residue(r, carry, g=g, d=d, c_ref=c_ref, p_ref=p_ref, n_ref=n_ref, bias_ref=bias_ref,
                    tq=tq, nsub=nsub):
            def kv_cat(parts, col):
                return jnp.concatenate([ref[0, r, rows, col:col + ATT_OUT] for ref, rows in parts], axis=0)

            def sub(j):
                if isinstance(j, int) and (j == 0 or j == nsub - 1):
                    parts = [(c_ref, slice(max(j * SUBQ - RADIUS, 0), min(j * SUBQ + SUBQ + RADIUS, tq)))]
                    if j == 0:
                        parts = [(p_ref, slice(None))] + parts
                    if j == nsub - 1:
                        parts = parts + [(n_ref, slice(None))]
                    k, v = kv_cat(parts, K0), kv_cat(parts, V0)
                    q = c_ref[0, r, j * SUBQ:(j + 1) * SUBQ, 0:ATT_OUT]
                else:
                    q0 = j * SUBQ if isinstance(j, int) else pl.multiple_of(j * SUBQ, SUBQ)
                    k0 = j * SUBQ - RADIUS if isinstance(j, int) else pl.multiple_of(j * SUBQ - RADIUS, RADIUS)
                    q = c_ref[0, r, pl.ds(q0, SUBQ), 0:ATT_OUT]
                    k = c_ref[0, r, pl.ds(k0, SUBK), K0:K0 + ATT_OUT]
                    v = c_ref[0, r, pl.ds(k0, SUBK), V0:V0 + ATT_OUT]
                if isinstance(j, int) and j == 0:
                    bias = edge_ref[2 * g]
                elif isinstance(j, int) and j == nsub - 1:
                    bias = edge_ref[2 * g + 1]
                else:
                    bias = bias_ref[...]
                subtile(g, d, q, k, v, bias, j * SUBQ * d + r)

            if nsub <= 2 * ATT_ILP:
                for j in range(nsub):
                    sub(j)
            else:
                for j in range(ATT_ILP):
                    sub(j)

                def interior(t, carry2):
                    for u in range(ATT_ILP):
                        sub(ATT_ILP + t * ATT_ILP + u)
                    return carry2

                lax.fori_loop(0, (nsub - 2 * ATT_ILP) // ATT_ILP, interior, 0)
                for j in range(nsub - ATT_ILP, nsub):
                    sub(j)
            return carry

        if d == 1:
            residue(0, 0)
        else:
            lax.fori_loop(0, d, residue, 0, unroll=max(1, ATT_ILP // nsub))

    w_lane = lax.broadcasted_iota(jnp.int32, (2 * LSE_LANES, ATT_OUT), 0) % LSE_LANES
    o_lane = lax.broadcasted_iota(jnp.int32, (2 * LSE_LANES, ATT_OUT), 1)
    spread = (w_lane == o_lane // HEAD_DIM * LSE_PER_HEAD).astype(BF16)
    weight_lane = lax.broadcasted_iota(jnp.int32, (1, LSE_LANES), 1) % LSE_PER_HEAD == 0

    def expand(w):
        w = jnp.where(weight_lane, w, 0.0)
        hi = w.astype(BF16)
        lo = (w - hi.astype(F32)).astype(BF16)
        return jnp.dot(jnp.concatenate([hi, lo], axis=1), spread, preferred_element_type=F32)

    def combine(t, carry):
        rows = pl.ds(pl.multiple_of(t * ATT_COMBINE_ROWS, ATT_COMBINE_ROWS), ATT_COMBINE_ROWS)
        stats = [lnat_ref[g, rows, :] for g in range(N_GROUPS)]
        mx = jnp.maximum(jnp.maximum(stats[0], stats[1]), stats[2])
        a = [jnp.exp2(st - mx) for st in stats]
        sums = [pltpu.roll(st, LSE_LANES - STAT_LANES, 1) for st in stats]
        inv = 1.0 / (a[0] * sums[0] + a[1] * sums[1] + a[2] * sums[2])
        o = None
        for g in range(N_GROUPS):
            acc = jnp.concatenate([onat_ref[2 * g, rows, :], onat_ref[2 * g + 1, rows, :]], axis=1)
            term = expand(a[g] * inv) * acc
            o = term if o is None else o + term
        o_ref[0, rows, :] = o.astype(BF16)
        return carry

    lax.fori_loop(0, ATT_TILE // ATT_COMBINE_ROWS, combine, 0)


def _attention(qkv, biases):
    B = qkv[0].shape[0]
    S = qkv[0].shape[2]
    in_specs = []
    for d in DILATIONS:
        tq = ATT_TILE // d
        nb = tq // RADIUS
        last_b = S // d // RADIUS - 1
        in_specs += [
            pl.BlockSpec((1, d, tq, QKV_WIDTH), lambda b, i: (b, 0, i, 0)),
            pl.BlockSpec((1, d, RADIUS, QKV_WIDTH),
                         lambda b, i, nb=nb: (b, 0, jnp.maximum(i * nb - 1, 0), 0)),
            pl.BlockSpec((1, d, RADIUS, QKV_WIDTH),
                         lambda b, i, nb=nb, last_b=last_b: (b, 0, jnp.minimum((i + 1) * nb, last_b), 0)),
        ]
    in_specs += [pl.BlockSpec((HEADS * SUBQ, SUBK), lambda b, i: (0, 0))] * N_GROUPS
    args = []
    for a in qkv:
        args += [a, a, a]
    return pl.pallas_call(
        _att_kernel,
        grid=(B, S // ATT_TILE),
        in_specs=in_specs,
        out_specs=pl.BlockSpec((1, ATT_TILE, ATT_OUT), lambda b, i: (b, i, 0)),
        out_shape=jax.ShapeDtypeStruct((B, S, ATT_OUT), BF16),
        scratch_shapes=[pltpu.VMEM((N_GROUPS * (ATT_OUT // LANES), ATT_TILE, LANES), F32),
                        pltpu.VMEM((N_GROUPS, ATT_TILE, LSE_LANES), F32),
                        pltpu.VMEM((2 * N_GROUPS, HEADS * SUBQ, SUBK), F32)],
        compiler_params=pltpu.CompilerParams(
            dimension_semantics=("parallel", "parallel"), vmem_limit_bytes=VMEM_LIMIT),
        name="dilated_att",
    )(*args, *biases)


def _t5_bucket_np(rel):
    nb = NUM_BUCKETS // 2
    max_exact = nb // 2
    ret = np.where(rel > 0, nb, 0)
    n = np.abs(rel)
    nf = np.maximum(n, 1).astype(np.float32)
    ratio = np.log(nf / np.float32(max_exact)) / np.float32(math.log(MAX_DISTANCE / max_exact))
    large = max_exact + (ratio * np.float32(nb - max_exact)).astype(np.int32)
    large = np.minimum(large, nb - 1)
    return ret + np.where(n < max_exact, n, large)


def _band_bias(rel_bias, g, dilation):
    period = SUBQ + SUBK
    nband = 2 * RADIUS + 1
    bucket = _t5_bucket_np((np.arange(nband) - RADIUS) * dilation)
    onehot = np.zeros((period, NUM_BUCKETS), np.float32)
    onehot[np.arange(nband), bucket] = 1.0
    tab = rel_bias[:, g * HEADS:(g + 1) * HEADS].astype(F32)
    t = jnp.dot(jnp.asarray(onehot), tab, precision=lax.Precision.HIGHEST)
    t = jnp.where((np.arange(period) < nband)[:, None], t, NEG_INF).T
    skew = jnp.tile(t, (1, SUBQ))[:, :SUBQ * (period - 1)].reshape(HEADS, SUBQ, period - 1)
    return skew[:, :, :SUBK].reshape(HEADS * SUBQ, SUBK) * LOG2E


def _mem_kv_kernel(mem_ref, g_ref, w_ref, kv_ref):
    h = _rms(mem_ref[0], g_ref[...]).astype(BF16)
    kv_ref[0] = jnp.dot(h, w_ref[0], preferred_element_type=F32).astype(BF16)


def _mem_kv(mem, g, w, layer):
    B, M, _ = mem.shape
    return pl.pallas_call(
        _mem_kv_kernel,
        grid=(B,),
        in_specs=[pl.BlockSpec((1, M, D_MODEL), lambda b: (b, 0, 0)),
                  pl.BlockSpec((1, D_MODEL), lambda b: (0, 0)),
                  _layer_weight(w, layer)],
        out_specs=pl.BlockSpec((1, M, 2 * MEM_WIDTH), lambda b: (b, 0, 0)),
        out_shape=jax.ShapeDtypeStruct((B, M, 2 * MEM_WIDTH), BF16),
        compiler_params=pltpu.CompilerParams(
            dimension_semantics=("parallel",), vmem_limit_bytes=VMEM_LIMIT),
        name="mem_kv",
    )(mem, g, w)


MIX_TM = 1024
MIX_RB = 512
MIX_TN = 256


def _mix_kernel(x_ref, c_ref, qm_ref, kv_ref, oatt_ref, gate_ref, wc_ref, wa_ref, wm_ref, wo_ref,
                gpost_ref, out_ref):
    scale = MEM_HEAD_DIM ** -0.5
    for rb in range(MIX_TM // MIX_RB):
        rows = slice(rb * MIX_RB, (rb + 1) * MIX_RB)
        heads = []
        for h in range(MEM_HEADS):
            lo = h * MEM_HEAD_DIM
            qh = qm_ref[rows, lo:lo + MEM_HEAD_DIM]
            kh = kv_ref[0, :, lo:lo + MEM_HEAD_DIM]
            vh = kv_ref[0, :, MEM_WIDTH + lo:MEM_WIDTH + lo + MEM_HEAD_DIM]
            s = lax.dot_general(qh, kh, (((1,), (1,)), ((), ())), preferred_element_type=F32) * scale
            m = jnp.max(s, axis=-1, keepdims=True)
            e = jnp.exp(s - m)
            p = (e * (1.0 / jnp.sum(e, axis=-1, keepdims=True))).astype(BF16)
            heads.append(jnp.dot(p, vh, preferred_element_type=F32).astype(BF16))
        o_mem = jnp.concatenate(heads, axis=-1)
        y = None
        for t in range(D_MODEL // MIX_TN):
            cols = slice(t * MIX_TN, (t + 1) * MIX_TN)

            def gate(branch):
                lo = branch * D_MODEL + t * MIX_TN
                return gate_ref[rows, lo:lo + MIX_TN].astype(F32)

            merged = (gate(0) * jnp.dot(c_ref[rows, :], wc_ref[0, :, cols], preferred_element_type=F32)
                      + gate(1) * jnp.dot(oatt_ref[rows, :], wa_ref[0, :, cols], preferred_element_type=F32)
                      + gate(2) * jnp.dot(o_mem, wm_ref[0, :, cols], preferred_element_type=F32))
            part = jnp.dot(merged.astype(BF16), wo_ref[0, cols, :], preferred_element_type=F32)
            y = part if y is None else y + part
        out_ref[rows, :] = x_ref[rows, :] + _rms(y, gpost_ref[...])


def _mix(x2, c2, qm, kv, o_att, gates, wc, wa, wm, wo, layer, gpost, tiles_per_batch):
    T = x2.shape[0]
    M = kv.shape[1]

    def rows(width):
        return pl.BlockSpec((MIX_TM, width), lambda i: (i, 0))

    def whole(shape):
        return pl.BlockSpec(shape, lambda i: (0,) * len(shape))

    return pl.pallas_call(
        _mix_kernel,
        grid=(T // MIX_TM,),
        in_specs=[rows(D_MODEL), rows(CONV_WIDTH), rows(MEM_WIDTH),
                  pl.BlockSpec((1, M, 2 * MEM_WIDTH), lambda i: (i // tiles_per_batch, 0, 0)),
                  rows(ATT_OUT), rows(GATE_WIDTH),
                  _layer_weight(wc, layer), _layer_weight(wa, layer),
                  _layer_weight(wm, layer), _layer_weight(wo, layer), whole((1, D_MODEL))],
        out_specs=rows(D_MODEL),
        out_shape=jax.ShapeDtypeStruct((T, D_MODEL), F32),
        compiler_params=pltpu.CompilerParams(
            dimension_semantics=("parallel",), vmem_limit_bytes=VMEM_LIMIT),
        name="mix_out",
    )(x2, c2, qm, kv, o_att, gates, wc, wa, wm, wo, gpost)


FFN_TM = 512
FFN_TH = 256
FFN_CHUNKS = FFN_HIDDEN // FFN_TH


def _ffn_kernel(x_ref, gpre_ref, win_ref, wout_ref, gpost_ref, out_ref):
    x = x_ref[...]
    h = _rms(x, gpre_ref[...]).astype(BF16)
    acc = None
    for c in range(FFN_CHUNKS):
        lo = c * FFN_TH
        gv = jnp.dot(h, win_ref[0, :, lo:lo + FFN_TH], preferred_element_type=F32)
        uv = jnp.dot(h, win_ref[0, :, FFN_HIDDEN + lo:FFN_HIDDEN + lo + FFN_TH], preferred_element_type=F32)
        a = (gv * _sigmoid(gv) * uv).astype(BF16)
        part = jnp.dot(a, wout_ref[0, lo:lo + FFN_TH, :], preferred_element_type=F32)
        acc = part if acc is None else acc + part
    out_ref[...] = x + _rms(acc, gpost_ref[...])


def _ffn(x2, gpre, w_in, w_out, layer, gpost):
    T = x2.shape[0]

    def resident(shape):
        return pl.BlockSpec(shape, lambda i: (0, 0), pipeline_mode=pl.Buffered(1))

    return pl.pallas_call(
        _ffn_kernel,
        grid=(T // FFN_TM,),
        in_specs=[pl.BlockSpec((FFN_TM, D_MODEL), lambda i: (i, 0)),
                  resident((1, D_MODEL)),
                  _layer_weight(w_in, layer), _layer_weight(w_out, layer),
                  resident((1, D_MODEL))],
        out_specs=pl.BlockSpec((FFN_TM, D_MODEL), lambda i: (i, 0)),
        out_shape=jax.ShapeDtypeStruct((T, D_MODEL), F32),
        compiler_params=pltpu.CompilerParams(
            dimension_semantics=("parallel",), vmem_limit_bytes=VMEM_LIMIT),
        name="ffn",
    )(x2, gpre, w_in, w_out, gpost)


def kernel(x, mem, rel_bias, norm_mix_pre, w_in, b_gate, conv_dw, conv_dw_bias, conv_ln_g, conv_ln_b,
           w_conv_out, w_att_out, norm_mem, w_mem_kv, w_mem_out, w_out, norm_mix_post, norm_ffn_pre,
           w_ffn_in, w_ffn_out, norm_ffn_post):
    B, S, D = x.shape
    depth = w_in.shape[0]
    T = B * S
    biases = [_band_bias(rel_bias, g, d) for g, d in enumerate(DILATIONS)]

    def row(v):
        return v.reshape(1, -1)

    bf16 = {name: w.astype(BF16) for name, w in dict(
        w_in=w_in, w_conv_out=w_conv_out, w_att_out=w_att_out, w_mem_kv=w_mem_kv, w_mem_out=w_mem_out,
        w_out=w_out, w_ffn_in=w_ffn_in, w_ffn_out=w_ffn_out).items()}
    x2 = x.reshape(T, D)
    for l in range(depth):
        c, qm, qkv0, qkv1, qkv2, gates = _in_proj(
            x2, row(norm_mix_pre[l]), bf16["w_in"], l, row(b_gate[l]),
            conv_dw[l], row(conv_dw_bias[l]), row(conv_ln_g[l]), row(conv_ln_b[l]), B, S)
        o_att = _attention((qkv0, qkv1, qkv2), biases)
        kv = _mem_kv(mem, row(norm_mem[l]), bf16["w_mem_kv"], l)
        x2 = _mix(x2, c, qm, kv, o_att.reshape(T, ATT_OUT), gates,
                  bf16["w_conv_out"], bf16["w_att_out"], bf16["w_mem_out"], bf16["w_out"], l,
                  row(norm_mix_post[l]), S // MIX_TM)
        x2 = _ffn(x2, row(norm_ffn_pre[l]), bf16["w_ffn_in"], bf16["w_ffn_out"], l, row(norm_ffn_post[l]))
    return x2.reshape(B, S, D)
```

```python
import functools
import math

import numpy as np
import jax
import jax.numpy as jnp
from jax import lax
from jax.experimental import pallas as pl
from jax.experimental.pallas import tpu as pltpu

F32 = jnp.float32
BF16 = jnp.bfloat16

D_MODEL = 1024
CONV_WIDTH = 512
CONV_KSIZE = 31
CONV_PAD = CONV_KSIZE // 2
DILATIONS = (1, 4, 16)
RADIUS = 64
N_GROUPS = 3
HEADS = 4
HEAD_DIM = 64
ATT_OUT = HEADS * HEAD_DIM
QKV_WIDTH = 3 * ATT_OUT
MEM_HEADS = 4
MEM_HEAD_DIM = 128
MEM_WIDTH = 512
FFN_HIDDEN = 2816
NUM_BUCKETS = 32
MAX_DISTANCE = 1024
RMS_EPS = 1e-6
LN_EPS = 1e-5
NEG_INF = -1e30

GATE_WIDTH = 3 * D_MODEL
SUBQ = 128
SUBK = SUBQ + 2 * RADIUS
LANES = 128
LSE_LANES = 128
LSE_PER_HEAD = LSE_LANES // HEADS
STAT_LANES = LSE_PER_HEAD // 2
LOG2E = math.log2(math.e)

VMEM_LIMIT = 56 * 1024 * 1024


def _sigmoid(v):
    return 1.0 / (1.0 + jnp.exp(-v))


def _rms(v, g):
    return v * lax.rsqrt(jnp.mean(v * v, axis=-1, keepdims=True) + RMS_EPS) * g


def _layer_weight(w, layer):
    return pl.BlockSpec((1,) + w.shape[1:], lambda *_: (layer, 0, 0), pipeline_mode=pl.Buffered(1))


IN_TM = 512
IN_TN = 768
AG_WIDTH = 2 * CONV_WIDTH
QKV_COL = AG_WIDTH
QM_COL = QKV_COL + N_GROUPS * QKV_WIDTH
GATE_COL = QM_COL + MEM_WIDTH
N_GATE_TILES = GATE_WIDTH // IN_TN
assert IN_TN == QKV_WIDTH
SUBLANES = 8
CONV_HALO = 16
CONV_HALF = 256
CONV_RC = 64
CONV_FIRST = CONV_HALO - CONV_PAD
CONV_SHIFT_ROWS = CONV_HALF + (CONV_FIRST + CONV_KSIZE - 1) // SUBLANES * SUBLANES
IN_EXT = IN_TM + 2 * CONV_HALO


def _in_proj_kernel(x_ref, xp_ref, xn_ref, g_ref, w_ref, b_ref, cw_ref, cb_ref, lg_ref, lb_ref,
                    c_ref, qm_ref, q0_ref, q1_ref, q2_ref, gate_ref, acc_ref, u_ref, us_ref,
                    *, tiles_per_batch):
    ib = pl.program_id(0) % tiles_per_batch
    x_ext = jnp.concatenate([xp_ref[...], x_ref[...], xn_ref[...]], axis=0)
    h_ext = _rms(x_ext, g_ref[...]).astype(BF16)
    h = h_ext[CONV_HALO:CONV_HALO + IN_TM]

    ag = jnp.dot(h_ext, w_ref[0, :, 0:AG_WIDTH], preferred_element_type=F32)
    u = ag[:, 0:CONV_WIDTH] * _sigmoid(ag[:, CONV_WIDTH:])
    u_ref[0:CONV_HALO, :] = jnp.where(ib > 0, u[0:CONV_HALO], 0.0)
    u_ref[CONV_HALO:CONV_HALO + IN_TM, :] = u[CONV_HALO:CONV_HALO + IN_TM]
    u_ref[CONV_HALO + IN_TM:, :] = jnp.where(ib < tiles_per_batch - 1, u[CONV_HALO + IN_TM:], 0.0)
    def shift_copies(half):
        for s in range(SUBLANES):
            us_ref[s] = u_ref[half * CONV_HALF + s:half * CONV_HALF + s + CONV_SHIFT_ROWS, :]

    def conv_chunk(half, c):
        acc = jnp.zeros((CONV_RC, CONV_WIDTH), F32)
        for k in range(CONV_KSIZE):
            off = CONV_FIRST + k
            r0 = c * CONV_RC + off // SUBLANES * SUBLANES
            acc = acc + us_ref[off % SUBLANES, r0:r0 + CONV_RC, :] * cw_ref[k:k + 1, :]
        y = acc + cb_ref[...]
        mu = jnp.mean(y, axis=-1, keepdims=True)
        yc = y - mu
        yn = yc * lax.rsqrt(jnp.mean(yc * yc, axis=-1, keepdims=True) + LN_EPS)
        yn = yn * lg_ref[...] + lb_ref[...]
        rows = slice(half * CONV_HALF + c * CONV_RC, half * CONV_HALF + (c + 1) * CONV_RC)
        c_ref[rows, :] = (yn * _sigmoid(yn)).astype(BF16)

    def project(col, width):
        return jnp.dot(h, w_ref[0, :, col:col + width], preferred_element_type=F32)

    def mem_queries():
        qm_ref[...] = project(QM_COL, MEM_WIDTH).astype(BF16)

    q_scale = HEAD_DIM ** -0.5 * LOG2E

    def qkv_group(g):
        d, out_ref = DILATIONS[g], (q0_ref, q1_ref, q2_ref)[g]
        for which in range(3):
            acc = project(QKV_COL + (which * N_GROUPS + g) * ATT_OUT, ATT_OUT)
            if which == 0:
                acc = acc * q_scale
            cols = slice(which * ATT_OUT, (which + 1) * ATT_OUT)
            if d == 1:
                out_ref[0, 0, :, cols] = acc.astype(BF16)
                continue
            for half in range(ATT_OUT // LANES):
                acc_ref[g - 1, 2 * which + half] = acc[:, half * LANES:(half + 1) * LANES]
        if d > 1:
            for r in range(d):
                for cb in range(QKV_WIDTH // LANES):
                    out_ref[0, r, :, cb * LANES:(cb + 1) * LANES] = (
                        acc_ref[g - 1, cb, pl.ds(r, IN_TM // d, stride=d), :].astype(BF16))

    def gate_tile(t):
        cols = slice(t * IN_TN, (t + 1) * IN_TN)
        gate_ref[:, cols] = _sigmoid(project(GATE_COL + t * IN_TN, IN_TN) + b_ref[:, cols]).astype(BF16)

    mxu_units = ([mem_queries] + [functools.partial(qkv_group, g) for g in range(N_GROUPS)]
                 + [functools.partial(gate_tile, t) for t in range(N_GATE_TILES)])
    chunks = [(half, c) for half in range(IN_TM // CONV_HALF) for c in range(CONV_HALF // CONV_RC)]
    for step in range(max(len(mxu_units), len(chunks))):
        if step < len(chunks):
            half, c = chunks[step]
            if c == 0:
                shift_copies(half)
            conv_chunk(half, c)
        if step < len(mxu_units):
            mxu_units[step]()


def _in_proj(x2, g, w, layer, b, conv_w, conv_b, ln_g, ln_b, batch, seq):
    T = x2.shape[0]
    tiles_per_batch = seq // IN_TM
    n_strided = sum(d > 1 for d in DILATIONS)
    halo_per_tile = IN_TM // CONV_HALO
    last_halo = T // CONV_HALO - 1

    def qkv_spec(d):
        return pl.BlockSpec((1, d, IN_TM // d, QKV_WIDTH),
                            lambda i: (i // tiles_per_batch, 0, i % tiles_per_batch, 0))

    def resident(shape):
        return pl.BlockSpec(shape, lambda i: (0, 0), pipeline_mode=pl.Buffered(1))

    def rows(width):
        return pl.BlockSpec((IN_TM, width), lambda i: (i, 0))

    return pl.pallas_call(
        functools.partial(_in_proj_kernel, tiles_per_batch=tiles_per_batch),
        grid=(T // IN_TM,),
        in_specs=[
            rows(D_MODEL),
            pl.BlockSpec((CONV_HALO, D_MODEL), lambda i: (jnp.maximum(i * halo_per_tile - 1, 0), 0)),
            pl.BlockSpec((CONV_HALO, D_MODEL), lambda i: (jnp.minimum((i + 1) * halo_per_tile, last_halo), 0)),
            resident((1, D_MODEL)),
            _layer_weight(w, layer),
            resident((1, GATE_WIDTH)),
            resident((CONV_KSIZE, CONV_WIDTH)), resident((1, CONV_WIDTH)),
            resident((1, CONV_WIDTH)), resident((1, CONV_WIDTH)),
        ],
        out_specs=[rows(CONV_WIDTH), rows(MEM_WIDTH),
                   qkv_spec(DILATIONS[0]), qkv_spec(DILATIONS[1]), qkv_spec(DILATIONS[2]),
                   rows(GATE_WIDTH)],
        out_shape=[jax.ShapeDtypeStruct((T, CONV_WIDTH), BF16), jax.ShapeDtypeStruct((T, MEM_WIDTH), BF16)]
        + [jax.ShapeDtypeStruct((batch, d, seq // d, QKV_WIDTH), BF16) for d in DILATIONS]
        + [jax.ShapeDtypeStruct((T, GATE_WIDTH), BF16)],
        scratch_shapes=[pltpu.VMEM((n_strided, QKV_WIDTH // LANES, IN_TM, LANES), F32),
                        pltpu.VMEM((IN_EXT, CONV_WIDTH), F32),
                        pltpu.VMEM((SUBLANES, CONV_SHIFT_ROWS, CONV_WIDTH), F32)],
        compiler_params=pltpu.CompilerParams(
            dimension_semantics=("parallel",), vmem_limit_bytes=VMEM_LIMIT),
        name="in_proj",
    )(x2, x2, x2, g, w, b, conv_w, conv_b, ln_g, ln_b)


ATT_TILE = 2048
ATT_ILP = 16
ATT_COMBINE_ROWS = 256


def _att_kernel(c0_ref, p0_ref, n0_ref, c1_ref, p1_ref, n1_ref, c2_ref, p2_ref, n2_ref,
                b0_ref, b1_ref, b2_ref, o_ref, onat_ref, lnat_ref, edge_ref):
    i = pl.program_id(1)
    n = pl.num_programs(1)
    stat_of_lane = lax.broadcasted_iota(jnp.int32, (1, LSE_LANES), 1) // STAT_LANES
    key_col = lax.broadcasted_iota(jnp.int32, (1, SUBK), 1)
    K0, V0 = ATT_OUT, 2 * ATT_OUT
    even_head_lanes = lax.broadcasted_iota(jnp.int32, (1, LANES), 1) < HEAD_DIM

    before_start = jnp.where(key_col >= jnp.where(i == 0, RADIUS, 0), 0.0, NEG_INF)
    past_end = jnp.where(key_col < jnp.where(i == n - 1, SUBK - RADIUS, SUBK), 0.0, NEG_INF)
    for g, (d, b_ref) in enumerate(zip(DILATIONS, (b0_ref, b1_ref, b2_ref))):
        if ATT_TILE // d == SUBQ:
            edge_ref[2 * g] = b_ref[...] + before_start + past_end
        else:
            edge_ref[2 * g] = b_ref[...] + before_start
            edge_ref[2 * g + 1] = b_ref[...] + past_end

    def subtile(g, d, q, k, v, bias, row0):
        zeros = jnp.zeros((SUBQ, LANES), BF16)
        blocks = []
        for h in range(HEADS):
            tile = q[:, h // 2 * LANES:(h // 2 + 1) * LANES]
            tile = jnp.where(even_head_lanes, tile, zeros) if h % 2 == 0 else jnp.where(even_head_lanes, zeros, tile)
            blocks.append(jnp.concatenate([tile, zeros] if h < 2 else [zeros, tile], axis=1))
        qs = jnp.concatenate(blocks, axis=0)
        s = lax.dot_general(qs, k, (((1,), (1,)), ((), ())), preferred_element_type=F32) + bias
        m = jnp.max(s, axis=-1, keepdims=True)
        e = jnp.exp2(s - m)
        l = jnp.sum(e, axis=-1, keepdims=True)
        o_all = jnp.dot(e.astype(BF16), v, preferred_element_type=F32)
        o = jnp.concatenate(
            [jnp.where(even_head_lanes,
                       o_all[2 * t * SUBQ:(2 * t + 1) * SUBQ, t * LANES:(t + 1) * LANES],
                       o_all[(2 * t + 1) * SUBQ:(2 * t + 2) * SUBQ, t * LANES:(t + 1) * LANES])
             for t in range(ATT_OUT // LANES)], axis=1)
        stats = jnp.broadcast_to(l[(HEADS - 1) * SUBQ:], (SUBQ, LSE_LANES))
        stats = jnp.where(stat_of_lane == 2 * (HEADS - 1), m[(HEADS - 1) * SUBQ:], stats)
        for h in range(HEADS - 1):
            rows = slice(h * SUBQ, (h + 1) * SUBQ)
            stats = jnp.where(stat_of_lane == 2 * h, m[rows], stats)
            stats = jnp.where(stat_of_lane == 2 * h + 1, l[rows], stats)
        if d == 1:
            if not isinstance(row0, int):
                row0 = pl.multiple_of(row0, SUBQ)
            rows = pl.ds(row0, SUBQ)
        else:
            rows = pl.ds(row0, SUBQ, stride=d)
        for half in range(ATT_OUT // LANES):
            onat_ref[2 * g + half, rows, :] = o[:, half * LANES:(half + 1) * LANES]
        lnat_ref[g, rows, :] = stats

    groups = ((c0_ref, p0_ref, n0_ref, b0_ref), (c1_ref, p1_ref, n1_ref, b1_ref),
              (c2_ref, p2_ref, n2_ref, b2_ref))
    for g, (d, (c_ref, p_ref, n_ref, bias_ref)) in enumerate(zip(DILATIONS, groups)):
        tq = ATT_TILE // d
        nsub = tq // SUBQ

        def residue(r, carry, g=g, d=d, c_ref=c_ref, p_ref=p_ref, n_ref=n_ref, bias_ref=bias_ref,
                    tq=tq, nsub=nsub):
            def kv_cat(parts, col):
                return jnp.concatenate([ref[0, r, rows, col:col + ATT_OUT] for ref, rows in parts], axis=0)

            def sub(j):
                if isinstance(j, int) and (j == 0 or j == nsub - 1):
                    parts = [(c_ref, slice(max(j * SUBQ - RADIUS, 0), min(j * SUBQ + SUBQ + RADIUS, tq)))]
                    if j == 0:
                        parts = [(p_ref, slice(None))] + parts
                    if j == nsub - 1:
                        parts = parts + [(n_ref, slice(None))]
                    k, v = kv_cat(parts, K0), kv_cat(parts, V0)
                    q = c_ref[0, r, j * SUBQ:(j + 1) * SUBQ, 0:ATT_OUT]
                else:
                    q0 = j * SUBQ if isinstance(j, int) else pl.multiple_of(j * SUBQ, SUBQ)
                    k0 = j * SUBQ - RADIUS if isinstance(j, int) else pl.multiple_of(j * SUBQ - RADIUS, RADIUS)
                    q = c_ref[0, r, pl.ds(q0, SUBQ), 0:ATT_OUT]
                    k = c_ref[0, r, pl.ds(k0, SUBK), K0:K0 + ATT_OUT]
                    v = c_ref[0, r, pl.ds(k0, SUBK), V0:V0 + ATT_OUT]
                if isinstance(j, int) and j == 0:
                    bias = edge_ref[2 * g]
                elif isinstance(j, int) and j == nsub - 1:
                    bias = edge_ref[2 * g + 1]
                else:
                    bias = bias_ref[...]
                subtile(g, d, q, k, v, bias, j * SUBQ * d + r)

            if nsub <= 2 * ATT_ILP:
                for j in range(nsub):
                    sub(j)
            else:
                for j in range(ATT_ILP):
                    sub(j)

                def interior(t, carry2):
                    for u in range(ATT_ILP):
                        sub(ATT_ILP + t * ATT_ILP + u)
                    return carry2

                lax.fori_loop(0, (nsub - 2 * ATT_ILP) // ATT_ILP, interior, 0)
                for j in range(nsub - ATT_ILP, nsub):
                    sub(j)
            return carry

        if d == 1:
            residue(0, 0)
        else:
            lax.fori_loop(0, d, residue, 0, unroll=max(1, ATT_ILP // nsub))

    w_lane = lax.broadcasted_iota(jnp.int32, (2 * LSE_LANES, ATT_OUT), 0) % LSE_LANES
    o_lane = lax.broadcasted_iota(jnp.int32, (2 * LSE_LANES, ATT_OUT), 1)
    spread = (w_lane == o_lane // HEAD_DIM * LSE_PER_HEAD).astype(BF16)
    weight_lane = lax.broadcasted_iota(jnp.int32, (1, LSE_LANES), 1) % LSE_PER_HEAD == 0

    def expand(w):
        w = jnp.where(weight_lane, w, 0.0)
        hi = w.astype(BF16)
        lo = (w - hi.astype(F32)).astype(BF16)
        return jnp.dot(jnp.concatenate([hi, lo], axis=1), spread, preferred_element_type=F32)

    def combine(t, carry):
        rows = pl.ds(pl.multiple_of(t * ATT_COMBINE_ROWS, ATT_COMBINE_ROWS), ATT_COMBINE_ROWS)
        stats = [lnat_ref[g, rows, :] for g in range(N_GROUPS)]
        mx = jnp.maximum(jnp.maximum(stats[0], stats[1]), stats[2])
        a = [jnp.exp2(st - mx) for st in stats]
        sums = [pltpu.roll(st, LSE_LANES - STAT_LANES, 1) for st in stats]
        inv = 1.0 / (a[0] * sums[0] + a[1] * sums[1] + a[2] * sums[2])
        o = None
        for g in range(N_GROUPS):
            acc = jnp.concatenate([onat_ref[2 * g, rows, :], onat_ref[2 * g + 1, rows, :]], axis=1)
            term = expand(a[g] * inv) * acc
            o = term if o is None else o + term
        o_ref[0, rows, :] = o.astype(BF16)
        return carry

    lax.fori_loop(0, ATT_TILE // ATT_COMBINE_ROWS, combine, 0)


def _attention(qkv, biases):
    B = qkv[0].shape[0]
    S = qkv[0].shape[2]
    in_specs = []
    for d in DILATIONS:
        tq = ATT_TILE // d
        nb = tq // RADIUS
        last_b = S // d // RADIUS - 1
        in_specs += [
            pl.BlockSpec((1, d, tq, QKV_WIDTH), lambda b, i: (b, 0, i, 0)),
            pl.BlockSpec((1, d, RADIUS, QKV_WIDTH),
                         lambda b, i, nb=nb: (b, 0, jnp.maximum(i * nb - 1, 0), 0)),
            pl.BlockSpec((1, d, RADIUS, QKV_WIDTH),
                         lambda b, i, nb=nb, last_b=last_b: (b, 0, jnp.minimum((i + 1) * nb, last_b), 0)),
        ]
    in_specs += [pl.BlockSpec((HEADS * SUBQ, SUBK), lambda b, i: (0, 0))] * N_GROUPS
    args = []
    for a in qkv:
        args += [a, a, a]
    return pl.pallas_call(
        _att_kernel,
        grid=(B, S // ATT_TILE),
        in_specs=in_specs,
        out_specs=pl.BlockSpec((1, ATT_TILE, ATT_OUT), lambda b, i: (b, i, 0)),
        out_shape=jax.ShapeDtypeStruct((B, S, ATT_OUT), BF16),
        scratch_shapes=[pltpu.VMEM((N_GROUPS * (ATT_OUT // LANES), ATT_TILE, LANES), F32),
                        pltpu.VMEM((N_GROUPS, ATT_TILE, LSE_LANES), F32),
                        pltpu.VMEM((2 * N_GROUPS, HEADS * SUBQ, SUBK), F32)],
        compiler_params=pltpu.CompilerParams(
            dimension_semantics=("parallel", "parallel"), vmem_limit_bytes=VMEM_LIMIT),
        name="dilated_att",
    )(*args, *biases)


def _t5_bucket_np(rel):
    nb = NUM_BUCKETS // 2
    max_exact = nb // 2
    ret = np.where(rel > 0, nb, 0)
    n = np.abs(rel)
    nf = np.maximum(n, 1).astype(np.float32)
    ratio = np.log(nf / np.float32(max_exact)) / np.float32(math.log(MAX_DISTANCE / max_exact))
    large = max_exact + (ratio * np.float32(nb - max_exact)).astype(np.int32)
    large = np.minimum(large, nb - 1)
    return ret + np.where(n < max_exact, n, large)


def _band_bias(rel_bias, g, dilation):
    period = SUBQ + SUBK
    nband = 2 * RADIUS + 1
    bucket = _t5_bucket_np((np.arange(nband) - RADIUS) * dilation)
    onehot = np.zeros((period, NUM_BUCKETS), np.float32)
    onehot[np.arange(nband), bucket] = 1.0
    tab = rel_bias[:, g * HEADS:(g + 1) * HEADS].astype(F32)
    t = jnp.dot(jnp.asarray(onehot), tab, precision=lax.Precision.HIGHEST)
    t = jnp.where((np.arange(period) < nband)[:, None], t, NEG_INF).T
    skew = jnp.tile(t, (1, SUBQ))[:, :SUBQ * (period - 1)].reshape(HEADS, SUBQ, period - 1)
    return skew[:, :, :SUBK].reshape(HEADS * SUBQ, SUBK) * LOG2E


def _mem_kv_kernel(mem_ref, g_ref, w_ref, kv_ref):
    h = _rms(mem_ref[0], g_ref[...]).astype(BF16)
    kv_ref[0] = jnp.dot(h, w_ref[0], preferred_element_type=F32).astype(BF16)


def _mem_kv(mem, g, w, layer):
    B, M, _ = mem.shape
    return pl.pallas_call(
        _mem_kv_kernel,
        grid=(B,),
        in_specs=[pl.BlockSpec((1, M, D_MODEL), lambda b: (b, 0, 0)),
                  pl.BlockSpec((1, D_MODEL), lambda b: (0, 0)),
                  _layer_weight(w, layer)],
        out_specs=pl.BlockSpec((1, M, 2 * MEM_WIDTH), lambda b: (b, 0, 0)),
        out_shape=jax.ShapeDtypeStruct((B, M, 2 * MEM_WIDTH), BF16),
        compiler_params=pltpu.CompilerParams(
            dimension_semantics=("parallel",), vmem_limit_bytes=VMEM_LIMIT),
        name="mem_kv",
    )(mem, g, w)


MIX_TM = 512
BF16_ROWS = 16
FFN_OUT_CAST_ROWS = 176


def _mix_kernel(x_ref, c_ref, qm_ref, kv_ref, oatt_ref, gate_ref, wc_ref, wa_ref, wm_ref, wo_ref,
                gpost_ref, wfi_ref, wfo_ref, out_ref, wfi_out_ref, wfo_out_ref):
    wfi_out_ref[...] = wfi_ref[...].astype(BF16)

    @pl.when(pl.program_id(0) < FFN_HIDDEN // FFN_OUT_CAST_ROWS)
    def _():
        wfo_out_ref[...] = wfo_ref[...].astype(BF16)

    scale = MEM_HEAD_DIM ** -0.5
    heads = []
    for h in range(MEM_HEADS):
        lo = h * MEM_HEAD_DIM
        qh = qm_ref[:, lo:lo + MEM_HEAD_DIM]
        kh = kv_ref[0, :, lo:lo + MEM_HEAD_DIM]
        vh = kv_ref[0, :, MEM_WIDTH + lo:MEM_WIDTH + lo + MEM_HEAD_DIM]
        s = lax.dot_general(qh, kh, (((1,), (1,)), ((), ())), preferred_element_type=F32) * scale
        m = jnp.max(s, axis=-1, keepdims=True)
        e = jnp.exp(s - m)
        p = (e * (1.0 / jnp.sum(e, axis=-1, keepdims=True))).astype(BF16)
        heads.append(jnp.dot(p, vh, preferred_element_type=F32).astype(BF16))
    o_mem = jnp.concatenate(heads, axis=-1)
    y_mem = jnp.dot(o_mem, wm_ref[0], preferred_element_type=F32)
    y_conv = jnp.dot(c_ref[...], wc_ref[0], preferred_element_type=F32)
    y_att = jnp.dot(oatt_ref[...], wa_ref[0], preferred_element_type=F32)
    merged = (gate_ref[:, 0:D_MODEL].astype(F32) * y_conv
              + gate_ref[:, D_MODEL:2 * D_MODEL].astype(F32) * y_att
              + gate_ref[:, 2 * D_MODEL:].astype(F32) * y_mem)
    y = jnp.dot(merged.astype(BF16), wo_ref[0], preferred_element_type=F32)
    out_ref[...] = x_ref[...] + _rms(y, gpost_ref[...])


def _mix(x2, c2, qm, kv, o_att, gates, wc, wa, wm, wo, layer, gpost, w_ffn_in, w_ffn_out, tiles_per_batch):
    T = x2.shape[0]
    M = kv.shape[1]
    steps = T // MIX_TM
    in_rows = D_MODEL // steps
    out_steps = FFN_HIDDEN // FFN_OUT_CAST_ROWS
    assert in_rows % BF16_ROWS == 0 and FFN_OUT_CAST_ROWS % BF16_ROWS == 0 and out_steps <= steps

    def w_slice(rows_per_step, width, lead, nsteps):
        return pl.BlockSpec((1, rows_per_step, width), lambda i: (lead, jnp.minimum(i, nsteps - 1), 0))

    def rows(width):
        return pl.BlockSpec((MIX_TM, width), lambda i: (i, 0))

    def whole(shape):
        return pl.BlockSpec(shape, lambda i: (0,) * len(shape))

    return pl.pallas_call(
        _mix_kernel,
        grid=(T // MIX_TM,),
        in_specs=[rows(D_MODEL), rows(CONV_WIDTH), rows(MEM_WIDTH),
                  pl.BlockSpec((1, M, 2 * MEM_WIDTH), lambda i: (i // tiles_per_batch, 0, 0)),
                  rows(ATT_OUT), rows(GATE_WIDTH),
                  _layer_weight(wc, layer), _layer_weight(wa, layer),
                  _layer_weight(wm, layer), _layer_weight(wo, layer), whole((1, D_MODEL)),
                  w_slice(in_rows, 2 * FFN_HIDDEN, layer, steps),
                  w_slice(FFN_OUT_CAST_ROWS, D_MODEL, layer, out_steps)],
        out_specs=[rows(D_MODEL),
                   w_slice(in_rows, 2 * FFN_HIDDEN, 0, steps),
                   w_slice(FFN_OUT_CAST_ROWS, D_MODEL, 0, out_steps)],
        out_shape=[jax.ShapeDtypeStruct((T, D_MODEL), F32),
                   jax.ShapeDtypeStruct((1, D_MODEL, 2 * FFN_HIDDEN), BF16),
                   jax.ShapeDtypeStruct((1, FFN_HIDDEN, D_MODEL), BF16)],
        compiler_params=pltpu.CompilerParams(
            dimension_semantics=("arbitrary",), vmem_limit_bytes=VMEM_LIMIT),
        name="mix_out",
    )(x2, c2, qm, kv, o_att, gates, wc, wa, wm, wo, gpost, w_ffn_in, w_ffn_out)


FFN_TM = 512
FFN_TH = 256
FFN_CHUNKS = FFN_HIDDEN // FFN_TH


def _ffn_kernel(x_ref, gpre_ref, win_ref, wout_ref, gpost_ref, out_ref):
    x = x_ref[...]
    h = _rms(x, gpre_ref[...]).astype(BF16)
    acc = None
    for c in range(FFN_CHUNKS):
        lo = c * FFN_TH
        gv = jnp.dot(h, win_ref[0, :, lo:lo + FFN_TH], preferred_element_type=F32)
        uv = jnp.dot(h, win_ref[0, :, FFN_HIDDEN + lo:FFN_HIDDEN + lo + FFN_TH], preferred_element_type=F32)
        a = (gv * _sigmoid(gv) * uv).astype(BF16)
        part = jnp.dot(a, wout_ref[0, lo:lo + FFN_TH, :], preferred_element_type=F32)
        acc = part if acc is None else acc + part
    out_ref[...] = x + _rms(acc, gpost_ref[...])


def _ffn(x2, gpre, w_in, w_out, layer, gpost):
    T = x2.shape[0]

    def resident(shape):
        return pl.BlockSpec(shape, lambda i: (0, 0), pipeline_mode=pl.Buffered(1))

    return pl.pallas_call(
        _ffn_kernel,
        grid=(T // FFN_TM,),
        in_specs=[pl.BlockSpec((FFN_TM, D_MODEL), lambda i: (i, 0)),
                  resident((1, D_MODEL)),
                  _layer_weight(w_in, layer), _layer_weight(w_out, layer),
                  resident((1, D_MODEL))],
        out_specs=pl.BlockSpec((FFN_TM, D_MODEL), lambda i: (i, 0)),
        out_shape=jax.ShapeDtypeStruct((T, D_MODEL), F32),
        compiler_params=pltpu.CompilerParams(
            dimension_semantics=("parallel",), vmem_limit_bytes=VMEM_LIMIT),
        name="ffn",
    )(x2, gpre, w_in, w_out, gpost)


def kernel(x, mem, rel_bias, norm_mix_pre, w_in, b_gate, conv_dw, conv_dw_bias, conv_ln_g, conv_ln_b,
           w_conv_out, w_att_out, norm_mem, w_mem_kv, w_mem_out, w_out, norm_mix_post, norm_ffn_pre,
           w_ffn_in, w_ffn_out, norm_ffn_post):
    B, S, D = x.shape
    depth = w_in.shape[0]
    T = B * S
    biases = [_band_bias(rel_bias, g, d) for g, d in enumerate(DILATIONS)]

    def row(v):
        return v.reshape(1, -1)

    bf16 = {name: w.astype(BF16) for name, w in dict(
        w_in=w_in, w_conv_out=w_conv_out, w_att_out=w_att_out, w_mem_kv=w_mem_kv, w_mem_out=w_mem_out,
        w_out=w_out).items()}
    x2 = x.reshape(T, D)
    for l in range(depth):
        c, qm, qkv0, qkv1, qkv2, gates = _in_proj(
            x2, row(norm_mix_pre[l]), bf16["w_in"], l, row(b_gate[l]),
            conv_dw[l], row(conv_dw_bias[l]), row(conv_ln_g[l]), row(conv_ln_b[l]), B, S)
        o_att = _attention((qkv0, qkv1, qkv2), biases)
        kv = _mem_kv(mem, row(norm_mem[l]), bf16["w_mem_kv"], l)
        x2, wfi, wfo = _mix(x2, c, qm, kv, o_att.reshape(T, ATT_OUT), gates,
                            bf16["w_conv_out"], bf16["w_att_out"], bf16["w_mem_out"], bf16["w_out"], l,
                            row(norm_mix_post[l]), w_ffn_in, w_ffn_out, S // MIX_TM)
        x2 = _ffn(x2, row(norm_ffn_pre[l]), wfi, wfo, 0, row(norm_ffn_post[l]))
    return x2.reshape(B, S, D)
```

```python
import functools
import math

import numpy as np
import jax
import jax.numpy as jnp
from jax import lax
from jax.experimental import pallas as pl
from jax.experimental.pallas import tpu as pltpu

F32 = jnp.float32
BF16 = jnp.bfloat16

D_MODEL = 1024
CONV_WIDTH = 512
CONV_KSIZE = 31
CONV_PAD = CONV_KSIZE // 2
DILATIONS = (1, 4, 16)
RADIUS = 64
N_GROUPS = 3
HEADS = 4
HEAD_DIM = 64
ATT_OUT = HEADS * HEAD_DIM
QKV_WIDTH = 3 * ATT_OUT
MEM_HEADS = 4
MEM_HEAD_DIM = 128
MEM_WIDTH = 512
FFN_HIDDEN = 2816
NUM_BUCKETS = 32
MAX_DISTANCE = 1024
RMS_EPS = 1e-6
LN_EPS = 1e-5
NEG_INF = -1e30

GATE_WIDTH = 3 * D_MODEL
SUBQ = 128
SUBK = SUBQ + 2 * RADIUS
LANES = 128
LSE_LANES = 128
LSE_PER_HEAD = LSE_LANES // HEADS
STAT_LANES = LSE_PER_HEAD // 2
LOG2E = math.log2(math.e)

VMEM_LIMIT = 56 * 1024 * 1024


def _sigmoid(v):
    return 1.0 / (1.0 + jnp.exp(-v))


def _rms(v, g):
    return v * lax.rsqrt(jnp.mean(v * v, axis=-1, keepdims=True) + RMS_EPS) * g


def _derived_zero(v):
    bits = lax.bitcast_convert_type(v, jnp.uint32)
    bits = lax.shift_right_logical(lax.shift_right_logical(bits, jnp.uint32(16)), jnp.uint32(16))
    return lax.bitcast_convert_type(bits, F32)


def _layer_weight(w, layer):
    return pl.BlockSpec((1,) + w.shape[1:], lambda *_: (layer, 0, 0), pipeline_mode=pl.Buffered(1))


IN_TM = 512
IN_TN = 768
AG_WIDTH = 2 * CONV_WIDTH
QKV_COL = AG_WIDTH
QM_COL = QKV_COL + N_GROUPS * QKV_WIDTH
GATE_COL = QM_COL + MEM_WIDTH
N_GATE_TILES = GATE_WIDTH // IN_TN
assert IN_TN == QKV_WIDTH
DOT_TN = 256
SUBLANES = 8
CONV_HALO = 16
CONV_HALF = 256
CONV_RC = 16
CONV_FIRST = CONV_HALO - CONV_PAD
CONV_SHIFT_ROWS = CONV_HALF + (CONV_FIRST + CONV_KSIZE - 1) // SUBLANES * SUBLANES
IN_EXT = IN_TM + 2 * CONV_HALO


def _in_proj_kernel(x_ref, xp_ref, xn_ref, g_ref, w_ref, b_ref, cw_ref, cb_ref, lg_ref, lb_ref,
                    c_ref, qm_ref, q0_ref, q1_ref, q2_ref, gate_ref, acc_ref, u_ref, us_ref,
                    *, tiles_per_batch):
    ib = pl.program_id(0) % tiles_per_batch
    x_ext = jnp.concatenate([xp_ref[...], x_ref[...], xn_ref[...]], axis=0)
    h_ext = _rms(x_ext, g_ref[...]).astype(BF16)
    h = h_ext[CONV_HALO:CONV_HALO + IN_TM]

    for t in range(CONV_WIDTH // DOT_TN):
        cols = slice(t * DOT_TN, (t + 1) * DOT_TN)
        a = jnp.dot(h_ext, w_ref[0, :, cols], preferred_element_type=F32)
        gt = jnp.dot(h_ext, w_ref[0, :, CONV_WIDTH + t * DOT_TN:CONV_WIDTH + (t + 1) * DOT_TN],
                     preferred_element_type=F32)
        u = a * _sigmoid(gt)
        u_ref[0:CONV_HALO, cols] = jnp.where(ib > 0, u[0:CONV_HALO], 0.0)
        u_ref[CONV_HALO:CONV_HALO + IN_TM, cols] = u[CONV_HALO:CONV_HALO + IN_TM]
        u_ref[CONV_HALO + IN_TM:, cols] = jnp.where(ib < tiles_per_batch - 1, u[CONV_HALO + IN_TM:], 0.0)

    def shift_copies(half):
        for s in range(SUBLANES):
            us_ref[s] = u_ref[half * CONV_HALF + s:half * CONV_HALF + s + CONV_SHIFT_ROWS, :]

    def conv_chunk(half, c, zero):
        acc = jnp.concatenate([zero] * (CONV_RC // SUBLANES), axis=0)
        for k in range(CONV_KSIZE):
            off = CONV_FIRST + k
            r0 = c * CONV_RC + off // SUBLANES * SUBLANES
            acc = acc + us_ref[off % SUBLANES, r0:r0 + CONV_RC, :] * cw_ref[k:k + 1, :]
        y = acc + cb_ref[...]
        mu = jnp.mean(y, axis=-1, keepdims=True)
        yc = y - mu
        yn = yc * lax.rsqrt(jnp.mean(yc * yc, axis=-1, keepdims=True) + LN_EPS)
        yn = yn * lg_ref[...] + lb_ref[...]
        act = yn * _sigmoid(yn)
        rows = slice(half * CONV_HALF + c * CONV_RC, half * CONV_HALF + (c + 1) * CONV_RC)
        c_ref[rows, :] = act.astype(BF16)
        return _derived_zero(y[CONV_RC - SUBLANES:])

    def project(col, width):
        return jnp.dot(h, w_ref[0, :, col:col + width], preferred_element_type=F32)

    def mem_queries():
        for t in range(MEM_WIDTH // DOT_TN):
            acc = project(QM_COL + t * DOT_TN, DOT_TN)
            qm_ref[:, t * DOT_TN:(t + 1) * DOT_TN] = acc.astype(BF16)
        return acc

    q_scale = HEAD_DIM ** -0.5 * LOG2E

    def qkv_group(g):
        d, out_ref = DILATIONS[g], (q0_ref, q1_ref, q2_ref)[g]
        for which in range(3):
            acc = project(QKV_COL + (which * N_GROUPS + g) * ATT_OUT, ATT_OUT)
            if which == 0:
                acc = acc * q_scale
            cols = slice(which * ATT_OUT, (which + 1) * ATT_OUT)
            if d == 1:
                out_ref[0, 0, :, cols] = acc.astype(BF16)
                continue
            for half in range(ATT_OUT // LANES):
                acc_ref[2 * which + half] = acc[:, half * LANES:(half + 1) * LANES]
        if d > 1:
            for r in range(d):
                for cb in range(QKV_WIDTH // LANES):
                    out_ref[0, r, :, cb * LANES:(cb + 1) * LANES] = (
                        acc_ref[cb, pl.ds(r, IN_TM // d, stride=d), :].astype(BF16))
        return acc

    def gate_tile(t):
        for sub in range(IN_TN // DOT_TN):
            cols = slice(t * IN_TN + sub * DOT_TN, t * IN_TN + (sub + 1) * DOT_TN)
            acc = project(GATE_COL + cols.start, DOT_TN)
            gate_ref[:, cols] = _sigmoid(acc + b_ref[:, cols]).astype(BF16)
        return acc

    mxu_units = ([mem_queries] + [functools.partial(qkv_group, g) for g in range(N_GROUPS)]
                 + [functools.partial(gate_tile, t) for t in range(N_GATE_TILES)])
    chunks = [(half, c) for half in range(IN_TM // CONV_HALF) for c in range(CONV_HALF // CONV_RC)]
    chunks_per_unit = len(chunks) // len(mxu_units)
    assert chunks_per_unit * len(mxu_units) == len(chunks)
    zero = jnp.zeros((SUBLANES, CONV_WIDTH), F32)
    for step, unit in enumerate(mxu_units):
        for half, c in chunks[step * chunks_per_unit:(step + 1) * chunks_per_unit]:
            if c == 0:
                shift_copies(half)
            zero = conv_chunk(half, c, zero)
        acc = unit()
        tail = _derived_zero(acc[IN_TM - SUBLANES:, acc.shape[1] - LANES:])
        zero = zero + jnp.concatenate([tail] * (CONV_WIDTH // LANES), axis=1)


def _in_proj(x2, g, w, layer, b, conv_w, conv_b, ln_g, ln_b, batch, seq):
    T = x2.shape[0]
    tiles_per_batch = seq // IN_TM
    halo_per_tile = IN_TM // CONV_HALO
    last_halo = T // CONV_HALO - 1

    def qkv_spec(d):
        return pl.BlockSpec((1, d, IN_TM // d, QKV_WIDTH),
                            lambda i: (i // tiles_per_batch, 0, i % tiles_per_batch, 0))

    def resident(shape):
        return pl.BlockSpec(shape, lambda i: (0, 0), pipeline_mode=pl.Buffered(1))

    def rows(width):
        return pl.BlockSpec((IN_TM, width), lambda i: (i, 0))

    return pl.pallas_call(
        functools.partial(_in_proj_kernel, tiles_per_batch=tiles_per_batch),
        grid=(T // IN_TM,),
        in_specs=[
            rows(D_MODEL),
            pl.BlockSpec((CONV_HALO, D_MODEL), lambda i: (jnp.maximum(i * halo_per_tile - 1, 0), 0)),
            pl.BlockSpec((CONV_HALO, D_MODEL), lambda i: (jnp.minimum((i + 1) * halo_per_tile, last_halo), 0)),
            resident((1, D_MODEL)),
            _layer_weight(w, layer),
            resident((1, GATE_WIDTH)),
            resident((CONV_KSIZE, CONV_WIDTH)), resident((1, CONV_WIDTH)),
            resident((1, CONV_WIDTH)), resident((1, CONV_WIDTH)),
        ],
        out_specs=[rows(CONV_WIDTH), rows(MEM_WIDTH),
                   qkv_spec(DILATIONS[0]), qkv_spec(DILATIONS[1]), qkv_spec(DILATIONS[2]),
                   rows(GATE_WIDTH)],
        out_shape=[jax.ShapeDtypeStruct((T, CONV_WIDTH), BF16), jax.ShapeDtypeStruct((T, MEM_WIDTH), BF16)]
        + [jax.ShapeDtypeStruct((batch, d, seq // d, QKV_WIDTH), BF16) for d in DILATIONS]
        + [jax.ShapeDtypeStruct((T, GATE_WIDTH), BF16)],
        scratch_shapes=[pltpu.VMEM((QKV_WIDTH // LANES, IN_TM, LANES), F32),
                        pltpu.VMEM((IN_EXT, CONV_WIDTH), F32),
                        pltpu.VMEM((SUBLANES, CONV_SHIFT_ROWS, CONV_WIDTH), F32)],
        compiler_params=pltpu.CompilerParams(
            dimension_semantics=("parallel",), vmem_limit_bytes=VMEM_LIMIT),
        name="in_proj",
    )(x2, x2, x2, g, w, b, conv_w, conv_b, ln_g, ln_b)


ATT_TILE = 2048
ATT_ILP = 16
ATT_COMBINE_ROWS = 256


def _att_kernel(c0_ref, p0_ref, n0_ref, c1_ref, p1_ref, n1_ref, c2_ref, p2_ref, n2_ref,
                b0_ref, b1_ref, b2_ref, o_ref, onat_ref, lnat_ref, edge_ref):
    i = pl.program_id(1)
    n = pl.num_programs(1)
    stat_of_lane = lax.broadcasted_iota(jnp.int32, (1, LSE_LANES), 1) // STAT_LANES
    key_col = lax.broadcasted_iota(jnp.int32, (1, SUBK), 1)
    K0, V0 = ATT_OUT, 2 * ATT_OUT
    even_head_lanes = lax.broadcasted_iota(jnp.int32, (1, LANES), 1) < HEAD_DIM

    before_start = jnp.where(key_col >= jnp.where(i == 0, RADIUS, 0), 0.0, NEG_INF)
    past_end = jnp.where(key_col < jnp.where(i == n - 1, SUBK - RADIUS, SUBK), 0.0, NEG_INF)
    for g, (d, b_ref) in enumerate(zip(DILATIONS, (b0_ref, b1_ref, b2_ref))):
        if ATT_TILE // d == SUBQ:
            edge_ref[2 * g] = b_ref[...] + before_start + past_end
        else:
            edge_ref[2 * g] = b_ref[...] + before_start
            edge_ref[2 * g + 1] = b_ref[...] + past_end

    def subtile(g, d, q, k, v, bias, row0):
        zeros = jnp.zeros((SUBQ, LANES), BF16)
        blocks = []
        for h in range(HEADS):
            tile = q[:, h // 2 * LANES:(h // 2 + 1) * LANES]
            tile = jnp.where(even_head_lanes, tile, zeros) if h % 2 == 0 else jnp.where(even_head_lanes, zeros, tile)
            blocks.append(jnp.concatenate([tile, zeros] if h < 2 else [zeros, tile], axis=1))
        qs = jnp.concatenate(blocks, axis=0)
        s = lax.dot_general(qs, k, (((1,), (1,)), ((), ())), preferred_element_type=F32) + bias
        m = jnp.max(s, axis=-1, keepdims=True)
        e = jnp.exp2(s - m)
        l = jnp.sum(e, axis=-1, keepdims=True)
        o_all = jnp.dot(e.astype(BF16), v, preferred_element_type=F32)
        o = jnp.concatenate(
            [jnp.where(even_head_lanes,
                       o_all[2 * t * SUBQ:(2 * t + 1) * SUBQ, t * LANES:(t + 1) * LANES],
                       o_all[(2 * t + 1) * SUBQ:(2 * t + 2) * SUBQ, t * LANES:(t + 1) * LANES])
             for t in range(ATT_OUT // LANES)], axis=1)
        stats = jnp.broadcast_to(l[(HEADS - 1) * SUBQ:], (SUBQ, LSE_LANES))
        stats = jnp.where(stat_of_lane == 2 * (HEADS - 1), m[(HEADS - 1) * SUBQ:], stats)
        for h in range(HEADS - 1):
            rows = slice(h * SUBQ, (h + 1) * SUBQ)
            stats = jnp.where(stat_of_lane == 2 * h, m[rows], stats)
            stats = jnp.where(stat_of_lane == 2 * h + 1, l[rows], stats)
        if d == 1:
            if not isinstance(row0, int):
                row0 = pl.multiple_of(row0, SUBQ)
            rows = pl.ds(row0, SUBQ)
        else:
            rows = pl.ds(row0, SUBQ, stride=d)
        for half in range(ATT_OUT // LANES):
            onat_ref[2 * g + half, rows, :] = o[:, half * LANES:(half + 1) * LANES]
        lnat_ref[g, rows, :] = stats

    groups = ((c0_ref, p0_ref, n0_ref, b0_ref), (c1_ref, p1_ref, n1_ref, b1_ref),
              (c2_ref, p2_ref, n2_ref, b2_ref))
    for g, (d, (c_ref, p_ref, n_ref, bias_ref)) in enumerate(zip(DILATIONS, groups)):
        tq = ATT_TILE // d
        nsub = tq // SUBQ

        def residue(r, carry, g=g, d=d, c_ref=c_ref, p_ref=p_ref, n_ref=n_ref, bias_ref=bias_ref,
                    tq=tq, nsub=nsub):
            def kv_cat(parts, col):
                return jnp.concatenate([ref[0, r, rows, col:col + ATT_OUT] for ref, rows in parts], axis=0)

            def sub(j):
                if isinstance(j, int) and (j == 0 or j == nsub - 1):
                    parts = [(c_ref, slice(max(j * SUBQ - RADIUS, 0), min(j * SUBQ + SUBQ + RADIUS, tq)))]
                    if j == 0:
                        parts = [(p_ref, slice(None))] + parts
                    if j == nsub - 1:
                        parts = parts + [(n_ref, slice(None))]
                    k, v = kv_cat(parts, K0), kv_cat(parts, V0)
                    q = c_ref[0, r, j * SUBQ:(j + 1) * SUBQ, 0:ATT_OUT]
                else:
                    q0 = j * SUBQ if isinstance(j, int) else pl.multiple_of(j * SUBQ, SUBQ)
                    k0 = j * SUBQ - RADIUS if isinstance(j, int) else pl.multiple_of(j * SUBQ - RADIUS, RADIUS)
                    q = c_ref[0, r, pl.ds(q0, SUBQ), 0:ATT_OUT]
                    k = c_ref[0, r, pl.ds(k0, SUBK), K0:K0 + ATT_OUT]
                    v = c_ref[0, r, pl.ds(k0, SUBK), V0:V0 + ATT_OUT]
                if isinstance(j, int) and j == 0:
                    bias = edge_ref[2 * g]
                elif isinstance(j, int) and j == nsub - 1:
                    bias = edge_ref[2 * g + 1]
                else:
                    bias = bias_ref[...]
                subtile(g, d, q, k, v, bias, j * SUBQ * d + r)

            if nsub <= 2 * ATT_ILP:
                for j in range(nsub):
                    sub(j)
            else:
                for j in range(ATT_ILP):
                    sub(j)

                def interior(t, carry2):
                    for u in range(ATT_ILP):
                        sub(ATT_ILP + t * ATT_ILP + u)
                    return carry2

                lax.fori_loop(0, (nsub - 2 * ATT_ILP) // ATT_ILP, interior, 0)
                for j in range(nsub - ATT_ILP, nsub):
                    sub(j)
            return carry

        if d == 1:
            residue(0, 0)
        else:
            lax.fori_loop(0, d, residue, 0, unroll=max(1, ATT_ILP // nsub))

    w_lane = lax.broadcasted_iota(jnp.int32, (2 * LSE_LANES, ATT_OUT), 0) % LSE_LANES
    o_lane = lax.broadcasted_iota(jnp.int32, (2 * LSE_LANES, ATT_OUT), 1)
    spread = (w_lane == o_lane // HEAD_DIM * LSE_PER_HEAD).astype(BF16)
    weight_lane = lax.broadcasted_iota(jnp.int32, (1, LSE_LANES), 1) % LSE_PER_HEAD == 0

    def expand(w):
        w = jnp.where(weight_lane, w, 0.0)
        hi = w.astype(BF16)
        lo = (w - hi.astype(F32)).astype(BF16)
        return jnp.dot(jnp.concatenate([hi, lo], axis=1), spread, preferred_element_type=F32)

    def combine(t, carry):
        rows = pl.ds(pl.multiple_of(t * ATT_COMBINE_ROWS, ATT_COMBINE_ROWS), ATT_COMBINE_ROWS)
        stats = [lnat_ref[g, rows, :] for g in range(N_GROUPS)]
        mx = jnp.maximum(jnp.maximum(stats[0], stats[1]), stats[2])
        a = [jnp.exp2(st - mx) for st in stats]
        sums = [pltpu.roll(st, LSE_LANES - STAT_LANES, 1) for st in stats]
        inv = 1.0 / (a[0] * sums[0] + a[1] * sums[1] + a[2] * sums[2])
        o = None
        for g in range(N_GROUPS):
            acc = jnp.concatenate([onat_ref[2 * g, rows, :], onat_ref[2 * g + 1, rows, :]], axis=1)
            term = expand(a[g] * inv) * acc
            o = term if o is None else o + term
        o_ref[0, rows, :] = o.astype(BF16)
        return carry

    lax.fori_loop(0, ATT_TILE // ATT_COMBINE_ROWS, combine, 0)


def _attention(qkv, biases):
    B = qkv[0].shape[0]
    S = qkv[0].shape[2]
    in_specs = []
    for d in DILATIONS:
        tq = ATT_TILE // d
        nb = tq // RADIUS
        last_b = S // d // RADIUS - 1
        in_specs += [
            pl.BlockSpec((1, d, tq, QKV_WIDTH), lambda b, i: (b, 0, i, 0)),
            pl.BlockSpec((1, d, RADIUS, QKV_WIDTH),
                         lambda b, i, nb=nb: (b, 0, jnp.maximum(i * nb - 1, 0), 0)),
            pl.BlockSpec((1, d, RADIUS, QKV_WIDTH),
                         lambda b, i, nb=nb, last_b=last_b: (b, 0, jnp.minimum((i + 1) * nb, last_b), 0)),
        ]
    in_specs += [pl.BlockSpec((HEADS * SUBQ, SUBK), lambda b, i: (0, 0))] * N_GROUPS
    args = []
    for a in qkv:
        args += [a, a, a]
    return pl.pallas_call(
        _att_kernel,
        grid=(B, S // ATT_TILE),
        in_specs=in_specs,
        out_specs=pl.BlockSpec((1, ATT_TILE, ATT_OUT), lambda b, i: (b, i, 0)),
        out_shape=jax.ShapeDtypeStruct((B, S, ATT_OUT), BF16),
        scratch_shapes=[pltpu.VMEM((N_GROUPS * (ATT_OUT // LANES), ATT_TILE, LANES), F32),
                        pltpu.VMEM((N_GROUPS, ATT_TILE, LSE_LANES), F32),
                        pltpu.VMEM((2 * N_GROUPS, HEADS * SUBQ, SUBK), F32)],
        compiler_params=pltpu.CompilerParams(
            dimension_semantics=("parallel", "parallel"), vmem_limit_bytes=VMEM_LIMIT),
        name="dilated_att",
    )(*args, *biases)


def _t5_bucket_np(rel):
    nb = NUM_BUCKETS // 2
    max_exact = nb // 2
    ret = np.where(rel > 0, nb, 0)
    n = np.abs(rel)
    nf = np.maximum(n, 1).astype(np.float32)
    ratio = np.log(nf / np.float32(max_exact)) / np.float32(math.log(MAX_DISTANCE / max_exact))
    large = max_exact + (ratio * np.float32(nb - max_exact)).astype(np.int32)
    large = np.minimum(large, nb - 1)
    return ret + np.where(n < max_exact, n, large)


def _band_bias(rel_bias, g, dilation):
    period = SUBQ + SUBK
    nband = 2 * RADIUS + 1
    bucket = _t5_bucket_np((np.arange(nband) - RADIUS) * dilation)
    onehot = np.zeros((period, NUM_BUCKETS), np.float32)
    onehot[np.arange(nband), bucket] = 1.0
    tab = rel_bias[:, g * HEADS:(g + 1) * HEADS].astype(F32)
    t = jnp.dot(jnp.asarray(onehot), tab, precision=lax.Precision.HIGHEST)
    t = jnp.where((np.arange(period) < nband)[:, None], t, NEG_INF).T
    skew = jnp.tile(t, (1, SUBQ))[:, :SUBQ * (period - 1)].reshape(HEADS, SUBQ, period - 1)
    return skew[:, :, :SUBK].reshape(HEADS * SUBQ, SUBK) * LOG2E


def _mem_kv_kernel(mem_ref, g_ref, w_ref, kv_ref):
    h = _rms(mem_ref[0], g_ref[...]).astype(BF16)
    kv_ref[0] = jnp.dot(h, w_ref[0], preferred_element_type=F32).astype(BF16)


def _mem_kv(mem, g, w, layer):
    B, M, _ = mem.shape
    return pl.pallas_call(
        _mem_kv_kernel,
        grid=(B,),
        in_specs=[pl.BlockSpec((1, M, D_MODEL), lambda b: (b, 0, 0)),
                  pl.BlockSpec((1, D_MODEL), lambda b: (0, 0)),
                  _layer_weight(w, layer)],
        out_specs=pl.BlockSpec((1, M, 2 * MEM_WIDTH), lambda b: (b, 0, 0)),
        out_shape=jax.ShapeDtypeStruct((B, M, 2 * MEM_WIDTH), BF16),
        compiler_params=pltpu.CompilerParams(
            dimension_semantics=("parallel",), vmem_limit_bytes=VMEM_LIMIT),
        name="mem_kv",
    )(mem, g, w)


MIX_TM = 512
BF16_ROWS = 16
FFN_OUT_CAST_ROWS = 176


def _mix_kernel(x_ref, c_ref, qm_ref, kv_ref, oatt_ref, gate_ref, wc_ref, wa_ref, wm_ref, wo_ref,
                gpost_ref, wfi_ref, wfo_ref, out_ref, wfi_out_ref, wfo_out_ref):
    wfi_out_ref[...] = wfi_ref[...].astype(BF16)

    @pl.when(pl.program_id(0) < FFN_HIDDEN // FFN_OUT_CAST_ROWS)
    def _():
        wfo_out_ref[...] = wfo_ref[...].astype(BF16)

    scale = MEM_HEAD_DIM ** -0.5
    heads = []
    for h in range(MEM_HEADS):
        lo = h * MEM_HEAD_DIM
        qh = qm_ref[:, lo:lo + MEM_HEAD_DIM]
        kh = kv_ref[0, :, lo:lo + MEM_HEAD_DIM]
        vh = kv_ref[0, :, MEM_WIDTH + lo:MEM_WIDTH + lo + MEM_HEAD_DIM]
        s = lax.dot_general(qh, kh, (((1,), (1,)), ((), ())), preferred_element_type=F32) * scale
        m = jnp.max(s, axis=-1, keepdims=True)
        e = jnp.exp(s - m)
        p = (e * (1.0 / jnp.sum(e, axis=-1, keepdims=True))).astype(BF16)
        heads.append(jnp.dot(p, vh, preferred_element_type=F32).astype(BF16))
    o_mem = jnp.concatenate(heads, axis=-1)
    y_mem = jnp.dot(o_mem, wm_ref[0], preferred_element_type=F32)
    y_conv = jnp.dot(c_ref[...], wc_ref[0], preferred_element_type=F32)
    y_att = jnp.dot(oatt_ref[...], wa_ref[0], preferred_element_type=F32)
    merged = (gate_ref[:, 0:D_MODEL].astype(F32) * y_conv
              + gate_ref[:, D_MODEL:2 * D_MODEL].astype(F32) * y_att
              + gate_ref[:, 2 * D_MODEL:].astype(F32) * y_mem)
    y = jnp.dot(merged.astype(BF16), wo_ref[0], preferred_element_type=F32)
    out_ref[...] = x_ref[...] + _rms(y, gpost_ref[...])


def _mix(x2, c2, qm, kv, o_att, gates, wc, wa, wm, wo, layer, gpost, w_ffn_in, w_ffn_out, tiles_per_batch):
    T = x2.shape[0]
    M = kv.shape[1]
    steps = T // MIX_TM
    in_rows = D_MODEL // steps
    out_steps = FFN_HIDDEN // FFN_OUT_CAST_ROWS
    assert in_rows % BF16_ROWS == 0 and FFN_OUT_CAST_ROWS % BF16_ROWS == 0 and out_steps <= steps

    def w_slice(rows_per_step, width, lead, nsteps):
        return pl.BlockSpec((1, rows_per_step, width), lambda i: (lead, jnp.minimum(i, nsteps - 1), 0))

    def rows(width):
        return pl.BlockSpec((MIX_TM, width), lambda i: (i, 0))

    def whole(shape):
        return pl.BlockSpec(shape, lambda i: (0,) * len(shape))

    return pl.pallas_call(
        _mix_kernel,
        grid=(T // MIX_TM,),
        in_specs=[rows(D_MODEL), rows(CONV_WIDTH), rows(MEM_WIDTH),
                  pl.BlockSpec((1, M, 2 * MEM_WIDTH), lambda i: (i // tiles_per_batch, 0, 0)),
                  rows(ATT_OUT), rows(GATE_WIDTH),
                  _layer_weight(wc, layer), _layer_weight(wa, layer),
                  _layer_weight(wm, layer), _layer_weight(wo, layer), whole((1, D_MODEL)),
                  w_slice(in_rows, 2 * FFN_HIDDEN, layer, steps),
                  w_slice(FFN_OUT_CAST_ROWS, D_MODEL, layer, out_steps)],
        out_specs=[rows(D_MODEL),
                   w_slice(in_rows, 2 * FFN_HIDDEN, 0, steps),
                   w_slice(FFN_OUT_CAST_ROWS, D_MODEL, 0, out_steps)],
        out_shape=[jax.ShapeDtypeStruct((T, D_MODEL), F32),
                   jax.ShapeDtypeStruct((1, D_MODEL, 2 * FFN_HIDDEN), BF16),
                   jax.ShapeDtypeStruct((1, FFN_HIDDEN, D_MODEL), BF16)],
        compiler_params=pltpu.CompilerParams(
            dimension_semantics=("arbitrary",), vmem_limit_bytes=VMEM_LIMIT),
        name="mix_out",
    )(x2, c2, qm, kv, o_att, gates, wc, wa, wm, wo, gpost, w_ffn_in, w_ffn_out)


FFN_TM = 1024
FFN_TH = 256
FFN_CHUNKS = FFN_HIDDEN // FFN_TH


def _ffn_kernel(x_ref, gpre_ref, win_ref, wout_ref, gpost_ref, out_ref):
    x = x_ref[...]
    h = _rms(x, gpre_ref[...]).astype(BF16)
    acc = None
    for c in range(FFN_CHUNKS):
        lo = c * FFN_TH
        gv = jnp.dot(h, win_ref[0, :, lo:lo + FFN_TH], preferred_element_type=F32)
        uv = jnp.dot(h, win_ref[0, :, FFN_HIDDEN + lo:FFN_HIDDEN + lo + FFN_TH], preferred_element_type=F32)
        a = (gv * _sigmoid(gv) * uv).astype(BF16)
        part = jnp.dot(a, wout_ref[0, lo:lo + FFN_TH, :], preferred_element_type=F32)
        acc = part if acc is None else acc + part
    out_ref[...] = x + _rms(acc, gpost_ref[...])


def _ffn(x2, gpre, w_in, w_out, layer, gpost):
    T = x2.shape[0]

    def resident(shape):
        return pl.BlockSpec(shape, lambda i: (0, 0), pipeline_mode=pl.Buffered(1))

    return pl.pallas_call(
        _ffn_kernel,
        grid=(T // FFN_TM,),
        in_specs=[pl.BlockSpec((FFN_TM, D_MODEL), lambda i: (i, 0)),
                  resident((1, D_MODEL)),
                  _layer_weight(w_in, layer), _layer_weight(w_out, layer),
                  resident((1, D_MODEL))],
        out_specs=pl.BlockSpec((FFN_TM, D_MODEL), lambda i: (i, 0)),
        out_shape=jax.ShapeDtypeStruct((T, D_MODEL), F32),
        compiler_params=pltpu.CompilerParams(
            dimension_semantics=("parallel",), vmem_limit_bytes=VMEM_LIMIT),
        name="ffn",
    )(x2, gpre, w_in, w_out, gpost)


def kernel(x, mem, rel_bias, norm_mix_pre, w_in, b_gate, conv_dw, conv_dw_bias, conv_ln_g, conv_ln_b,
           w_conv_out, w_att_out, norm_mem, w_mem_kv, w_mem_out, w_out, norm_mix_post, norm_ffn_pre,
           w_ffn_in, w_ffn_out, norm_ffn_post):
    B, S, D = x.shape
    depth = w_in.shape[0]
    T = B * S
    biases = [_band_bias(rel_bias, g, d) for g, d in enumerate(DILATIONS)]

    def row(v):
        return v.reshape(1, -1)

    bf16 = {name: w.astype(BF16) for name, w in dict(
        w_in=w_in, w_conv_out=w_conv_out, w_att_out=w_att_out, w_mem_kv=w_mem_kv, w_mem_out=w_mem_out,
        w_out=w_out).items()}
    x2 = x.reshape(T, D)
    for l in range(depth):
        c, qm, qkv0, qkv1, qkv2, gates = _in_proj(
            x2, row(norm_mix_pre[l]), bf16["w_in"], l, row(b_gate[l]),
            conv_dw[l], row(conv_dw_bias[l]), row(conv_ln_g[l]), row(conv_ln_b[l]), B, S)
        o_att = _attention((qkv0, qkv1, qkv2), biases)
        kv = _mem_kv(mem, row(norm_mem[l]), bf16["w_mem_kv"], l)
        x2, wfi, wfo = _mix(x2, c, qm, kv, o_att.reshape(T, ATT_OUT), gates,
                            bf16["w_conv_out"], bf16["w_att_out"], bf16["w_mem_out"], bf16["w_out"], l,
                            row(norm_mix_post[l]), w_ffn_in, w_ffn_out, S // MIX_TM)
        x2 = _ffn(x2, row(norm_ffn_pre[l]), wfi, wfo, 0, row(norm_ffn_post[l]))
    return x2.reshape(B, S, D)
```

```python
import functools
import math

import numpy as np
import jax
import jax.numpy as jnp
from jax import lax
from jax.experimental import pallas as pl
from jax.experimental.pallas import tpu as pltpu

F32 = jnp.float32
BF16 = jnp.bfloat16

D_MODEL = 1024
CONV_WIDTH = 512
CONV_KSIZE = 31
CONV_PAD = CONV_KSIZE // 2
DILATIONS = (1, 4, 16)
RADIUS = 64
N_GROUPS = 3
HEADS = 4
HEAD_DIM = 64
ATT_OUT = HEADS * HEAD_DIM
QKV_WIDTH = 3 * ATT_OUT
MEM_HEADS = 4
MEM_HEAD_DIM = 128
MEM_WIDTH = 512
FFN_HIDDEN = 2816
NUM_BUCKETS = 32
MAX_DISTANCE = 1024
RMS_EPS = 1e-6
LN_EPS = 1e-5
NEG_INF = -1e30

GATE_WIDTH = 3 * D_MODEL
SUBQ = 128
SUBK = SUBQ + 2 * RADIUS
LANES = 128
BF16_ROWS = 16
LSE_LANES = 128
LSE_PER_HEAD = LSE_LANES // HEADS
STAT_LANES = LSE_PER_HEAD // 2
LOG2E = math.log2(math.e)

VMEM_LIMIT = 56 * 1024 * 1024


def _sigmoid(v):
    return 1.0 / (1.0 + jnp.exp(-v))


def _rms(v, g):
    return v * lax.rsqrt(jnp.mean(v * v, axis=-1, keepdims=True) + RMS_EPS) * g


def _derived_zero(v):
    bits = lax.bitcast_convert_type(v, jnp.uint32)
    bits = lax.shift_right_logical(lax.shift_right_logical(bits, jnp.uint32(16)), jnp.uint32(16))
    return lax.bitcast_convert_type(bits, F32)


def _layer_weight(w, layer):
    return pl.BlockSpec((1,) + w.shape[1:], lambda *_: (layer, 0, 0), pipeline_mode=pl.Buffered(1))


IN_TM = 512
IN_TN = 768
AG_WIDTH = 2 * CONV_WIDTH
QKV_COL = AG_WIDTH
QM_COL = QKV_COL + N_GROUPS * QKV_WIDTH
GATE_COL = QM_COL + MEM_WIDTH
N_GATE_TILES = GATE_WIDTH // IN_TN
assert IN_TN == QKV_WIDTH
DOT_TN = 256
SUBLANES = 8
CONV_HALO = 16
CONV_HALF = 256
CONV_RC = 16
CONV_FIRST = CONV_HALO - CONV_PAD
CONV_SHIFT_ROWS = CONV_HALF + (CONV_FIRST + CONV_KSIZE - 1) // SUBLANES * SUBLANES
IN_EXT = IN_TM + 2 * CONV_HALO


def _in_proj_kernel(x_ref, xp_ref, xn_ref, g_ref, w_ref, b_ref, cw_ref, cb_ref, lg_ref, lb_ref,
                    c_ref, qm_ref, q0_ref, q1_ref, q2_ref, gate_ref, acc_ref, u_ref, us_ref,
                    *, tiles_per_batch):
    ib = pl.program_id(0) % tiles_per_batch
    x_ext = jnp.concatenate([xp_ref[...], x_ref[...], xn_ref[...]], axis=0)
    h_ext = _rms(x_ext, g_ref[...]).astype(BF16)
    h = h_ext[CONV_HALO:CONV_HALO + IN_TM]

    for t in range(CONV_WIDTH // DOT_TN):
        cols = slice(t * DOT_TN, (t + 1) * DOT_TN)
        a = jnp.dot(h_ext, w_ref[0, :, cols], preferred_element_type=F32)
        gt = jnp.dot(h_ext, w_ref[0, :, CONV_WIDTH + t * DOT_TN:CONV_WIDTH + (t + 1) * DOT_TN],
                     preferred_element_type=F32)
        u = a * _sigmoid(gt)
        u_ref[0:CONV_HALO, cols] = jnp.where(ib > 0, u[0:CONV_HALO], 0.0)
        u_ref[CONV_HALO:CONV_HALO + IN_TM, cols] = u[CONV_HALO:CONV_HALO + IN_TM]
        u_ref[CONV_HALO + IN_TM:, cols] = jnp.where(ib < tiles_per_batch - 1, u[CONV_HALO + IN_TM:], 0.0)

    def shift_copies(half):
        for s in range(SUBLANES):
            us_ref[s] = u_ref[half * CONV_HALF + s:half * CONV_HALF + s + CONV_SHIFT_ROWS, :]

    def conv_chunk(half, c, zero):
        acc = jnp.concatenate([zero] * (CONV_RC // SUBLANES), axis=0)
        for k in range(CONV_KSIZE):
            off = CONV_FIRST + k
            r0 = c * CONV_RC + off // SUBLANES * SUBLANES
            acc = acc + us_ref[off % SUBLANES, r0:r0 + CONV_RC, :] * cw_ref[k:k + 1, :]
        y = acc + cb_ref[...]
        mu = jnp.mean(y, axis=-1, keepdims=True)
        yc = y - mu
        yn = yc * lax.rsqrt(jnp.mean(yc * yc, axis=-1, keepdims=True) + LN_EPS)
        yn = yn * lg_ref[...] + lb_ref[...]
        act = yn * _sigmoid(yn)
        rows = slice(half * CONV_HALF + c * CONV_RC, half * CONV_HALF + (c + 1) * CONV_RC)
        c_ref[rows, :] = act.astype(BF16)
        return _derived_zero(y[CONV_RC - SUBLANES:])

    def project(col, width):
        return jnp.dot(h, w_ref[0, :, col:col + width], preferred_element_type=F32)

    def mem_queries():
        for t in range(MEM_WIDTH // DOT_TN):
            acc = project(QM_COL + t * DOT_TN, DOT_TN)
            qm_ref[:, t * DOT_TN:(t + 1) * DOT_TN] = acc.astype(BF16)
        return acc

    q_scale = HEAD_DIM ** -0.5 * LOG2E

    def qkv_group(g):
        d, out_ref = DILATIONS[g], (q0_ref, q1_ref, q2_ref)[g]
        for which in range(3):
            acc = project(QKV_COL + (which * N_GROUPS + g) * ATT_OUT, ATT_OUT)
            if which == 0:
                acc = acc * q_scale
            cols = slice(which * ATT_OUT, (which + 1) * ATT_OUT)
            if d == 1:
                out_ref[0, 0, :, cols] = acc.astype(BF16)
                continue
            for half in range(ATT_OUT // LANES):
                acc_ref[2 * which + half] = acc[:, half * LANES:(half + 1) * LANES]
        if d > 1:
            for r in range(d):
                for cb in range(QKV_WIDTH // LANES):
                    out_ref[0, r, :, cb * LANES:(cb + 1) * LANES] = (
                        acc_ref[cb, pl.ds(r, IN_TM // d, stride=d), :].astype(BF16))
        return acc

    def gate_tile(t):
        for sub in range(IN_TN // DOT_TN):
            cols = slice(t * IN_TN + sub * DOT_TN, t * IN_TN + (sub + 1) * DOT_TN)
            acc = project(GATE_COL + cols.start, DOT_TN)
            gate_ref[:, cols] = _sigmoid(acc + b_ref[:, cols]).astype(BF16)
        return acc

    mxu_units = ([mem_queries] + [functools.partial(qkv_group, g) for g in range(N_GROUPS)]
                 + [functools.partial(gate_tile, t) for t in range(N_GATE_TILES)])
    chunks = [(half, c) for half in range(IN_TM // CONV_HALF) for c in range(CONV_HALF // CONV_RC)]
    chunks_per_unit = len(chunks) // len(mxu_units)
    assert chunks_per_unit * len(mxu_units) == len(chunks)
    zero = jnp.zeros((SUBLANES, CONV_WIDTH), F32)
    for step, unit in enumerate(mxu_units):
        for half, c in chunks[step * chunks_per_unit:(step + 1) * chunks_per_unit]:
            if c == 0:
                shift_copies(half)
            zero = conv_chunk(half, c, zero)
        acc = unit()
        tail = _derived_zero(acc[IN_TM - SUBLANES:, acc.shape[1] - LANES:])
        zero = zero + jnp.concatenate([tail] * (CONV_WIDTH // LANES), axis=1)


def _in_proj(x2, g, w, layer, b, conv_w, conv_b, ln_g, ln_b, batch, seq):
    T = x2.shape[0]
    tiles_per_batch = seq // IN_TM
    halo_per_tile = IN_TM // CONV_HALO
    last_halo = T // CONV_HALO - 1

    def qkv_spec(d):
        return pl.BlockSpec((1, d, IN_TM // d, QKV_WIDTH),
                            lambda i: (i // tiles_per_batch, 0, i % tiles_per_batch, 0))

    def resident(shape):
        return pl.BlockSpec(shape, lambda i: (0, 0), pipeline_mode=pl.Buffered(1))

    def rows(width):
        return pl.BlockSpec((IN_TM, width), lambda i: (i, 0))

    return pl.pallas_call(
        functools.partial(_in_proj_kernel, tiles_per_batch=tiles_per_batch),
        grid=(T // IN_TM,),
        in_specs=[
            rows(D_MODEL),
            pl.BlockSpec((CONV_HALO, D_MODEL), lambda i: (jnp.maximum(i * halo_per_tile - 1, 0), 0)),
            pl.BlockSpec((CONV_HALO, D_MODEL), lambda i: (jnp.minimum((i + 1) * halo_per_tile, last_halo), 0)),
            resident((1, D_MODEL)),
            _layer_weight(w, layer),
            resident((1, GATE_WIDTH)),
            resident((CONV_KSIZE, CONV_WIDTH)), resident((1, CONV_WIDTH)),
            resident((1, CONV_WIDTH)), resident((1, CONV_WIDTH)),
        ],
        out_specs=[rows(CONV_WIDTH), rows(MEM_WIDTH),
                   qkv_spec(DILATIONS[0]), qkv_spec(DILATIONS[1]), qkv_spec(DILATIONS[2]),
                   rows(GATE_WIDTH)],
        out_shape=[jax.ShapeDtypeStruct((T, CONV_WIDTH), BF16), jax.ShapeDtypeStruct((T, MEM_WIDTH), BF16)]
        + [jax.ShapeDtypeStruct((batch, d, seq // d, QKV_WIDTH), BF16) for d in DILATIONS]
        + [jax.ShapeDtypeStruct((T, GATE_WIDTH), BF16)],
        scratch_shapes=[pltpu.VMEM((QKV_WIDTH // LANES, IN_TM, LANES), F32),
                        pltpu.VMEM((IN_EXT, CONV_WIDTH), F32),
                        pltpu.VMEM((SUBLANES, CONV_SHIFT_ROWS, CONV_WIDTH), F32)],
        compiler_params=pltpu.CompilerParams(
            dimension_semantics=("parallel",), vmem_limit_bytes=VMEM_LIMIT),
        name="in_proj",
    )(x2, x2, x2, g, w, b, conv_w, conv_b, ln_g, ln_b)


ATT_TILE = 2048
ATT_COMBINE_ROWS = 256


def _att_kernel(c0_ref, p0_ref, n0_ref, c1_ref, p1_ref, n1_ref, c2_ref, p2_ref, n2_ref,
                b0_ref, b1_ref, b2_ref, *refs):
    n_cast = (len(refs) - 4) // 2
    o_ref = refs[n_cast]
    onat_ref, lnat_ref, edge_ref = refs[2 * n_cast + 1:]
    for src_ref, dst_ref in zip(refs[:n_cast], refs[n_cast + 1:2 * n_cast + 1]):
        dst_ref[...] = src_ref[...].astype(BF16)
    i = pl.program_id(1)
    n = pl.num_programs(1)
    stat_of_lane = lax.broadcasted_iota(jnp.int32, (1, LSE_LANES), 1) // STAT_LANES
    key_col = lax.broadcasted_iota(jnp.int32, (1, SUBK), 1)
    K0, V0 = ATT_OUT, 2 * ATT_OUT
    even_head_lanes = lax.broadcasted_iota(jnp.int32, (1, LANES), 1) < HEAD_DIM

    before_start = jnp.where(key_col >= jnp.where(i == 0, RADIUS, 0), 0.0, NEG_INF)
    past_end = jnp.where(key_col < jnp.where(i == n - 1, SUBK - RADIUS, SUBK), 0.0, NEG_INF)
    for g, (d, b_ref) in enumerate(zip(DILATIONS, (b0_ref, b1_ref, b2_ref))):
        if ATT_TILE // d == SUBQ:
            edge_ref[2 * g] = b_ref[...] + before_start + past_end
        else:
            edge_ref[2 * g] = b_ref[...] + before_start
            edge_ref[2 * g + 1] = b_ref[...] + past_end

    def subtile(g, d, q, k, v, bias, row0):
        zeros = jnp.zeros((SUBQ, LANES), BF16)
        blocks = []
        for h in range(HEADS):
            tile = q[:, h // 2 * LANES:(h // 2 + 1) * LANES]
            tile = jnp.where(even_head_lanes, tile, zeros) if h % 2 == 0 else jnp.where(even_head_lanes, zeros, tile)
            blocks.append(jnp.concatenate([tile, zeros] if h < 2 else [zeros, tile], axis=1))
        qs = jnp.concatenate(blocks, axis=0)
        s = lax.dot_general(qs, k, (((1,), (1,)), ((), ())), preferred_element_type=F32) + bias
        m = jnp.max(s, axis=-1, keepdims=True)
        e = jnp.exp2(s - m)
        l = jnp.sum(e, axis=-1, keepdims=True)
        o_all = jnp.dot(e.astype(BF16), v, preferred_element_type=F32)
        o = jnp.concatenate(
            [jnp.where(even_head_lanes,
                       o_all[2 * t * SUBQ:(2 * t + 1) * SUBQ, t * LANES:(t + 1) * LANES],
                       o_all[(2 * t + 1) * SUBQ:(2 * t + 2) * SUBQ, t * LANES:(t + 1) * LANES])
             for t in range(ATT_OUT // LANES)], axis=1)
        stats = jnp.broadcast_to(l[(HEADS - 1) * SUBQ:], (SUBQ, LSE_LANES))
        stats = jnp.where(stat_of_lane == 2 * (HEADS - 1), m[(HEADS - 1) * SUBQ:], stats)
        for h in range(HEADS - 1):
            rows = slice(h * SUBQ, (h + 1) * SUBQ)
            stats = jnp.where(stat_of_lane == 2 * h, m[rows], stats)
            stats = jnp.where(stat_of_lane == 2 * h + 1, l[rows], stats)
        rows = pl.ds(row0, SUBQ) if d == 1 else pl.ds(row0, SUBQ, stride=d)
        for half in range(ATT_OUT // LANES):
            onat_ref[2 * g + half, rows, :] = o[:, half * LANES:(half + 1) * LANES]
        lnat_ref[g, rows, :] = stats

    groups = ((c0_ref, p0_ref, n0_ref, b0_ref), (c1_ref, p1_ref, n1_ref, b1_ref),
              (c2_ref, p2_ref, n2_ref, b2_ref))
    for g, (d, (c_ref, p_ref, n_ref, bias_ref)) in enumerate(zip(DILATIONS, groups)):
        tq = ATT_TILE // d
        nsub = tq // SUBQ
        for r in range(d):
            for j in range(nsub):
                parts = [(c_ref, slice(max(j * SUBQ - RADIUS, 0), min(j * SUBQ + SUBQ + RADIUS, tq)))]
                if j == 0:
                    parts = [(p_ref, slice(None))] + parts
                if j == nsub - 1:
                    parts = parts + [(n_ref, slice(None))]
                k, v = [jnp.concatenate([ref[0, r, rows, col:col + ATT_OUT] for ref, rows in parts], axis=0)
                        for col in (K0, V0)]
                q = c_ref[0, r, j * SUBQ:(j + 1) * SUBQ, 0:ATT_OUT]
                if j == 0:
                    bias = edge_ref[2 * g]
                elif j == nsub - 1:
                    bias = edge_ref[2 * g + 1]
                else:
                    bias = bias_ref[...]
                subtile(g, d, q, k, v, bias, j * SUBQ * d + r)

    w_lane = lax.broadcasted_iota(jnp.int32, (2 * LSE_LANES, ATT_OUT), 0) % LSE_LANES
    o_lane = lax.broadcasted_iota(jnp.int32, (2 * LSE_LANES, ATT_OUT), 1)
    spread = (w_lane == o_lane // HEAD_DIM * LSE_PER_HEAD).astype(BF16)
    weight_lane = lax.broadcasted_iota(jnp.int32, (1, LSE_LANES), 1) % LSE_PER_HEAD == 0

    def expand(w):
        w = jnp.where(weight_lane, w, 0.0)
        hi = w.astype(BF16)
        lo = (w - hi.astype(F32)).astype(BF16)
        return jnp.dot(jnp.concatenate([hi, lo], axis=1), spread, preferred_element_type=F32)

    def combine(t, carry):
        rows = pl.ds(pl.multiple_of(t * ATT_COMBINE_ROWS, ATT_COMBINE_ROWS), ATT_COMBINE_ROWS)
        stats = [lnat_ref[g, rows, :] for g in range(N_GROUPS)]
        mx = jnp.maximum(jnp.maximum(stats[0], stats[1]), stats[2])
        a = [jnp.exp2(st - mx) for st in stats]
        sums = [pltpu.roll(st, LSE_LANES - STAT_LANES, 1) for st in stats]
        inv = 1.0 / (a[0] * sums[0] + a[1] * sums[1] + a[2] * sums[2])
        o = None
        for g in range(N_GROUPS):
            acc = jnp.concatenate([onat_ref[2 * g, rows, :], onat_ref[2 * g + 1, rows, :]], axis=1)
            term = expand(a[g] * inv) * acc
            o = term if o is None else o + term
        o_ref[0, rows, :] = o.astype(BF16)
        return carry

    lax.fori_loop(0, ATT_TILE // ATT_COMBINE_ROWS, combine, 0)


def _attention(qkv, biases, weights, layer):
    B = qkv[0].shape[0]
    S = qkv[0].shape[2]
    tiles = S // ATT_TILE
    steps = B * tiles

    def w_slice(w, lead):
        assert w.shape[1] % (steps * BF16_ROWS) == 0
        return pl.BlockSpec((1, w.shape[1] // steps, w.shape[2]), lambda b, i: (lead, b * tiles + i, 0))
    in_specs = []
    for d in DILATIONS:
        tq = ATT_TILE // d
        nb = tq // RADIUS
        last_b = S // d // RADIUS - 1
        in_specs += [
            pl.BlockSpec((1, d, tq, QKV_WIDTH), lambda b, i: (b, 0, i, 0)),
            pl.BlockSpec((1, d, RADIUS, QKV_WIDTH),
                         lambda b, i, nb=nb: (b, 0, jnp.maximum(i * nb - 1, 0), 0)),
            pl.BlockSpec((1, d, RADIUS, QKV_WIDTH),
                         lambda b, i, nb=nb, last_b=last_b: (b, 0, jnp.minimum((i + 1) * nb, last_b), 0)),
        ]
    in_specs += [pl.BlockSpec((HEADS * SUBQ, SUBK), lambda b, i: (0, 0))] * N_GROUPS
    in_specs += [w_slice(w, layer) for w in weights]
    args = []
    for a in qkv:
        args += [a, a, a]
    return pl.pallas_call(
        _att_kernel,
        grid=(B, tiles),
        in_specs=in_specs,
        out_specs=[pl.BlockSpec((1, ATT_TILE, ATT_OUT), lambda b, i: (b, i, 0))]
        + [w_slice(w, 0) for w in weights],
        out_shape=[jax.ShapeDtypeStruct((B, S, ATT_OUT), BF16)]
        + [jax.ShapeDtypeStruct((1,) + w.shape[1:], BF16) for w in weights],
        scratch_shapes=[pltpu.VMEM((N_GROUPS * (ATT_OUT // LANES), ATT_TILE, LANES), F32),
                        pltpu.VMEM((N_GROUPS, ATT_TILE, LSE_LANES), F32),
                        pltpu.VMEM((2 * N_GROUPS, HEADS * SUBQ, SUBK), F32)],
        compiler_params=pltpu.CompilerParams(
            dimension_semantics=("arbitrary", "arbitrary"), vmem_limit_bytes=VMEM_LIMIT),
        name="dilated_att",
    )(*args, *biases, *weights)


def _t5_bucket_np(rel):
    nb = NUM_BUCKETS // 2
    max_exact = nb // 2
    ret = np.where(rel > 0, nb, 0)
    n = np.abs(rel)
    nf = np.maximum(n, 1).astype(np.float32)
    ratio = np.log(nf / np.float32(max_exact)) / np.float32(math.log(MAX_DISTANCE / max_exact))
    large = max_exact + (ratio * np.float32(nb - max_exact)).astype(np.int32)
    large = np.minimum(large, nb - 1)
    return ret + np.where(n < max_exact, n, large)


def _band_bias(rel_bias, g, dilation):
    period = SUBQ + SUBK
    nband = 2 * RADIUS + 1
    bucket = _t5_bucket_np((np.arange(nband) - RADIUS) * dilation)
    onehot = np.zeros((period, NUM_BUCKETS), np.float32)
    onehot[np.arange(nband), bucket] = 1.0
    tab = rel_bias[:, g * HEADS:(g + 1) * HEADS].astype(F32)
    t = jnp.dot(jnp.asarray(onehot), tab, precision=lax.Precision.HIGHEST)
    t = jnp.where((np.arange(period) < nband)[:, None], t, NEG_INF).T
    skew = jnp.tile(t, (1, SUBQ))[:, :SUBQ * (period - 1)].reshape(HEADS, SUBQ, period - 1)
    return skew[:, :, :SUBK].reshape(HEADS * SUBQ, SUBK) * LOG2E


def _mem_kv_kernel(mem_ref, g_ref, w_ref, kv_ref):
    h = _rms(mem_ref[0], g_ref[...]).astype(BF16)
    kv_ref[0] = jnp.dot(h, w_ref[0], preferred_element_type=F32).astype(BF16)


def _mem_kv(mem, g, w, layer):
    B, M, _ = mem.shape
    return pl.pallas_call(
        _mem_kv_kernel,
        grid=(B,),
        in_specs=[pl.BlockSpec((1, M, D_MODEL), lambda b: (b, 0, 0)),
                  pl.BlockSpec((1, D_MODEL), lambda b: (0, 0)),
                  _layer_weight(w, layer)],
        out_specs=pl.BlockSpec((1, M, 2 * MEM_WIDTH), lambda b: (b, 0, 0)),
        out_shape=jax.ShapeDtypeStruct((B, M, 2 * MEM_WIDTH), BF16),
        compiler_params=pltpu.CompilerParams(
            dimension_semantics=("parallel",), vmem_limit_bytes=VMEM_LIMIT),
        name="mem_kv",
    )(mem, g, w)


MIX_TM = 512
FFN_OUT_CAST_ROWS = 176


def _mix_kernel(x_ref, c_ref, qm_ref, kv_ref, oatt_ref, gate_ref, wc_ref, wa_ref, wm_ref, wo_ref,
                gpost_ref, wfi_ref, wfo_ref, out_ref, wfi_out_ref, wfo_out_ref):
    wfi_out_ref[...] = wfi_ref[...].astype(BF16)

    @pl.when(pl.program_id(0) < FFN_HIDDEN // FFN_OUT_CAST_ROWS)
    def _():
        wfo_out_ref[...] = wfo_ref[...].astype(BF16)

    scale = MEM_HEAD_DIM ** -0.5
    heads = []
    for h in range(MEM_HEADS):
        lo = h * MEM_HEAD_DIM
        qh = qm_ref[:, lo:lo + MEM_HEAD_DIM]
        kh = kv_ref[0, :, lo:lo + MEM_HEAD_DIM]
        vh = kv_ref[0, :, MEM_WIDTH + lo:MEM_WIDTH + lo + MEM_HEAD_DIM]
        s = lax.dot_general(qh, kh, (((1,), (1,)), ((), ())), preferred_element_type=F32) * scale
        m = jnp.max(s, axis=-1, keepdims=True)
        e = jnp.exp(s - m)
        p = (e * (1.0 / jnp.sum(e, axis=-1, keepdims=True))).astype(BF16)
        heads.append(jnp.dot(p, vh, preferred_element_type=F32).astype(BF16))
    o_mem = jnp.concatenate(heads, axis=-1)
    y_mem = jnp.dot(o_mem, wm_ref[0], preferred_element_type=F32)
    y_conv = jnp.dot(c_ref[...], wc_ref[0], preferred_element_type=F32)
    y_att = jnp.dot(oatt_ref[...], wa_ref[0], preferred_element_type=F32)
    merged = (gate_ref[:, 0:D_MODEL].astype(F32) * y_conv
              + gate_ref[:, D_MODEL:2 * D_MODEL].astype(F32) * y_att
              + gate_ref[:, 2 * D_MODEL:].astype(F32) * y_mem)
    y = jnp.dot(merged.astype(BF16), wo_ref[0], preferred_element_type=F32)
    out_ref[...] = x_ref[...] + _rms(y, gpost_ref[...])


def _mix(x2, c2, qm, kv, o_att, gates, wc, wa, wm, wo, layer, gpost, w_ffn_in, w_ffn_out, ffn_layer,
         tiles_per_batch):
    T = x2.shape[0]
    M = kv.shape[1]
    steps = T // MIX_TM
    in_rows = D_MODEL // steps
    out_steps = FFN_HIDDEN // FFN_OUT_CAST_ROWS
    assert in_rows % BF16_ROWS == 0 and FFN_OUT_CAST_ROWS % BF16_ROWS == 0 and out_steps <= steps

    def w_slice(rows_per_step, width, lead, nsteps):
        return pl.BlockSpec((1, rows_per_step, width), lambda i: (lead, jnp.minimum(i, nsteps - 1), 0))

    def rows(width):
        return pl.BlockSpec((MIX_TM, width), lambda i: (i, 0))

    def whole(shape):
        return pl.BlockSpec(shape, lambda i: (0,) * len(shape))

    return pl.pallas_call(
        _mix_kernel,
        grid=(T // MIX_TM,),
        in_specs=[rows(D_MODEL), rows(CONV_WIDTH), rows(MEM_WIDTH),
                  pl.BlockSpec((1, M, 2 * MEM_WIDTH), lambda i: (i // tiles_per_batch, 0, 0)),
                  rows(ATT_OUT), rows(GATE_WIDTH),
                  _layer_weight(wc, layer), _layer_weight(wa, layer),
                  _layer_weight(wm, layer), _layer_weight(wo, layer), whole((1, D_MODEL)),
                  w_slice(in_rows, 2 * FFN_HIDDEN, ffn_layer, steps),
                  w_slice(FFN_OUT_CAST_ROWS, D_MODEL, ffn_layer, out_steps)],
        out_specs=[rows(D_MODEL),
                   w_slice(in_rows, 2 * FFN_HIDDEN, 0, steps),
                   w_slice(FFN_OUT_CAST_ROWS, D_MODEL, 0, out_steps)],
        out_shape=[jax.ShapeDtypeStruct((T, D_MODEL), F32),
                   jax.ShapeDtypeStruct((1, D_MODEL, 2 * FFN_HIDDEN), BF16),
                   jax.ShapeDtypeStruct((1, FFN_HIDDEN, D_MODEL), BF16)],
        compiler_params=pltpu.CompilerParams(
            dimension_semantics=("arbitrary",), vmem_limit_bytes=VMEM_LIMIT),
        name="mix_out",
    )(x2, c2, qm, kv, o_att, gates, wc, wa, wm, wo, gpost, w_ffn_in, w_ffn_out)


FFN_TM = 1024
FFN_TH = 256
FFN_CHUNKS = FFN_HIDDEN // FFN_TH


def _ffn_kernel(x_ref, gpre_ref, win_ref, wout_ref, gpost_ref, *refs):
    out_ref = refs[0] if len(refs) == 1 else refs[1]
    if len(refs) == 3:
        refs[2][...] = refs[0][...].astype(BF16)
    x = x_ref[...]
    h = _rms(x, gpre_ref[...]).astype(BF16)
    acc = None
    for c in range(FFN_CHUNKS):
        lo = c * FFN_TH
        gv = jnp.dot(h, win_ref[0, :, lo:lo + FFN_TH], preferred_element_type=F32)
        uv = jnp.dot(h, win_ref[0, :, FFN_HIDDEN + lo:FFN_HIDDEN + lo + FFN_TH], preferred_element_type=F32)
        a = (gv * _sigmoid(gv) * uv).astype(BF16)
        part = jnp.dot(a, wout_ref[0, lo:lo + FFN_TH, :], preferred_element_type=F32)
        acc = part if acc is None else acc + part
    out_ref[...] = x + _rms(acc, gpost_ref[...])


def _ffn(x2, gpre, w_in, w_out, layer, gpost, next_w=None):
    T = x2.shape[0]
    steps = T // FFN_TM
    cast_in, cast_out, cast_shape, cast_args = [], [], [], []
    if next_w is not None:
        w, lead = next_w
        assert w.shape[1] % (steps * BF16_ROWS) == 0
        cast_in = [pl.BlockSpec((1, w.shape[1] // steps, w.shape[2]), lambda i: (lead, i, 0))]
        cast_out = [pl.BlockSpec((1, w.shape[1] // steps, w.shape[2]), lambda i: (0, i, 0))]
        cast_shape = [jax.ShapeDtypeStruct((1,) + w.shape[1:], BF16)]
        cast_args = [w]

    def resident(shape):
        return pl.BlockSpec(shape, lambda i: (0, 0), pipeline_mode=pl.Buffered(1))

    outs = pl.pallas_call(
        _ffn_kernel,
        grid=(steps,),
        in_specs=[pl.BlockSpec((FFN_TM, D_MODEL), lambda i: (i, 0)),
                  resident((1, D_MODEL)),
                  _layer_weight(w_in, layer), _layer_weight(w_out, layer),
                  resident((1, D_MODEL))] + cast_in,
        out_specs=[pl.BlockSpec((FFN_TM, D_MODEL), lambda i: (i, 0))] + cast_out,
        out_shape=[jax.ShapeDtypeStruct((T, D_MODEL), F32)] + cast_shape,
        compiler_params=pltpu.CompilerParams(
            dimension_semantics=("arbitrary",), vmem_limit_bytes=VMEM_LIMIT),
        name="ffn",
    )(x2, gpre, w_in, w_out, gpost, *cast_args)
    return outs[0], (outs[1] if next_w is not None else None)


def kernel(x, mem, rel_bias, norm_mix_pre, w_in, b_gate, conv_dw, conv_dw_bias, conv_ln_g, conv_ln_b,
           w_conv_out, w_att_out, norm_mem, w_mem_kv, w_mem_out, w_out, norm_mix_post, norm_ffn_pre,
           w_ffn_in, w_ffn_out, norm_ffn_post):
    B, S, D = x.shape
    depth = w_in.shape[0]
    T = B * S
    biases = [_band_bias(rel_bias, g, d) for g, d in enumerate(DILATIONS)]

    def row(v):
        return v.reshape(1, -1)

    w_in_bf16 = w_in[:1].astype(BF16)
    x2 = x.reshape(T, D)
    for l in range(depth):
        c, qm, qkv0, qkv1, qkv2, gates = _in_proj(
            x2, row(norm_mix_pre[l]), w_in_bf16, 0, row(b_gate[l]),
            conv_dw[l], row(conv_dw_bias[l]), row(conv_ln_g[l]), row(conv_ln_b[l]), B, S)
        o_att, wkv, wc, wa, wm, wo = _attention(
            (qkv0, qkv1, qkv2), biases, (w_mem_kv, w_conv_out, w_att_out, w_mem_out, w_out), l)
        kv = _mem_kv(mem, row(norm_mem[l]), wkv, 0)
        x2, wfi, wfo = _mix(x2, c, qm, kv, o_att.reshape(T, ATT_OUT), gates, wc, wa, wm, wo, 0,
                            row(norm_mix_post[l]), w_ffn_in, w_ffn_out, l, S // MIX_TM)
        next_w_in = (w_in, l + 1) if l + 1 < depth else None
        x2, w_in_bf16 = _ffn(x2, row(norm_ffn_pre[l]), wfi, wfo, 0, row(norm_ffn_post[l]), next_w_in)
    return x2.reshape(B, S, D)
```

```python
import functools
import math

import numpy as np
import jax
import jax.numpy as jnp
from jax import lax
from jax.experimental import pallas as pl
from jax.experimental.pallas import tpu as pltpu

F32 = jnp.float32
BF16 = jnp.bfloat16

D_MODEL = 1024
CONV_WIDTH = 512
CONV_KSIZE = 31
CONV_PAD = CONV_KSIZE // 2
DILATIONS = (1, 4, 16)
RADIUS = 64
N_GROUPS = 3
HEADS = 4
HEAD_DIM = 64
ATT_OUT = HEADS * HEAD_DIM
QKV_WIDTH = 3 * ATT_OUT
MEM_HEADS = 4
MEM_HEAD_DIM = 128
MEM_WIDTH = 512
FFN_HIDDEN = 2816
NUM_BUCKETS = 32
MAX_DISTANCE = 1024
RMS_EPS = 1e-6
LN_EPS = 1e-5
NEG_INF = -1e30

GATE_WIDTH = 3 * D_MODEL
SUBQ = 128
SUBK = SUBQ + 2 * RADIUS
LANES = 128
BF16_ROWS = 16
LSE_LANES = 128
LSE_PER_HEAD = LSE_LANES // HEADS
STAT_LANES = LSE_PER_HEAD // 2
LOG2E = math.log2(math.e)

V7X_VMEM_BYTES = 64 * 1024 * 1024
VMEM_LIMIT = V7X_VMEM_BYTES * 7 // 8


def _sigmoid(v):
    return 1.0 / (1.0 + jnp.exp(-v))


def _rms(v, g):
    return v * lax.rsqrt(jnp.mean(v * v, axis=-1, keepdims=True) + RMS_EPS) * g


def _derived_zero(v):
    bits = lax.bitcast_convert_type(v, jnp.uint32)
    bits = lax.shift_right_logical(lax.shift_right_logical(bits, jnp.uint32(16)), jnp.uint32(16))
    return lax.bitcast_convert_type(bits, F32)


def _layer_weight(w, layer):
    return pl.BlockSpec((1,) + w.shape[1:], lambda *_: (layer, 0, 0), pipeline_mode=pl.Buffered(1))


IN_TM = 512
IN_TN = 768
AG_WIDTH = 2 * CONV_WIDTH
QKV_COL = AG_WIDTH
QM_COL = QKV_COL + N_GROUPS * QKV_WIDTH
GATE_COL = QM_COL + MEM_WIDTH
N_GATE_TILES = GATE_WIDTH // IN_TN
assert IN_TN == QKV_WIDTH
DOT_TN = 256
SUBLANES = 8
CONV_HALO = 16
CONV_HALF = 256
CONV_RC = 16
CONV_FIRST = CONV_HALO - CONV_PAD
CONV_SHIFT_ROWS = CONV_HALF + (CONV_FIRST + CONV_KSIZE - 1) // SUBLANES * SUBLANES
IN_EXT = IN_TM + 2 * CONV_HALO


def _in_proj_kernel(x_ref, xp_ref, xn_ref, g_ref, w_ref, b_ref, cw_ref, cb_ref, lg_ref, lb_ref,
                    c_ref, qm_ref, q0_ref, q1_ref, q2_ref, gate_ref, acc_ref, u_ref, us_ref,
                    *, tiles_per_batch):
    ib = pl.program_id(0) % tiles_per_batch
    x_ext = jnp.concatenate([xp_ref[...], x_ref[...], xn_ref[...]], axis=0)
    h_ext = _rms(x_ext, g_ref[...]).astype(BF16)
    h = h_ext[CONV_HALO:CONV_HALO + IN_TM]

    for t in range(CONV_WIDTH // DOT_TN):
        cols = slice(t * DOT_TN, (t + 1) * DOT_TN)
        a = jnp.dot(h_ext, w_ref[0, :, cols], preferred_element_type=F32)
        gt = jnp.dot(h_ext, w_ref[0, :, CONV_WIDTH + t * DOT_TN:CONV_WIDTH + (t + 1) * DOT_TN],
                     preferred_element_type=F32)
        u = a * _sigmoid(gt)
        u_ref[0:CONV_HALO, cols] = jnp.where(ib > 0, u[0:CONV_HALO], 0.0)
        u_ref[CONV_HALO:CONV_HALO + IN_TM, cols] = u[CONV_HALO:CONV_HALO + IN_TM]
        u_ref[CONV_HALO + IN_TM:, cols] = jnp.where(ib < tiles_per_batch - 1, u[CONV_HALO + IN_TM:], 0.0)

    def shift_copies(half):
        for s in range(SUBLANES):
            us_ref[s] = u_ref[half * CONV_HALF + s:half * CONV_HALF + s + CONV_SHIFT_ROWS, :]

    def conv_chunk(half, c, zero):
        acc = jnp.concatenate([zero] * (CONV_RC // SUBLANES), axis=0)
        for k in range(CONV_KSIZE):
            off = CONV_FIRST + k
            r0 = c * CONV_RC + off // SUBLANES * SUBLANES
            acc = acc + us_ref[off % SUBLANES, r0:r0 + CONV_RC, :] * cw_ref[k:k + 1, :]
        y = acc + cb_ref[...]
        mu = jnp.mean(y, axis=-1, keepdims=True)
        yc = y - mu
        yn = yc * lax.rsqrt(jnp.mean(yc * yc, axis=-1, keepdims=True) + LN_EPS)
        yn = yn * lg_ref[...] + lb_ref[...]
        act = yn * _sigmoid(yn)
        rows = slice(half * CONV_HALF + c * CONV_RC, half * CONV_HALF + (c + 1) * CONV_RC)
        c_ref[rows, :] = act.astype(BF16)
        return _derived_zero(y[CONV_RC - SUBLANES:])

    def project(col, width):
        return jnp.dot(h, w_ref[0, :, col:col + width], preferred_element_type=F32)

    def mem_queries():
        for t in range(MEM_WIDTH // DOT_TN):
            acc = project(QM_COL + t * DOT_TN, DOT_TN)
            qm_ref[:, t * DOT_TN:(t + 1) * DOT_TN] = acc.astype(BF16)
        return acc

    q_scale = HEAD_DIM ** -0.5 * LOG2E

    def qkv_group(g):
        d, out_ref = DILATIONS[g], (q0_ref, q1_ref, q2_ref)[g]
        for which in range(3):
            acc = project(QKV_COL + (which * N_GROUPS + g) * ATT_OUT, ATT_OUT)
            if which == 0:
                acc = acc * q_scale
            cols = slice(which * ATT_OUT, (which + 1) * ATT_OUT)
            if d == 1:
                out_ref[0, 0, :, cols] = acc.astype(BF16)
                continue
            for half in range(ATT_OUT // LANES):
                acc_ref[2 * which + half] = acc[:, half * LANES:(half + 1) * LANES]
        if d > 1:
            for r in range(d):
                for cb in range(QKV_WIDTH // LANES):
                    out_ref[0, r, :, cb * LANES:(cb + 1) * LANES] = (
                        acc_ref[cb, pl.ds(r, IN_TM // d, stride=d), :].astype(BF16))
        return acc

    def gate_tile(t):
        for sub in range(IN_TN // DOT_TN):
            cols = slice(t * IN_TN + sub * DOT_TN, t * IN_TN + (sub + 1) * DOT_TN)
            acc = project(GATE_COL + cols.start, DOT_TN)
            gate_ref[:, cols] = _sigmoid(acc + b_ref[:, cols]).astype(BF16)
        return acc

    mxu_units = ([mem_queries] + [functools.partial(qkv_group, g) for g in range(N_GROUPS)]
                 + [functools.partial(gate_tile, t) for t in range(N_GATE_TILES)])
    chunks = [(half, c) for half in range(IN_TM // CONV_HALF) for c in range(CONV_HALF // CONV_RC)]
    chunks_per_unit = len(chunks) // len(mxu_units)
    assert chunks_per_unit * len(mxu_units) == len(chunks)
    zero = jnp.zeros((SUBLANES, CONV_WIDTH), F32)
    for step, unit in enumerate(mxu_units):
        for half, c in chunks[step * chunks_per_unit:(step + 1) * chunks_per_unit]:
            if c == 0:
                shift_copies(half)
            zero = conv_chunk(half, c, zero)
        acc = unit()
        tail = _derived_zero(acc[IN_TM - SUBLANES:, acc.shape[1] - LANES:])
        zero = zero + jnp.concatenate([tail] * (CONV_WIDTH // LANES), axis=1)


def _in_proj(x2, g, w, layer, b, conv_w, conv_b, ln_g, ln_b, batch, seq):
    T = x2.shape[0]
    tiles_per_batch = seq // IN_TM
    halo_per_tile = IN_TM // CONV_HALO
    last_halo = T // CONV_HALO - 1

    def qkv_spec(d):
        return pl.BlockSpec((1, d, IN_TM // d, QKV_WIDTH),
                            lambda i: (i // tiles_per_batch, 0, i % tiles_per_batch, 0))

    def resident(shape):
        return pl.BlockSpec(shape, lambda i: (0, 0), pipeline_mode=pl.Buffered(1))

    def rows(width):
        return pl.BlockSpec((IN_TM, width), lambda i: (i, 0))

    return pl.pallas_call(
        functools.partial(_in_proj_kernel, tiles_per_batch=tiles_per_batch),
        grid=(T // IN_TM,),
        in_specs=[
            rows(D_MODEL),
            pl.BlockSpec((CONV_HALO, D_MODEL), lambda i: (jnp.maximum(i * halo_per_tile - 1, 0), 0)),
            pl.BlockSpec((CONV_HALO, D_MODEL), lambda i: (jnp.minimum((i + 1) * halo_per_tile, last_halo), 0)),
            resident((1, D_MODEL)),
            _layer_weight(w, layer),
            resident((1, GATE_WIDTH)),
            resident((CONV_KSIZE, CONV_WIDTH)), resident((1, CONV_WIDTH)),
            resident((1, CONV_WIDTH)), resident((1, CONV_WIDTH)),
        ],
        out_specs=[rows(CONV_WIDTH), rows(MEM_WIDTH),
                   qkv_spec(DILATIONS[0]), qkv_spec(DILATIONS[1]), qkv_spec(DILATIONS[2]),
                   rows(GATE_WIDTH)],
        out_shape=[jax.ShapeDtypeStruct((T, CONV_WIDTH), BF16), jax.ShapeDtypeStruct((T, MEM_WIDTH), BF16)]
        + [jax.ShapeDtypeStruct((batch, d, seq // d, QKV_WIDTH), BF16) for d in DILATIONS]
        + [jax.ShapeDtypeStruct((T, GATE_WIDTH), BF16)],
        scratch_shapes=[pltpu.VMEM((QKV_WIDTH // LANES, IN_TM, LANES), F32),
                        pltpu.VMEM((IN_EXT, CONV_WIDTH), F32),
                        pltpu.VMEM((SUBLANES, CONV_SHIFT_ROWS, CONV_WIDTH), F32)],
        compiler_params=pltpu.CompilerParams(
            dimension_semantics=("parallel",), vmem_limit_bytes=VMEM_LIMIT),
        name="in_proj",
    )(x2, x2, x2, g, w, b, conv_w, conv_b, ln_g, ln_b)


ATT_TILE = 2048
ATT_COMBINE_ROWS = 1024


def _att_kernel(c0_ref, p0_ref, n0_ref, c1_ref, p1_ref, n1_ref, c2_ref, p2_ref, n2_ref,
                b0_ref, b1_ref, b2_ref, *refs):
    n_cast = (len(refs) - 4) // 2
    o_ref = refs[n_cast]
    onat_ref, lnat_ref, edge_ref = refs[2 * n_cast + 1:]
    for src_ref, dst_ref in zip(refs[:n_cast], refs[n_cast + 1:2 * n_cast + 1]):
        dst_ref[...] = src_ref[...].astype(BF16)
    i = pl.program_id(1)
    n = pl.num_programs(1)
    stat_of_lane = lax.broadcasted_iota(jnp.int32, (1, LSE_LANES), 1) // STAT_LANES
    key_col = lax.broadcasted_iota(jnp.int32, (1, SUBK), 1)
    K0, V0 = ATT_OUT, 2 * ATT_OUT
    even_head_lanes = lax.broadcasted_iota(jnp.int32, (1, LANES), 1) < HEAD_DIM

    before_start = jnp.where(key_col >= jnp.where(i == 0, RADIUS, 0), 0.0, NEG_INF)
    past_end = jnp.where(key_col < jnp.where(i == n - 1, SUBK - RADIUS, SUBK), 0.0, NEG_INF)
    for g, (d, b_ref) in enumerate(zip(DILATIONS, (b0_ref, b1_ref, b2_ref))):
        if ATT_TILE // d == SUBQ:
            edge_ref[2 * g] = b_ref[...] + before_start + past_end
        else:
            edge_ref[2 * g] = b_ref[...] + before_start
            edge_ref[2 * g + 1] = b_ref[...] + past_end

    def subtile(g, d, q, k, v, bias, row0):
        zeros = jnp.zeros((SUBQ, LANES), BF16)
        blocks = []
        for h in range(HEADS):
            tile = q[:, h // 2 * LANES:(h // 2 + 1) * LANES]
            tile = jnp.where(even_head_lanes, tile, zeros) if h % 2 == 0 else jnp.where(even_head_lanes, zeros, tile)
            blocks.append(jnp.concatenate([tile, zeros] if h < 2 else [zeros, tile], axis=1))
        qs = jnp.concatenate(blocks, axis=0)
        s = lax.dot_general(qs, k, (((1,), (1,)), ((), ())), preferred_element_type=F32) + bias
        m = jnp.max(s, axis=-1, keepdims=True)
        e = jnp.exp2(s - m)
        l = jnp.sum(e, axis=-1, keepdims=True)
        o_all = jnp.dot(e.astype(BF16), v, preferred_element_type=F32)
        o = jnp.concatenate(
            [jnp.where(even_head_lanes,
                       o_all[2 * t * SUBQ:(2 * t + 1) * SUBQ, t * LANES:(t + 1) * LANES],
                       o_all[(2 * t + 1) * SUBQ:(2 * t + 2) * SUBQ, t * LANES:(t + 1) * LANES])
             for t in range(ATT_OUT // LANES)], axis=1)
        stats = jnp.broadcast_to(l[(HEADS - 1) * SUBQ:], (SUBQ, LSE_LANES))
        stats = jnp.where(stat_of_lane == 2 * (HEADS - 1), m[(HEADS - 1) * SUBQ:], stats)
        for h in range(HEADS - 1):
            rows = slice(h * SUBQ, (h + 1) * SUBQ)
            stats = jnp.where(stat_of_lane == 2 * h, m[rows], stats)
            stats = jnp.where(stat_of_lane == 2 * h + 1, l[rows], stats)
        rows = pl.ds(row0, SUBQ) if d == 1 else pl.ds(row0, SUBQ, stride=d)
        for half in range(ATT_OUT // LANES):
            onat_ref[2 * g + half, rows, :] = o[:, half * LANES:(half + 1) * LANES]
        lnat_ref[g, rows, :] = stats

    groups = ((c0_ref, p0_ref, n0_ref, b0_ref), (c1_ref, p1_ref, n1_ref, b1_ref),
              (c2_ref, p2_ref, n2_ref, b2_ref))
    for g, (d, (c_ref, p_ref, n_ref, bias_ref)) in enumerate(zip(DILATIONS, groups)):
        tq = ATT_TILE // d
        nsub = tq // SUBQ
        for r in range(d):
            for j in range(nsub):
                parts = [(c_ref, slice(max(j * SUBQ - RADIUS, 0), min(j * SUBQ + SUBQ + RADIUS, tq)))]
                if j == 0:
                    parts = [(p_ref, slice(None))] + parts
                if j == nsub - 1:
                    parts = parts + [(n_ref, slice(None))]
                k, v = [jnp.concatenate([ref[0, r, rows, col:col + ATT_OUT] for ref, rows in parts], axis=0)
                        for col in (K0, V0)]
                q = c_ref[0, r, j * SUBQ:(j + 1) * SUBQ, 0:ATT_OUT]
                if j == 0:
                    bias = edge_ref[2 * g]
                elif j == nsub - 1:
                    bias = edge_ref[2 * g + 1]
                else:
                    bias = bias_ref[...]
                subtile(g, d, q, k, v, bias, j * SUBQ * d + r)

    w_lane = lax.broadcasted_iota(jnp.int32, (2 * LSE_LANES, ATT_OUT), 0) % LSE_LANES
    o_lane = lax.broadcasted_iota(jnp.int32, (2 * LSE_LANES, ATT_OUT), 1)
    spread = (w_lane == o_lane // HEAD_DIM * LSE_PER_HEAD).astype(BF16)
    weight_lane = lax.broadcasted_iota(jnp.int32, (1, LSE_LANES), 1) % LSE_PER_HEAD == 0

    def expand(w):
        w = jnp.where(weight_lane, w, 0.0)
        hi = w.astype(BF16)
        lo = (w - hi.astype(F32)).astype(BF16)
        return jnp.dot(jnp.concatenate([hi, lo], axis=1), spread, preferred_element_type=F32)

    def combine(t, carry):
        rows = pl.ds(pl.multiple_of(t * ATT_COMBINE_ROWS, ATT_COMBINE_ROWS), ATT_COMBINE_ROWS)
        stats = [lnat_ref[g, rows, :] for g in range(N_GROUPS)]
        mx = jnp.maximum(jnp.maximum(stats[0], stats[1]), stats[2])
        a = [jnp.exp2(st - mx) for st in stats]
        sums = [pltpu.roll(st, LSE_LANES - STAT_LANES, 1) for st in stats]
        inv = 1.0 / (a[0] * sums[0] + a[1] * sums[1] + a[2] * sums[2])
        o = None
        for g in range(N_GROUPS):
            acc = jnp.concatenate([onat_ref[2 * g, rows, :], onat_ref[2 * g + 1, rows, :]], axis=1)
            term = expand(a[g] * inv) * acc
            o = term if o is None else o + term
        o_ref[0, rows, :] = o.astype(BF16)
        return carry

    lax.fori_loop(0, ATT_TILE // ATT_COMBINE_ROWS, combine, 0)


def _attention(qkv, biases, weights, layer):
    B = qkv[0].shape[0]
    S = qkv[0].shape[2]
    tiles = S // ATT_TILE
    steps = B * tiles

    def w_slice(w, lead):
        assert w.shape[1] % (steps * BF16_ROWS) == 0
        return pl.BlockSpec((1, w.shape[1] // steps, w.shape[2]), lambda b, i: (lead, b * tiles + i, 0))
    in_specs = []
    for d in DILATIONS:
        tq = ATT_TILE // d
        nb = tq // RADIUS
        last_b = S // d // RADIUS - 1
        in_specs += [
            pl.BlockSpec((1, d, tq, QKV_WIDTH), lambda b, i: (b, 0, i, 0)),
            pl.BlockSpec((1, d, RADIUS, QKV_WIDTH),
                         lambda b, i, nb=nb: (b, 0, jnp.maximum(i * nb - 1, 0), 0)),
            pl.BlockSpec((1, d, RADIUS, QKV_WIDTH),
                         lambda b, i, nb=nb, last_b=last_b: (b, 0, jnp.minimum((i + 1) * nb, last_b), 0)),
        ]
    in_specs += [pl.BlockSpec((HEADS * SUBQ, SUBK), lambda b, i: (0, 0))] * N_GROUPS
    in_specs += [w_slice(w, layer) for w in weights]
    args = []
    for a in qkv:
        args += [a, a, a]
    return pl.pallas_call(
        _att_kernel,
        grid=(B, tiles),
        in_specs=in_specs,
        out_specs=[pl.BlockSpec((1, ATT_TILE, ATT_OUT), lambda b, i: (b, i, 0))]
        + [w_slice(w, 0) for w in weights],
        out_shape=[jax.ShapeDtypeStruct((B, S, ATT_OUT), BF16)]
        + [jax.ShapeDtypeStruct((1,) + w.shape[1:], BF16) for w in weights],
        scratch_shapes=[pltpu.VMEM((N_GROUPS * (ATT_OUT // LANES), ATT_TILE, LANES), F32),
                        pltpu.VMEM((N_GROUPS, ATT_TILE, LSE_LANES), F32),
                        pltpu.VMEM((2 * N_GROUPS, HEADS * SUBQ, SUBK), F32)],
        compiler_params=pltpu.CompilerParams(
            dimension_semantics=("arbitrary", "arbitrary"), vmem_limit_bytes=VMEM_LIMIT),
        name="dilated_att",
    )(*args, *biases, *weights)


def _t5_bucket_np(rel):
    nb = NUM_BUCKETS // 2
    max_exact = nb // 2
    ret = np.where(rel > 0, nb, 0)
    n = np.abs(rel)
    nf = np.maximum(n, 1).astype(np.float32)
    ratio = np.log(nf / np.float32(max_exact)) / np.float32(math.log(MAX_DISTANCE / max_exact))
    large = max_exact + (ratio * np.float32(nb - max_exact)).astype(np.int32)
    large = np.minimum(large, nb - 1)
    return ret + np.where(n < max_exact, n, large)


def _band_bias(rel_bias, g, dilation):
    period = SUBQ + SUBK
    nband = 2 * RADIUS + 1
    bucket = _t5_bucket_np((np.arange(nband) - RADIUS) * dilation)
    onehot = np.zeros((period, NUM_BUCKETS), np.float32)
    onehot[np.arange(nband), bucket] = 1.0
    tab = rel_bias[:, g * HEADS:(g + 1) * HEADS].astype(F32)
    t = jnp.dot(jnp.asarray(onehot), tab, precision=lax.Precision.HIGHEST)
    t = jnp.where((np.arange(period) < nband)[:, None], t, NEG_INF).T
    skew = jnp.tile(t, (1, SUBQ))[:, :SUBQ * (period - 1)].reshape(HEADS, SUBQ, period - 1)
    return skew[:, :, :SUBK].reshape(HEADS * SUBQ, SUBK) * LOG2E


def _mem_kv_kernel(mem_ref, g_ref, w_ref, kv_ref):
    h = _rms(mem_ref[0], g_ref[...]).astype(BF16)
    kv_ref[0] = jnp.dot(h, w_ref[0], preferred_element_type=F32).astype(BF16)


def _mem_kv(mem, g, w, layer):
    B, M, _ = mem.shape
    return pl.pallas_call(
        _mem_kv_kernel,
        grid=(B,),
        in_specs=[pl.BlockSpec((1, M, D_MODEL), lambda b: (b, 0, 0)),
                  pl.BlockSpec((1, D_MODEL), lambda b: (0, 0)),
                  _layer_weight(w, layer)],
        out_specs=pl.BlockSpec((1, M, 2 * MEM_WIDTH), lambda b: (b, 0, 0)),
        out_shape=jax.ShapeDtypeStruct((B, M, 2 * MEM_WIDTH), BF16),
        compiler_params=pltpu.CompilerParams(
            dimension_semantics=("parallel",), vmem_limit_bytes=VMEM_LIMIT),
        name="mem_kv",
    )(mem, g, w)


MIX_TM = 512
FFN_OUT_CAST_ROWS = 176


def _mix_kernel(x_ref, c_ref, qm_ref, kv_ref, oatt_ref, gate_ref, wc_ref, wa_ref, wm_ref, wo_ref,
                gpost_ref, wfi_ref, wfo_ref, out_ref, wfi_out_ref, wfo_out_ref):
    wfi_out_ref[...] = wfi_ref[...].astype(BF16)

    @pl.when(pl.program_id(0) < FFN_HIDDEN // FFN_OUT_CAST_ROWS)
    def _():
        wfo_out_ref[...] = wfo_ref[...].astype(BF16)

    scale = MEM_HEAD_DIM ** -0.5
    heads = []
    for h in range(MEM_HEADS):
        lo = h * MEM_HEAD_DIM
        qh = qm_ref[:, lo:lo + MEM_HEAD_DIM]
        kh = kv_ref[0, :, lo:lo + MEM_HEAD_DIM]
        vh = kv_ref[0, :, MEM_WIDTH + lo:MEM_WIDTH + lo + MEM_HEAD_DIM]
        s = lax.dot_general(qh, kh, (((1,), (1,)), ((), ())), preferred_element_type=F32) * scale
        m = jnp.max(s, axis=-1, keepdims=True)
        e = jnp.exp(s - m)
        p = (e * (1.0 / jnp.sum(e, axis=-1, keepdims=True))).astype(BF16)
        heads.append(jnp.dot(p, vh, preferred_element_type=F32).astype(BF16))
    o_mem = jnp.concatenate(heads, axis=-1)
    y_mem = jnp.dot(o_mem, wm_ref[0], preferred_element_type=F32)
    y_conv = jnp.dot(c_ref[...], wc_ref[0], preferred_element_type=F32)
    y_att = jnp.dot(oatt_ref[...], wa_ref[0], preferred_element_type=F32)
    merged = (gate_ref[:, 0:D_MODEL].astype(F32) * y_conv
              + gate_ref[:, D_MODEL:2 * D_MODEL].astype(F32) * y_att
              + gate_ref[:, 2 * D_MODEL:].astype(F32) * y_mem)
    y = jnp.dot(merged.astype(BF16), wo_ref[0], preferred_element_type=F32)
    out_ref[...] = x_ref[...] + _rms(y, gpost_ref[...])


def _mix(x2, c2, qm, kv, o_att, gates, wc, wa, wm, wo, layer, gpost, w_ffn_in, w_ffn_out, ffn_layer,
         tiles_per_batch):
    T = x2.shape[0]
    M = kv.shape[1]
    steps = T // MIX_TM
    in_rows = D_MODEL // steps
    out_steps = FFN_HIDDEN // FFN_OUT_CAST_ROWS
    assert in_rows % BF16_ROWS == 0 and FFN_OUT_CAST_ROWS % BF16_ROWS == 0 and out_steps <= steps

    def w_slice(rows_per_step, width, lead, nsteps):
        return pl.BlockSpec((1, rows_per_step, width), lambda i: (lead, jnp.minimum(i, nsteps - 1), 0))

    def rows(width):
        return pl.BlockSpec((MIX_TM, width), lambda i: (i, 0))

    def whole(shape):
        return pl.BlockSpec(shape, lambda i: (0,) * len(shape))

    return pl.pallas_call(
        _mix_kernel,
        grid=(T // MIX_TM,),
        in_specs=[rows(D_MODEL), rows(CONV_WIDTH), rows(MEM_WIDTH),
                  pl.BlockSpec((1, M, 2 * MEM_WIDTH), lambda i: (i // tiles_per_batch, 0, 0)),
                  rows(ATT_OUT), rows(GATE_WIDTH),
                  _layer_weight(wc, layer), _layer_weight(wa, layer),
                  _layer_weight(wm, layer), _layer_weight(wo, layer), whole((1, D_MODEL)),
                  w_slice(in_rows, 2 * FFN_HIDDEN, ffn_layer, steps),
                  w_slice(FFN_OUT_CAST_ROWS, D_MODEL, ffn_layer, out_steps)],
        out_specs=[rows(D_MODEL),
                   w_slice(in_rows, 2 * FFN_HIDDEN, 0, steps),
                   w_slice(FFN_OUT_CAST_ROWS, D_MODEL, 0, out_steps)],
        out_shape=[jax.ShapeDtypeStruct((T, D_MODEL), F32),
                   jax.ShapeDtypeStruct((1, D_MODEL, 2 * FFN_HIDDEN), BF16),
                   jax.ShapeDtypeStruct((1, FFN_HIDDEN, D_MODEL), BF16)],
        compiler_params=pltpu.CompilerParams(
            dimension_semantics=("arbitrary",), vmem_limit_bytes=VMEM_LIMIT),
        name="mix_out",
    )(x2, c2, qm, kv, o_att, gates, wc, wa, wm, wo, gpost, w_ffn_in, w_ffn_out)


FFN_TM = 1024
FFN_TH = 256
FFN_CHUNKS = FFN_HIDDEN // FFN_TH


def _ffn_kernel(x_ref, gpre_ref, win_ref, wout_ref, gpost_ref, *refs):
    out_ref = refs[0] if len(refs) == 1 else refs[1]
    if len(refs) == 3:
        refs[2][...] = refs[0][...].astype(BF16)
    x = x_ref[...]
    h = _rms(x, gpre_ref[...]).astype(BF16)
    acc = None
    for c in range(FFN_CHUNKS):
        lo = c * FFN_TH
        gv = jnp.dot(h, win_ref[0, :, lo:lo + FFN_TH], preferred_element_type=F32)
        uv = jnp.dot(h, win_ref[0, :, FFN_HIDDEN + lo:FFN_HIDDEN + lo + FFN_TH], preferred_element_type=F32)
        a = (gv * _sigmoid(gv) * uv).astype(BF16)
        part = jnp.dot(a, wout_ref[0, lo:lo + FFN_TH, :], preferred_element_type=F32)
        acc = part if acc is None else acc + part
    out_ref[...] = x + _rms(acc, gpost_ref[...])


def _ffn(x2, gpre, w_in, w_out, layer, gpost, next_w=None):
    T = x2.shape[0]
    steps = T // FFN_TM
    cast_in, cast_out, cast_shape, cast_args = [], [], [], []
    if next_w is not None:
        w, lead = next_w
        assert w.shape[1] % (steps * BF16_ROWS) == 0
        cast_in = [pl.BlockSpec((1, w.shape[1] // steps, w.shape[2]), lambda i: (lead, i, 0))]
        cast_out = [pl.BlockSpec((1, w.shape[1] // steps, w.shape[2]), lambda i: (0, i, 0))]
        cast_shape = [jax.ShapeDtypeStruct((1,) + w.shape[1:], BF16)]
        cast_args = [w]

    def resident(shape):
        return pl.BlockSpec(shape, lambda i: (0, 0), pipeline_mode=pl.Buffered(1))

    outs = pl.pallas_call(
        _ffn_kernel,
        grid=(steps,),
        in_specs=[pl.BlockSpec((FFN_TM, D_MODEL), lambda i: (i, 0)),
                  resident((1, D_MODEL)),
                  _layer_weight(w_in, layer), _layer_weight(w_out, layer),
                  resident((1, D_MODEL))] + cast_in,
        out_specs=[pl.BlockSpec((FFN_TM, D_MODEL), lambda i: (i, 0))] + cast_out,
        out_shape=[jax.ShapeDtypeStruct((T, D_MODEL), F32)] + cast_shape,
        compiler_params=pltpu.CompilerParams(
            dimension_semantics=("arbitrary",), vmem_limit_bytes=VMEM_LIMIT),
        name="ffn",
    )(x2, gpre, w_in, w_out, gpost, *cast_args)
    return outs[0], (outs[1] if next_w is not None else None)


def kernel(x, mem, rel_bias, norm_mix_pre, w_in, b_gate, conv_dw, conv_dw_bias, conv_ln_g, conv_ln_b,
           w_conv_out, w_att_out, norm_mem, w_mem_kv, w_mem_out, w_out, norm_mix_post, norm_ffn_pre,
           w_ffn_in, w_ffn_out, norm_ffn_post):
    B, S, D = x.shape
    depth = w_in.shape[0]
    T = B * S
    biases = [_band_bias(rel_bias, g, d) for g, d in enumerate(DILATIONS)]

    def row(v):
        return v.reshape(1, -1)

    w_in_bf16 = w_in[:1].astype(BF16)
    x2 = x.reshape(T, D)
    for l in range(depth):
        c, qm, qkv0, qkv1, qkv2, gates = _in_proj(
            x2, row(norm_mix_pre[l]), w_in_bf16, 0, row(b_gate[l]),
            conv_dw[l], row(conv_dw_bias[l]), row(conv_ln_g[l]), row(conv_ln_b[l]), B, S)
        o_att, wkv, wc, wa, wm, wo = _attention(
            (qkv0, qkv1, qkv2), biases, (w_mem_kv, w_conv_out, w_att_out, w_mem_out, w_out), l)
        kv = _mem_kv(mem, row(norm_mem[l]), wkv, 0)
        x2, wfi, wfo = _mix(x2, c, qm, kv, o_att.reshape(T, ATT_OUT), gates, wc, wa, wm, wo, 0,
                            row(norm_mix_post[l]), w_ffn_in, w_ffn_out, l, S // MIX_TM)
        next_w_in = (w_in, l + 1) if l + 1 < depth else None
        x2, w_in_bf16 = _ffn(x2, row(norm_ffn_pre[l]), wfi, wfo, 0, row(norm_ffn_post[l]), next_w_in)
    return x2.reshape(B, S, D)
```

```python
import functools
import math

import numpy as np
import jax
import jax.numpy as jnp
from jax import lax
from jax.experimental import pallas as pl
from jax.experimental.pallas import tpu as pltpu

F32 = jnp.float32
BF16 = jnp.bfloat16

D_MODEL = 1024
CONV_WIDTH = 512
CONV_KSIZE = 31
CONV_PAD = CONV_KSIZE // 2
DILATIONS = (1, 4, 16)
RADIUS = 64
N_GROUPS = 3
HEADS = 4
HEAD_DIM = 64
ATT_OUT = HEADS * HEAD_DIM
QKV_WIDTH = 3 * ATT_OUT
MEM_HEADS = 4
MEM_HEAD_DIM = 128
MEM_WIDTH = 512
FFN_HIDDEN = 2816
NUM_BUCKETS = 32
MAX_DISTANCE = 1024
RMS_EPS = 1e-6
LN_EPS = 1e-5
NEG_INF = -1e30

GATE_WIDTH = 3 * D_MODEL
SUBQ = 128
SUBK = SUBQ + 2 * RADIUS
LANES = 128
BF16_ROWS = 16
LSE_LANES = 128
LSE_PER_HEAD = LSE_LANES // HEADS
STAT_LANES = LSE_PER_HEAD // 2
LOG2E = math.log2(math.e)

V7X_VMEM_BYTES = 64 * 1024 * 1024
VMEM_LIMIT = V7X_VMEM_BYTES * 7 // 8


def _sigmoid(v):
    return 1.0 / (1.0 + jnp.exp(-v))


def _rms(v, g):
    return v * lax.rsqrt(jnp.mean(v * v, axis=-1, keepdims=True) + RMS_EPS) * g


def _derived_zero(v):
    bits = lax.bitcast_convert_type(v, jnp.uint32)
    bits = lax.shift_right_logical(lax.shift_right_logical(bits, jnp.uint32(16)), jnp.uint32(16))
    return lax.bitcast_convert_type(bits, F32)


def _layer_weight(w, layer):
    return pl.BlockSpec((1,) + w.shape[1:], lambda *_: (layer, 0, 0), pipeline_mode=pl.Buffered(1))


IN_TM = 512
IN_TN = 768
AG_WIDTH = 2 * CONV_WIDTH
QKV_COL = AG_WIDTH
QM_COL = QKV_COL + N_GROUPS * QKV_WIDTH
GATE_COL = QM_COL + MEM_WIDTH
N_GATE_TILES = GATE_WIDTH // IN_TN
assert IN_TN == QKV_WIDTH
DOT_TN = 256
SUBLANES = 8
CONV_HALO = 16
CONV_HALF = 256
CONV_RC = 16
CONV_FIRST = CONV_HALO - CONV_PAD
CONV_SHIFT_ROWS = CONV_HALF + (CONV_FIRST + CONV_KSIZE - 1) // SUBLANES * SUBLANES
IN_EXT = IN_TM + 2 * CONV_HALO


def _in_proj_kernel(x_ref, xp_ref, xn_ref, g_ref, w_ref, b_ref, cw_ref, cb_ref, lg_ref, lb_ref,
                    c_ref, qm_ref, q0_ref, q1_ref, q2_ref, gate_ref, acc_ref, u_ref, us_ref,
                    *, tiles_per_batch):
    ib = pl.program_id(0) % tiles_per_batch
    x_ext = jnp.concatenate([xp_ref[...], x_ref[...], xn_ref[...]], axis=0)
    h_ext = _rms(x_ext, g_ref[...]).astype(BF16)
    h = h_ext[CONV_HALO:CONV_HALO + IN_TM]

    for t in range(CONV_WIDTH // DOT_TN):
        cols = slice(t * DOT_TN, (t + 1) * DOT_TN)
        a = jnp.dot(h_ext, w_ref[0, :, cols], preferred_element_type=F32)
        gt = jnp.dot(h_ext, w_ref[0, :, CONV_WIDTH + t * DOT_TN:CONV_WIDTH + (t + 1) * DOT_TN],
                     preferred_element_type=F32)
        u = a * _sigmoid(gt)
        u_ref[0:CONV_HALO, cols] = jnp.where(ib > 0, u[0:CONV_HALO], 0.0)
        u_ref[CONV_HALO:CONV_HALO + IN_TM, cols] = u[CONV_HALO:CONV_HALO + IN_TM]
        u_ref[CONV_HALO + IN_TM:, cols] = jnp.where(ib < tiles_per_batch - 1, u[CONV_HALO + IN_TM:], 0.0)

    def shift_copies(half):
        for s in range(SUBLANES):
            us_ref[s] = u_ref[half * CONV_HALF + s:half * CONV_HALF + s + CONV_SHIFT_ROWS, :]

    def conv_chunk(half, c, zero):
        acc = jnp.concatenate([zero] * (CONV_RC // SUBLANES), axis=0)
        for k in range(CONV_KSIZE):
            off = CONV_FIRST + k
            r0 = c * CONV_RC + off // SUBLANES * SUBLANES
            acc = acc + us_ref[off % SUBLANES, r0:r0 + CONV_RC, :] * cw_ref[k:k + 1, :]
        y = acc + cb_ref[...]
        mu = jnp.mean(y, axis=-1, keepdims=True)
        yc = y - mu
        yn = yc * lax.rsqrt(jnp.mean(yc * yc, axis=-1, keepdims=True) + LN_EPS)
        yn = yn * lg_ref[...] + lb_ref[...]
        act = yn * _sigmoid(yn)
        rows = slice(half * CONV_HALF + c * CONV_RC, half * CONV_HALF + (c + 1) * CONV_RC)
        c_ref[rows, :] = act.astype(BF16)
        return _derived_zero(y[CONV_RC - SUBLANES:])

    def project(col, width):
        return jnp.dot(h, w_ref[0, :, col:col + width], preferred_element_type=F32)

    def mem_queries():
        for t in range(MEM_WIDTH // DOT_TN):
            acc = project(QM_COL + t * DOT_TN, DOT_TN)
            qm_ref[:, t * DOT_TN:(t + 1) * DOT_TN] = acc.astype(BF16)
        return acc

    q_scale = HEAD_DIM ** -0.5 * LOG2E

    def qkv_group(g):
        d, out_ref = DILATIONS[g], (q0_ref, q1_ref, q2_ref)[g]
        for which in range(3):
            acc = project(QKV_COL + (which * N_GROUPS + g) * ATT_OUT, ATT_OUT)
            if which == 0:
                acc = acc * q_scale
            cols = slice(which * ATT_OUT, (which + 1) * ATT_OUT)
            if d == 1:
                out_ref[0, 0, :, cols] = acc.astype(BF16)
                continue
            for half in range(ATT_OUT // LANES):
                acc_ref[2 * which + half] = acc[:, half * LANES:(half + 1) * LANES]
        if d > 1:
            for r in range(d):
                for cb in range(QKV_WIDTH // LANES):
                    out_ref[0, r, :, cb * LANES:(cb + 1) * LANES] = (
                        acc_ref[cb, pl.ds(r, IN_TM // d, stride=d), :].astype(BF16))
        return acc

    def gate_tile(t):
        for sub in range(IN_TN // DOT_TN):
            cols = slice(t * IN_TN + sub * DOT_TN, t * IN_TN + (sub + 1) * DOT_TN)
            acc = project(GATE_COL + cols.start, DOT_TN)
            gate_ref[:, cols] = _sigmoid(acc + b_ref[:, cols]).astype(BF16)
        return acc

    mxu_units = ([mem_queries] + [functools.partial(qkv_group, g) for g in range(N_GROUPS)]
                 + [functools.partial(gate_tile, t) for t in range(N_GATE_TILES)])
    chunks = [(half, c) for half in range(IN_TM // CONV_HALF) for c in range(CONV_HALF // CONV_RC)]
    chunks_per_unit = len(chunks) // len(mxu_units)
    assert chunks_per_unit * len(mxu_units) == len(chunks)
    zero = jnp.zeros((SUBLANES, CONV_WIDTH), F32)
    for step, unit in enumerate(mxu_units):
        for half, c in chunks[step * chunks_per_unit:(step + 1) * chunks_per_unit]:
            if c == 0:
                shift_copies(half)
            zero = conv_chunk(half, c, zero)
        acc = unit()
        tail = _derived_zero(acc[IN_TM - SUBLANES:, acc.shape[1] - LANES:])
        zero = zero + jnp.concatenate([tail] * (CONV_WIDTH // LANES), axis=1)


def _in_proj(x2, g, w, layer, b, conv_w, conv_b, ln_g, ln_b, batch, seq):
    T = x2.shape[0]
    tiles_per_batch = seq // IN_TM
    halo_per_tile = IN_TM // CONV_HALO
    last_halo = T // CONV_HALO - 1

    def qkv_spec(d):
        return pl.BlockSpec((1, d, IN_TM // d, QKV_WIDTH),
                            lambda i: (i // tiles_per_batch, 0, i % tiles_per_batch, 0))

    def resident(shape):
        return pl.BlockSpec(shape, lambda i: (0, 0), pipeline_mode=pl.Buffered(1))

    def rows(width):
        return pl.BlockSpec((IN_TM, width), lambda i: (i, 0))

    return pl.pallas_call(
        functools.partial(_in_proj_kernel, tiles_per_batch=tiles_per_batch),
        grid=(T // IN_TM,),
        in_specs=[
            rows(D_MODEL),
            pl.BlockSpec((CONV_HALO, D_MODEL), lambda i: (jnp.maximum(i * halo_per_tile - 1, 0), 0)),
            pl.BlockSpec((CONV_HALO, D_MODEL), lambda i: (jnp.minimum((i + 1) * halo_per_tile, last_halo), 0)),
            resident((1, D_MODEL)),
            _layer_weight(w, layer),
            resident((1, GATE_WIDTH)),
            resident((CONV_KSIZE, CONV_WIDTH)), resident((1, CONV_WIDTH)),
            resident((1, CONV_WIDTH)), resident((1, CONV_WIDTH)),
        ],
        out_specs=[rows(CONV_WIDTH), rows(MEM_WIDTH),
                   qkv_spec(DILATIONS[0]), qkv_spec(DILATIONS[1]), qkv_spec(DILATIONS[2]),
                   rows(GATE_WIDTH)],
        out_shape=[jax.ShapeDtypeStruct((T, CONV_WIDTH), BF16), jax.ShapeDtypeStruct((T, MEM_WIDTH), BF16)]
        + [jax.ShapeDtypeStruct((batch, d, seq // d, QKV_WIDTH), BF16) for d in DILATIONS]
        + [jax.ShapeDtypeStruct((T, GATE_WIDTH), BF16)],
        scratch_shapes=[pltpu.VMEM((QKV_WIDTH // LANES, IN_TM, LANES), F32),
                        pltpu.VMEM((IN_EXT, CONV_WIDTH), F32),
                        pltpu.VMEM((SUBLANES, CONV_SHIFT_ROWS, CONV_WIDTH), F32)],
        compiler_params=pltpu.CompilerParams(
            dimension_semantics=("parallel",), vmem_limit_bytes=VMEM_LIMIT),
        name="in_proj",
    )(x2, x2, x2, g, w, b, conv_w, conv_b, ln_g, ln_b)


ATT_TILE = 2048
ATT_COMBINE_ROWS = 1024


def _att_kernel(c0_ref, p0_ref, n0_ref, c1_ref, p1_ref, n1_ref, c2_ref, p2_ref, n2_ref,
                b0_ref, b1_ref, b2_ref, *refs):
    n_cast = (len(refs) - 4) // 2
    o_ref = refs[n_cast]
    onat_ref, lnat_ref, edge_ref = refs[2 * n_cast + 1:]
    for src_ref, dst_ref in zip(refs[:n_cast], refs[n_cast + 1:2 * n_cast + 1]):
        dst_ref[...] = src_ref[...].astype(BF16)
    i = pl.program_id(1)
    n = pl.num_programs(1)
    stat_of_lane = lax.broadcasted_iota(jnp.int32, (1, LSE_LANES), 1) // STAT_LANES
    key_col = lax.broadcasted_iota(jnp.int32, (1, SUBK), 1)
    K0, V0 = ATT_OUT, 2 * ATT_OUT
    even_head_lanes = lax.broadcasted_iota(jnp.int32, (1, LANES), 1) < HEAD_DIM

    before_start = jnp.where(key_col >= jnp.where(i == 0, RADIUS, 0), 0.0, NEG_INF)
    past_end = jnp.where(key_col < jnp.where(i == n - 1, SUBK - RADIUS, SUBK), 0.0, NEG_INF)
    for g, (d, b_ref) in enumerate(zip(DILATIONS, (b0_ref, b1_ref, b2_ref))):
        if ATT_TILE // d == SUBQ:
            edge_ref[2 * g] = b_ref[...] + before_start + past_end
        else:
            edge_ref[2 * g] = b_ref[...] + before_start
            edge_ref[2 * g + 1] = b_ref[...] + past_end

    def subtile(g, d, q, k, v, bias, row0):
        zeros = jnp.zeros((SUBQ, LANES), BF16)
        blocks = []
        for h in range(HEADS):
            tile = q[:, h // 2 * LANES:(h // 2 + 1) * LANES]
            tile = jnp.where(even_head_lanes, tile, zeros) if h % 2 == 0 else jnp.where(even_head_lanes, zeros, tile)
            blocks.append(jnp.concatenate([tile, zeros] if h < 2 else [zeros, tile], axis=1))
        qs = jnp.concatenate(blocks, axis=0)
        s = lax.dot_general(qs, k, (((1,), (1,)), ((), ())), preferred_element_type=F32) + bias
        m = jnp.max(s, axis=-1, keepdims=True)
        e = jnp.exp2(s - m)
        l = jnp.sum(e, axis=-1, keepdims=True)
        o_all = jnp.dot(e.astype(BF16), v, preferred_element_type=F32)
        o = jnp.concatenate(
            [jnp.where(even_head_lanes,
                       o_all[2 * t * SUBQ:(2 * t + 1) * SUBQ, t * LANES:(t + 1) * LANES],
                       o_all[(2 * t + 1) * SUBQ:(2 * t + 2) * SUBQ, t * LANES:(t + 1) * LANES])
             for t in range(ATT_OUT // LANES)], axis=1)
        stats = jnp.broadcast_to(l[(HEADS - 1) * SUBQ:], (SUBQ, LSE_LANES))
        stats = jnp.where(stat_of_lane == 2 * (HEADS - 1), m[(HEADS - 1) * SUBQ:], stats)
        for h in range(HEADS - 1):
            rows = slice(h * SUBQ, (h + 1) * SUBQ)
            stats = jnp.where(stat_of_lane == 2 * h, m[rows], stats)
            stats = jnp.where(stat_of_lane == 2 * h + 1, l[rows], stats)
        rows = pl.ds(row0, SUBQ) if d == 1 else pl.ds(row0, SUBQ, stride=d)
        for half in range(ATT_OUT // LANES):
            onat_ref[2 * g + half, rows, :] = o[:, half * LANES:(half + 1) * LANES]
        lnat_ref[g, rows, :] = stats

    groups = ((c0_ref, p0_ref, n0_ref, b0_ref), (c1_ref, p1_ref, n1_ref, b1_ref),
              (c2_ref, p2_ref, n2_ref, b2_ref))
    for g, (d, (c_ref, p_ref, n_ref, bias_ref)) in enumerate(zip(DILATIONS, groups)):
        tq = ATT_TILE // d
        nsub = tq // SUBQ
        for r in range(d):
            for j in range(nsub):
                parts = [(c_ref, slice(max(j * SUBQ - RADIUS, 0), min(j * SUBQ + SUBQ + RADIUS, tq)))]
                if j == 0:
                    parts = [(p_ref, slice(None))] + parts
                if j == nsub - 1:
                    parts = parts + [(n_ref, slice(None))]
                k, v = [jnp.concatenate([ref[0, r, rows, col:col + ATT_OUT] for ref, rows in parts], axis=0)
                        for col in (K0, V0)]
                q = c_ref[0, r, j * SUBQ:(j + 1) * SUBQ, 0:ATT_OUT]
                if j == 0:
                    bias = edge_ref[2 * g]
                elif j == nsub - 1:
                    bias = edge_ref[2 * g + 1]
                else:
                    bias = bias_ref[...]
                subtile(g, d, q, k, v, bias, j * SUBQ * d + r)

    w_lane = lax.broadcasted_iota(jnp.int32, (2 * LSE_LANES, ATT_OUT), 0) % LSE_LANES
    o_lane = lax.broadcasted_iota(jnp.int32, (2 * LSE_LANES, ATT_OUT), 1)
    spread = (w_lane == o_lane // HEAD_DIM * LSE_PER_HEAD).astype(BF16)
    weight_lane = lax.broadcasted_iota(jnp.int32, (1, LSE_LANES), 1) % LSE_PER_HEAD == 0

    def expand(w):
        w = jnp.where(weight_lane, w, 0.0)
        hi = w.astype(BF16)
        lo = (w - hi.astype(F32)).astype(BF16)
        return jnp.dot(jnp.concatenate([hi, lo], axis=1), spread, preferred_element_type=F32)

    def combine(t, carry):
        rows = pl.ds(pl.multiple_of(t * ATT_COMBINE_ROWS, ATT_COMBINE_ROWS), ATT_COMBINE_ROWS)
        stats = [lnat_ref[g, rows, :] for g in range(N_GROUPS)]
        mx = jnp.maximum(jnp.maximum(stats[0], stats[1]), stats[2])
        a = [jnp.exp2(st - mx) for st in stats]
        sums = [pltpu.roll(st, LSE_LANES - STAT_LANES, 1) for st in stats]
        inv = 1.0 / (a[0] * sums[0] + a[1] * sums[1] + a[2] * sums[2])
        o = None
        for g in range(N_GROUPS):
            acc = jnp.concatenate([onat_ref[2 * g, rows, :], onat_ref[2 * g + 1, rows, :]], axis=1)
            term = expand(a[g] * inv) * acc
            o = term if o is None else o + term
        o_ref[0, rows, :] = o.astype(BF16)
        return carry

    lax.fori_loop(0, ATT_TILE // ATT_COMBINE_ROWS, combine, 0)


def _attention(qkv, biases, weights, layer):
    B = qkv[0].shape[0]
    S = qkv[0].shape[2]
    tiles = S // ATT_TILE
    steps = B * tiles

    def w_slice(w, lead):
        assert w.shape[1] % (steps * BF16_ROWS) == 0
        return pl.BlockSpec((1, w.shape[1] // steps, w.shape[2]), lambda b, i: (lead, b * tiles + i, 0))
    in_specs = []
    for d in DILATIONS:
        tq = ATT_TILE // d
        nb = tq // RADIUS
        last_b = S // d // RADIUS - 1
        in_specs += [
            pl.BlockSpec((1, d, tq, QKV_WIDTH), lambda b, i: (b, 0, i, 0)),
            pl.BlockSpec((1, d, RADIUS, QKV_WIDTH),
                         lambda b, i, nb=nb: (b, 0, jnp.maximum(i * nb - 1, 0), 0)),
            pl.BlockSpec((1, d, RADIUS, QKV_WIDTH),
                         lambda b, i, nb=nb, last_b=last_b: (b, 0, jnp.minimum((i + 1) * nb, last_b), 0)),
        ]
    in_specs += [pl.BlockSpec((HEADS * SUBQ, SUBK), lambda b, i: (0, 0))] * N_GROUPS
    in_specs += [w_slice(w, layer) for w in weights]
    args = []
    for a in qkv:
        args += [a, a, a]
    return pl.pallas_call(
        _att_kernel,
        grid=(B, tiles),
        in_specs=in_specs,
        out_specs=[pl.BlockSpec((1, ATT_TILE, ATT_OUT), lambda b, i: (b, i, 0))]
        + [w_slice(w, 0) for w in weights],
        out_shape=[jax.ShapeDtypeStruct((B, S, ATT_OUT), BF16)]
        + [jax.ShapeDtypeStruct((1,) + w.shape[1:], BF16) for w in weights],
        scratch_shapes=[pltpu.VMEM((N_GROUPS * (ATT_OUT // LANES), ATT_TILE, LANES), F32),
                        pltpu.VMEM((N_GROUPS, ATT_TILE, LSE_LANES), F32),
                        pltpu.VMEM((2 * N_GROUPS, HEADS * SUBQ, SUBK), F32)],
        compiler_params=pltpu.CompilerParams(
            dimension_semantics=("arbitrary", "arbitrary"), vmem_limit_bytes=VMEM_LIMIT),
        name="dilated_att",
    )(*args, *biases, *weights)


def _t5_bucket_np(rel):
    nb = NUM_BUCKETS // 2
    max_exact = nb // 2
    ret = np.where(rel > 0, nb, 0)
    n = np.abs(rel)
    nf = np.maximum(n, 1).astype(np.float32)
    ratio = np.log(nf / np.float32(max_exact)) / np.float32(math.log(MAX_DISTANCE / max_exact))
    large = max_exact + (ratio * np.float32(nb - max_exact)).astype(np.int32)
    large = np.minimum(large, nb - 1)
    return ret + np.where(n < max_exact, n, large)


def _band_bias(rel_bias, g, dilation):
    period = SUBQ + SUBK
    nband = 2 * RADIUS + 1
    bucket = _t5_bucket_np((np.arange(nband) - RADIUS) * dilation)
    onehot = np.zeros((period, NUM_BUCKETS), np.float32)
    onehot[np.arange(nband), bucket] = 1.0
    tab = rel_bias[:, g * HEADS:(g + 1) * HEADS].astype(F32)
    t = jnp.dot(jnp.asarray(onehot), tab, precision=lax.Precision.HIGHEST)
    t = jnp.where((np.arange(period) < nband)[:, None], t, NEG_INF).T
    skew = jnp.tile(t, (1, SUBQ))[:, :SUBQ * (period - 1)].reshape(HEADS, SUBQ, period - 1)
    return skew[:, :, :SUBK].reshape(HEADS * SUBQ, SUBK) * LOG2E


def _mem_kv_kernel(mem_ref, g_ref, w_ref, kv_ref):
    h = _rms(mem_ref[0], g_ref[...]).astype(BF16)
    kv_ref[0] = jnp.dot(h, w_ref[0], preferred_element_type=F32).astype(BF16)


def _mem_kv(mem, g, w, layer):
    B, M, _ = mem.shape
    return pl.pallas_call(
        _mem_kv_kernel,
        grid=(B,),
        in_specs=[pl.BlockSpec((1, M, D_MODEL), lambda b: (b, 0, 0)),
                  pl.BlockSpec((1, D_MODEL), lambda b: (0, 0)),
                  _layer_weight(w, layer)],
        out_specs=pl.BlockSpec((1, M, 2 * MEM_WIDTH), lambda b: (b, 0, 0)),
        out_shape=jax.ShapeDtypeStruct((B, M, 2 * MEM_WIDTH), BF16),
        compiler_params=pltpu.CompilerParams(
            dimension_semantics=("parallel",), vmem_limit_bytes=VMEM_LIMIT),
        name="mem_kv",
    )(mem, g, w)


MIX_TM = 1024
FFN_OUT_CAST_ROWS = 176


def _mix_kernel(x_ref, c_ref, qm_ref, kv_ref, oatt_ref, gate_ref, wc_ref, wa_ref, wm_ref, wo_ref,
                gpost_ref, wfi_ref, wfo_ref, out_ref, wfi_out_ref, wfo_out_ref):
    wfi_out_ref[...] = wfi_ref[...].astype(BF16)
    wfo_out_ref[...] = wfo_ref[...].astype(BF16)

    scale = MEM_HEAD_DIM ** -0.5
    heads = []
    for h in range(MEM_HEADS):
        lo = h * MEM_HEAD_DIM
        qh = qm_ref[:, lo:lo + MEM_HEAD_DIM]
        kh = kv_ref[0, :, lo:lo + MEM_HEAD_DIM]
        vh = kv_ref[0, :, MEM_WIDTH + lo:MEM_WIDTH + lo + MEM_HEAD_DIM]
        s = lax.dot_general(qh, kh, (((1,), (1,)), ((), ())), preferred_element_type=F32) * scale
        m = jnp.max(s, axis=-1, keepdims=True)
        e = jnp.exp(s - m)
        p = (e * (1.0 / jnp.sum(e, axis=-1, keepdims=True))).astype(BF16)
        heads.append(jnp.dot(p, vh, preferred_element_type=F32).astype(BF16))
    o_mem = jnp.concatenate(heads, axis=-1)
    y_mem = jnp.dot(o_mem, wm_ref[0], preferred_element_type=F32)
    y_conv = jnp.dot(c_ref[...], wc_ref[0], preferred_element_type=F32)
    y_att = jnp.dot(oatt_ref[...], wa_ref[0], preferred_element_type=F32)
    merged = (gate_ref[:, 0:D_MODEL].astype(F32) * y_conv
              + gate_ref[:, D_MODEL:2 * D_MODEL].astype(F32) * y_att
              + gate_ref[:, 2 * D_MODEL:].astype(F32) * y_mem)
    y = jnp.dot(merged.astype(BF16), wo_ref[0], preferred_element_type=F32)
    out_ref[...] = x_ref[...] + _rms(y, gpost_ref[...])


def _mix(x2, c2, qm, kv, o_att, gates, wc, wa, wm, wo, layer, gpost, w_ffn_in, w_ffn_out, ffn_layer,
         tiles_per_batch):
    T = x2.shape[0]
    M = kv.shape[1]
    steps = T // MIX_TM
    in_rows = D_MODEL // steps
    out_steps = FFN_HIDDEN // FFN_OUT_CAST_ROWS
    assert in_rows % BF16_ROWS == 0 and FFN_OUT_CAST_ROWS % BF16_ROWS == 0 and out_steps <= steps

    def w_slice(rows_per_step, width, lead, nsteps):
        return pl.BlockSpec((1, rows_per_step, width), lambda i: (lead, jnp.minimum(i, nsteps - 1), 0))

    def rows(width):
        return pl.BlockSpec((MIX_TM, width), lambda i: (i, 0))

    def whole(shape):
        return pl.BlockSpec(shape, lambda i: (0,) * len(shape))

    return pl.pallas_call(
        _mix_kernel,
        grid=(T // MIX_TM,),
        in_specs=[rows(D_MODEL), rows(CONV_WIDTH), rows(MEM_WIDTH),
                  pl.BlockSpec((1, M, 2 * MEM_WIDTH), lambda i: (i // tiles_per_batch, 0, 0)),
                  rows(ATT_OUT), rows(GATE_WIDTH),
                  _layer_weight(wc, layer), _layer_weight(wa, layer),
                  _layer_weight(wm, layer), _layer_weight(wo, layer), whole((1, D_MODEL)),
                  w_slice(in_rows, 2 * FFN_HIDDEN, ffn_layer, steps),
                  w_slice(FFN_OUT_CAST_ROWS, D_MODEL, ffn_layer, out_steps)],
        out_specs=[rows(D_MODEL),
                   w_slice(in_rows, 2 * FFN_HIDDEN, 0, steps),
                   w_slice(FFN_OUT_CAST_ROWS, D_MODEL, 0, out_steps)],
        out_shape=[jax.ShapeDtypeStruct((T, D_MODEL), F32),
                   jax.ShapeDtypeStruct((1, D_MODEL, 2 * FFN_HIDDEN), BF16),
                   jax.ShapeDtypeStruct((1, FFN_HIDDEN, D_MODEL), BF16)],
        compiler_params=pltpu.CompilerParams(
            dimension_semantics=("arbitrary",), vmem_limit_bytes=VMEM_LIMIT),
        name="mix_out",
    )(x2, c2, qm, kv, o_att, gates, wc, wa, wm, wo, gpost, w_ffn_in, w_ffn_out)


FFN_TM = 1024
FFN_TH = 256
FFN_CHUNKS = FFN_HIDDEN // FFN_TH


def _ffn_kernel(x_ref, gpre_ref, win_ref, wout_ref, gpost_ref, *refs):
    out_ref = refs[0] if len(refs) == 1 else refs[1]
    if len(refs) == 3:
        refs[2][...] = refs[0][...].astype(BF16)
    x = x_ref[...]
    h = _rms(x, gpre_ref[...]).astype(BF16)
    acc = None
    for c in range(FFN_CHUNKS):
        lo = c * FFN_TH
        gv = jnp.dot(h, win_ref[0, :, lo:lo + FFN_TH], preferred_element_type=F32)
        uv = jnp.dot(h, win_ref[0, :, FFN_HIDDEN + lo:FFN_HIDDEN + lo + FFN_TH], preferred_element_type=F32)
        a = (gv * _sigmoid(gv) * uv).astype(BF16)
        part = jnp.dot(a, wout_ref[0, lo:lo + FFN_TH, :], preferred_element_type=F32)
        acc = part if acc is None else acc + part
    out_ref[...] = x + _rms(acc, gpost_ref[...])


def _ffn(x2, gpre, w_in, w_out, layer, gpost, next_w=None):
    T = x2.shape[0]
    steps = T // FFN_TM
    cast_in, cast_out, cast_shape, cast_args = [], [], [], []
    if next_w is not None:
        w, lead = next_w
        assert w.shape[1] % (steps * BF16_ROWS) == 0
        cast_in = [pl.BlockSpec((1, w.shape[1] // steps, w.shape[2]), lambda i: (lead, i, 0))]
        cast_out = [pl.BlockSpec((1, w.shape[1] // steps, w.shape[2]), lambda i: (0, i, 0))]
        cast_shape = [jax.ShapeDtypeStruct((1,) + w.shape[1:], BF16)]
        cast_args = [w]

    def resident(shape):
        return pl.BlockSpec(shape, lambda i: (0, 0), pipeline_mode=pl.Buffered(1))

    outs = pl.pallas_call(
        _ffn_kernel,
        grid=(steps,),
        in_specs=[pl.BlockSpec((FFN_TM, D_MODEL), lambda i: (i, 0)),
                  resident((1, D_MODEL)),
                  _layer_weight(w_in, layer), _layer_weight(w_out, layer),
                  resident((1, D_MODEL))] + cast_in,
        out_specs=[pl.BlockSpec((FFN_TM, D_MODEL), lambda i: (i, 0))] + cast_out,
        out_shape=[jax.ShapeDtypeStruct((T, D_MODEL), F32)] + cast_shape,
        compiler_params=pltpu.CompilerParams(
            dimension_semantics=("arbitrary",), vmem_limit_bytes=VMEM_LIMIT),
        name="ffn",
    )(x2, gpre, w_in, w_out, gpost, *cast_args)
    return outs[0], (outs[1] if next_w is not None else None)


def kernel(x, mem, rel_bias, norm_mix_pre, w_in, b_gate, conv_dw, conv_dw_bias, conv_ln_g, conv_ln_b,
           w_conv_out, w_att_out, norm_mem, w_mem_kv, w_mem_out, w_out, norm_mix_post, norm_ffn_pre,
           w_ffn_in, w_ffn_out, norm_ffn_post):
    B, S, D = x.shape
    depth = w_in.shape[0]
    T = B * S
    biases = [_band_bias(rel_bias, g, d) for g, d in enumerate(DILATIONS)]

    def row(v):
        return v.reshape(1, -1)

    w_in_bf16 = w_in[:1].astype(BF16)
    x2 = x.reshape(T, D)
    for l in range(depth):
        c, qm, qkv0, qkv1, qkv2, gates = _in_proj(
            x2, row(norm_mix_pre[l]), w_in_bf16, 0, row(b_gate[l]),
            conv_dw[l], row(conv_dw_bias[l]), row(conv_ln_g[l]), row(conv_ln_b[l]), B, S)
        o_att, wkv, wc, wa, wm, wo = _attention(
            (qkv0, qkv1, qkv2), biases, (w_mem_kv, w_conv_out, w_att_out, w_mem_out, w_out), l)
        kv = _mem_kv(mem, row(norm_mem[l]), wkv, 0)
        x2, wfi, wfo = _mix(x2, c, qm, kv, o_att.reshape(T, ATT_OUT), gates, wc, wa, wm, wo, 0,
                            row(norm_mix_post[l]), w_ffn_in, w_ffn_out, l, S // MIX_TM)
        next_w_in = (w_in, l + 1) if l + 1 < depth else None
        x2, w_in_bf16 = _ffn(x2, row(norm_ffn_pre[l]), wfi, wfo, 0, row(norm_ffn_post[l]), next_w_in)
    return x2.reshape(B, S, D)
```

```python
import functools
import math

import numpy as np
import jax
import jax.numpy as jnp
from jax import lax
from jax.experimental import pallas as pl
from jax.experimental.pallas import tpu as pltpu

F32 = jnp.float32
BF16 = jnp.bfloat16

D_MODEL = 1024
CONV_WIDTH = 512
CONV_KSIZE = 31
CONV_PAD = CONV_KSIZE // 2
DILATIONS = (1, 4, 16)
RADIUS = 64
N_GROUPS = 3
HEADS = 4
HEAD_DIM = 64
ATT_OUT = HEADS * HEAD_DIM
QKV_WIDTH = 3 * ATT_OUT
MEM_HEADS = 4
MEM_HEAD_DIM = 128
MEM_WIDTH = 512
FFN_HIDDEN = 2816
NUM_BUCKETS = 32
MAX_DISTANCE = 1024
RMS_EPS = 1e-6
LN_EPS = 1e-5
NEG_INF = -1e30

GATE_WIDTH = 3 * D_MODEL
SUBQ = 128
SUBK = SUBQ + 2 * RADIUS
LANES = 128
BF16_ROWS = 16
LSE_LANES = 128
LSE_PER_HEAD = LSE_LANES // HEADS
STAT_LANES = LSE_PER_HEAD // 2
LOG2E = math.log2(math.e)

V7X_VMEM_BYTES = 64 * 1024 * 1024
VMEM_LIMIT = V7X_VMEM_BYTES * 7 // 8


def _sigmoid(v):
    return 1.0 / (1.0 + jnp.exp(-v))


def _rms(v, g):
    return v * lax.rsqrt(jnp.mean(v * v, axis=-1, keepdims=True) + RMS_EPS) * g


def _derived_zero(v):
    bits = lax.bitcast_convert_type(v, jnp.uint32)
    bits = lax.shift_right_logical(lax.shift_right_logical(bits, jnp.uint32(16)), jnp.uint32(16))
    return lax.bitcast_convert_type(bits, F32)


def _layer_weight(w, layer):
    return pl.BlockSpec((1,) + w.shape[1:], lambda *_: (layer, 0, 0), pipeline_mode=pl.Buffered(1))


IN_TM = 512
IN_TN = 768
AG_WIDTH = 2 * CONV_WIDTH
QKV_COL = AG_WIDTH
QM_COL = QKV_COL + N_GROUPS * QKV_WIDTH
GATE_COL = QM_COL + MEM_WIDTH
N_GATE_TILES = GATE_WIDTH // IN_TN
assert IN_TN == QKV_WIDTH
DOT_TN = 256
SUBLANES = 8
CONV_HALO = 16
CONV_HALF = 256
CONV_RC = 16
CONV_FIRST = CONV_HALO - CONV_PAD
CONV_SHIFT_ROWS = CONV_HALF + (CONV_FIRST + CONV_KSIZE - 1) // SUBLANES * SUBLANES
IN_EXT = IN_TM + 2 * CONV_HALO


def _in_proj_kernel(x_ref, xp_ref, xn_ref, g_ref, w_ref, b_ref, cw_ref, cb_ref, lg_ref, lb_ref,
                    c_ref, qm_ref, q0_ref, q1_ref, q2_ref, gate_ref, acc_ref, u_ref, us_ref,
                    *, tiles_per_batch):
    ib = pl.program_id(0) % tiles_per_batch
    x_ext = jnp.concatenate([xp_ref[...], x_ref[...], xn_ref[...]], axis=0)
    h_ext = _rms(x_ext, g_ref[...]).astype(BF16)
    h = h_ext[CONV_HALO:CONV_HALO + IN_TM]

    for t in range(CONV_WIDTH // DOT_TN):
        cols = slice(t * DOT_TN, (t + 1) * DOT_TN)
        a = jnp.dot(h_ext, w_ref[0, :, cols], preferred_element_type=F32)
        gt = jnp.dot(h_ext, w_ref[0, :, CONV_WIDTH + t * DOT_TN:CONV_WIDTH + (t + 1) * DOT_TN],
                     preferred_element_type=F32)
        u = a * _sigmoid(gt)
        u_ref[0:CONV_HALO, cols] = jnp.where(ib > 0, u[0:CONV_HALO], 0.0)
        u_ref[CONV_HALO:CONV_HALO + IN_TM, cols] = u[CONV_HALO:CONV_HALO + IN_TM]
        u_ref[CONV_HALO + IN_TM:, cols] = jnp.where(ib < tiles_per_batch - 1, u[CONV_HALO + IN_TM:], 0.0)

    def shift_copies(half):
        for s in range(SUBLANES):
            us_ref[s] = u_ref[half * CONV_HALF + s:half * CONV_HALF + s + CONV_SHIFT_ROWS, :]

    def conv_chunk(half, c, zero):
        acc = jnp.concatenate([zero] * (CONV_RC // SUBLANES), axis=0)
        for k in range(CONV_KSIZE):
            off = CONV_FIRST + k
            r0 = c * CONV_RC + off // SUBLANES * SUBLANES
            acc = acc + us_ref[off % SUBLANES, r0:r0 + CONV_RC, :] * cw_ref[k:k + 1, :]
        y = acc + cb_ref[...]
        mu = jnp.mean(y, axis=-1, keepdims=True)
        yc = y - mu
        yn = yc * lax.rsqrt(jnp.mean(yc * yc, axis=-1, keepdims=True) + LN_EPS)
        yn = yn * lg_ref[...] + lb_ref[...]
        act = yn * _sigmoid(yn)
        rows = slice(half * CONV_HALF + c * CONV_RC, half * CONV_HALF + (c + 1) * CONV_RC)
        c_ref[rows, :] = act.astype(BF16)
        return _derived_zero(y[CONV_RC - SUBLANES:])

    def project(col, width):
        return jnp.dot(h, w_ref[0, :, col:col + width], preferred_element_type=F32)

    def mem_queries():
        for t in range(MEM_WIDTH // DOT_TN):
            acc = project(QM_COL + t * DOT_TN, DOT_TN)
            qm_ref[:, t * DOT_TN:(t + 1) * DOT_TN] = acc.astype(BF16)
        return acc

    q_scale = HEAD_DIM ** -0.5 * LOG2E

    def qkv_group(g):
        d, out_ref = DILATIONS[g], (q0_ref, q1_ref, q2_ref)[g]
        for which in range(3):
            acc = project(QKV_COL + (which * N_GROUPS + g) * ATT_OUT, ATT_OUT)
            if which == 0:
                acc = acc * q_scale
            cols = slice(which * ATT_OUT, (which + 1) * ATT_OUT)
            if d == 1:
                out_ref[0, 0, :, cols] = acc.astype(BF16)
                continue
            for half in range(ATT_OUT // LANES):
                acc_ref[2 * which + half] = acc[:, half * LANES:(half + 1) * LANES]
        if d > 1:
            for r in range(d):
                for cb in range(QKV_WIDTH // LANES):
                    out_ref[0, r, :, cb * LANES:(cb + 1) * LANES] = (
                        acc_ref[cb, pl.ds(r, IN_TM // d, stride=d), :].astype(BF16))
        return acc

    def gate_tile(t):
        for sub in range(IN_TN // DOT_TN):
            cols = slice(t * IN_TN + sub * DOT_TN, t * IN_TN + (sub + 1) * DOT_TN)
            acc = project(GATE_COL + cols.start, DOT_TN)
            gate_ref[:, cols] = _sigmoid(acc + b_ref[:, cols]).astype(BF16)
        return acc

    mxu_units = ([mem_queries] + [functools.partial(qkv_group, g) for g in range(N_GROUPS)]
                 + [functools.partial(gate_tile, t) for t in range(N_GATE_TILES)])
    chunks = [(half, c) for half in range(IN_TM // CONV_HALF) for c in range(CONV_HALF // CONV_RC)]
    chunks_per_unit = len(chunks) // len(mxu_units)
    assert chunks_per_unit * len(mxu_units) == len(chunks)
    zero = jnp.zeros((SUBLANES, CONV_WIDTH), F32)
    for step, unit in enumerate(mxu_units):
        for half, c in chunks[step * chunks_per_unit:(step + 1) * chunks_per_unit]:
            if c == 0:
                shift_copies(half)
            zero = conv_chunk(half, c, zero)
        acc = unit()
        tail = _derived_zero(acc[IN_TM - SUBLANES:, acc.shape[1] - LANES:])
        zero = zero + jnp.concatenate([tail] * (CONV_WIDTH // LANES), axis=1)


def _in_proj(x2, g, w, layer, b, conv_w, conv_b, ln_g, ln_b, batch, seq):
    T = x2.shape[0]
    tiles_per_batch = seq // IN_TM
    halo_per_tile = IN_TM // CONV_HALO
    last_halo = T // CONV_HALO - 1

    def qkv_spec(d):
        return pl.BlockSpec((1, d, IN_TM // d, QKV_WIDTH),
                            lambda i: (i // tiles_per_batch, 0, i % tiles_per_batch, 0))

    def resident(shape):
        return pl.BlockSpec(shape, lambda i: (0, 0), pipeline_mode=pl.Buffered(1))

    def rows(width):
        return pl.BlockSpec((IN_TM, width), lambda i: (i, 0))

    return pl.pallas_call(
        functools.partial(_in_proj_kernel, tiles_per_batch=tiles_per_batch),
        grid=(T // IN_TM,),
        in_specs=[
            rows(D_MODEL),
            pl.BlockSpec((CONV_HALO, D_MODEL), lambda i: (jnp.maximum(i * halo_per_tile - 1, 0), 0)),
            pl.BlockSpec((CONV_HALO, D_MODEL), lambda i: (jnp.minimum((i + 1) * halo_per_tile, last_halo), 0)),
            resident((1, D_MODEL)),
            _layer_weight(w, layer),
            resident((1, GATE_WIDTH)),
            resident((CONV_KSIZE, CONV_WIDTH)), resident((1, CONV_WIDTH)),
            resident((1, CONV_WIDTH)), resident((1, CONV_WIDTH)),
        ],
        out_specs=[rows(CONV_WIDTH), rows(MEM_WIDTH),
                   qkv_spec(DILATIONS[0]), qkv_spec(DILATIONS[1]), qkv_spec(DILATIONS[2]),
                   rows(GATE_WIDTH)],
        out_shape=[jax.ShapeDtypeStruct((T, CONV_WIDTH), BF16), jax.ShapeDtypeStruct((T, MEM_WIDTH), BF16)]
        + [jax.ShapeDtypeStruct((batch, d, seq // d, QKV_WIDTH), BF16) for d in DILATIONS]
        + [jax.ShapeDtypeStruct((T, GATE_WIDTH), BF16)],
        scratch_shapes=[pltpu.VMEM((QKV_WIDTH // LANES, IN_TM, LANES), F32),
                        pltpu.VMEM((IN_EXT, CONV_WIDTH), F32),
                        pltpu.VMEM((SUBLANES, CONV_SHIFT_ROWS, CONV_WIDTH), F32)],
        compiler_params=pltpu.CompilerParams(
            dimension_semantics=("parallel",), vmem_limit_bytes=VMEM_LIMIT),
        name="in_proj",
    )(x2, x2, x2, g, w, b, conv_w, conv_b, ln_g, ln_b)


ATT_TILE = 2048
ATT_COMBINE_ROWS = 1024


def _att_kernel(c0_ref, p0_ref, n0_ref, c1_ref, p1_ref, n1_ref, c2_ref, p2_ref, n2_ref,
                b0_ref, b1_ref, b2_ref, *refs):
    n_cast = (len(refs) - 4) // 2
    o_ref = refs[n_cast]
    onat_ref, lnat_ref, edge_ref = refs[2 * n_cast + 1:]
    for src_ref, dst_ref in zip(refs[:n_cast], refs[n_cast + 1:2 * n_cast + 1]):
        dst_ref[...] = src_ref[...].astype(BF16)
    i = pl.program_id(1)
    n = pl.num_programs(1)
    stat_of_lane = lax.broadcasted_iota(jnp.int32, (1, LSE_LANES), 1) // STAT_LANES
    key_col = lax.broadcasted_iota(jnp.int32, (1, SUBK), 1)
    K0, V0 = ATT_OUT, 2 * ATT_OUT
    even_head_lanes = lax.broadcasted_iota(jnp.int32, (1, LANES), 1) < HEAD_DIM

    before_start = jnp.where(key_col >= jnp.where(i == 0, RADIUS, 0), 0.0, NEG_INF)
    past_end = jnp.where(key_col < jnp.where(i == n - 1, SUBK - RADIUS, SUBK), 0.0, NEG_INF)
    for g, (d, b_ref) in enumerate(zip(DILATIONS, (b0_ref, b1_ref, b2_ref))):
        if ATT_TILE // d == SUBQ:
            edge_ref[2 * g] = b_ref[...] + before_start + past_end
        else:
            edge_ref[2 * g] = b_ref[...] + before_start
            edge_ref[2 * g + 1] = b_ref[...] + past_end

    def subtile(g, d, q, k, v, bias, row0):
        zeros = jnp.zeros((SUBQ, LANES), BF16)
        blocks = []
        for h in range(HEADS):
            tile = q[:, h // 2 * LANES:(h // 2 + 1) * LANES]
            tile = jnp.where(even_head_lanes, tile, zeros) if h % 2 == 0 else jnp.where(even_head_lanes, zeros, tile)
            blocks.append(jnp.concatenate([tile, zeros] if h < 2 else [zeros, tile], axis=1))
        qs = jnp.concatenate(blocks, axis=0)
        s = lax.dot_general(qs, k, (((1,), (1,)), ((), ())), preferred_element_type=F32) + bias
        m = jnp.max(s, axis=-1, keepdims=True)
        e = jnp.exp2(s - m)
        l = jnp.sum(e, axis=-1, keepdims=True)
        o_all = jnp.dot(e.astype(BF16), v, preferred_element_type=F32)
        o = jnp.concatenate(
            [jnp.where(even_head_lanes,
                       o_all[2 * t * SUBQ:(2 * t + 1) * SUBQ, t * LANES:(t + 1) * LANES],
                       o_all[(2 * t + 1) * SUBQ:(2 * t + 2) * SUBQ, t * LANES:(t + 1) * LANES])
             for t in range(ATT_OUT // LANES)], axis=1)
        stats = jnp.broadcast_to(l[(HEADS - 1) * SUBQ:], (SUBQ, LSE_LANES))
        stats = jnp.where(stat_of_lane == 2 * (HEADS - 1), m[(HEADS - 1) * SUBQ:], stats)
        for h in range(HEADS - 1):
            rows = slice(h * SUBQ, (h + 1) * SUBQ)
            stats = jnp.where(stat_of_lane == 2 * h, m[rows], stats)
            stats = jnp.where(stat_of_lane == 2 * h + 1, l[rows], stats)
        rows = pl.ds(row0, SUBQ) if d == 1 else pl.ds(row0, SUBQ, stride=d)
        for half in range(ATT_OUT // LANES):
            onat_ref[2 * g + half, rows, :] = o[:, half * LANES:(half + 1) * LANES]
        lnat_ref[g, rows, :] = stats

    groups = ((c0_ref, p0_ref, n0_ref, b0_ref), (c1_ref, p1_ref, n1_ref, b1_ref),
              (c2_ref, p2_ref, n2_ref, b2_ref))
    for g, (d, (c_ref, p_ref, n_ref, bias_ref)) in enumerate(zip(DILATIONS, groups)):
        tq = ATT_TILE // d
        nsub = tq // SUBQ
        for r in range(d):
            for j in range(nsub):
                parts = [(c_ref, slice(max(j * SUBQ - RADIUS, 0), min(j * SUBQ + SUBQ + RADIUS, tq)))]
                if j == 0:
                    parts = [(p_ref, slice(None))] + parts
                if j == nsub - 1:
                    parts = parts + [(n_ref, slice(None))]
                k, v = [jnp.concatenate([ref[0, r, rows, col:col + ATT_OUT] for ref, rows in parts], axis=0)
                        for col in (K0, V0)]
                q = c_ref[0, r, j * SUBQ:(j + 1) * SUBQ, 0:ATT_OUT]
                if j == 0:
                    bias = edge_ref[2 * g]
                elif j == nsub - 1:
                    bias = edge_ref[2 * g + 1]
                else:
                    bias = bias_ref[...]
                subtile(g, d, q, k, v, bias, j * SUBQ * d + r)

    w_lane = lax.broadcasted_iota(jnp.int32, (2 * LSE_LANES, ATT_OUT), 0) % LSE_LANES
    o_lane = lax.broadcasted_iota(jnp.int32, (2 * LSE_LANES, ATT_OUT), 1)
    spread = (w_lane == o_lane // HEAD_DIM * LSE_PER_HEAD).astype(BF16)
    weight_lane = lax.broadcasted_iota(jnp.int32, (1, LSE_LANES), 1) % LSE_PER_HEAD == 0

    def expand(w):
        w = jnp.where(weight_lane, w, 0.0)
        hi = w.astype(BF16)
        lo = (w - hi.astype(F32)).astype(BF16)
        return jnp.dot(jnp.concatenate([hi, lo], axis=1), spread, preferred_element_type=F32)

    def combine(t, carry):
        rows = pl.ds(pl.multiple_of(t * ATT_COMBINE_ROWS, ATT_COMBINE_ROWS), ATT_COMBINE_ROWS)
        stats = [lnat_ref[g, rows, :] for g in range(N_GROUPS)]
        mx = jnp.maximum(jnp.maximum(stats[0], stats[1]), stats[2])
        a = [jnp.exp2(st - mx) for st in stats]
        sums = [pltpu.roll(st, LSE_LANES - STAT_LANES, 1) for st in stats]
        inv = 1.0 / (a[0] * sums[0] + a[1] * sums[1] + a[2] * sums[2])
        o = None
        for g in range(N_GROUPS):
            acc = jnp.concatenate([onat_ref[2 * g, rows, :], onat_ref[2 * g + 1, rows, :]], axis=1)
            term = expand(a[g] * inv) * acc
            o = term if o is None else o + term
        o_ref[0, rows, :] = o.astype(BF16)
        return carry

    lax.fori_loop(0, ATT_TILE // ATT_COMBINE_ROWS, combine, 0)


def _attention(qkv, biases, weights, layer):
    B = qkv[0].shape[0]
    S = qkv[0].shape[2]
    tiles = S // ATT_TILE
    steps = B * tiles

    def w_slice(w, lead):
        assert w.shape[1] % (steps * BF16_ROWS) == 0
        return pl.BlockSpec((1, w.shape[1] // steps, w.shape[2]), lambda b, i: (lead, b * tiles + i, 0))
    in_specs = []
    for d in DILATIONS:
        tq = ATT_TILE // d
        nb = tq // RADIUS
        last_b = S // d // RADIUS - 1
        in_specs += [
            pl.BlockSpec((1, d, tq, QKV_WIDTH), lambda b, i: (b, 0, i, 0)),
            pl.BlockSpec((1, d, RADIUS, QKV_WIDTH),
                         lambda b, i, nb=nb: (b, 0, jnp.maximum(i * nb - 1, 0), 0)),
            pl.BlockSpec((1, d, RADIUS, QKV_WIDTH),
                         lambda b, i, nb=nb, last_b=last_b: (b, 0, jnp.minimum((i + 1) * nb, last_b), 0)),
        ]
    in_specs += [pl.BlockSpec((None, HEADS * SUBQ, SUBK), lambda b, i, g=g: (g, 0, 0)) for g in range(N_GROUPS)]
    in_specs += [w_slice(w, layer) for w in weights]
    args = []
    for a in qkv:
        args += [a, a, a]
    return pl.pallas_call(
        _att_kernel,
        grid=(B, tiles),
        in_specs=in_specs,
        out_specs=[pl.BlockSpec((1, ATT_TILE, ATT_OUT), lambda b, i: (b, i, 0))]
        + [w_slice(w, 0) for w in weights],
        out_shape=[jax.ShapeDtypeStruct((B, S, ATT_OUT), BF16)]
        + [jax.ShapeDtypeStruct((1,) + w.shape[1:], BF16) for w in weights],
        scratch_shapes=[pltpu.VMEM((N_GROUPS * (ATT_OUT // LANES), ATT_TILE, LANES), F32),
                        pltpu.VMEM((N_GROUPS, ATT_TILE, LSE_LANES), F32),
                        pltpu.VMEM((2 * N_GROUPS, HEADS * SUBQ, SUBK), F32)],
        compiler_params=pltpu.CompilerParams(
            dimension_semantics=("arbitrary", "arbitrary"), vmem_limit_bytes=VMEM_LIMIT),
        name="dilated_att",
    )(*args, *([biases] * N_GROUPS), *weights)


def _t5_bucket_np(rel):
    nb = NUM_BUCKETS // 2
    max_exact = nb // 2
    ret = np.where(rel > 0, nb, 0)
    n = np.abs(rel)
    nf = np.maximum(n, 1).astype(np.float32)
    ratio = np.log(nf / np.float32(max_exact)) / np.float32(math.log(MAX_DISTANCE / max_exact))
    large = max_exact + (ratio * np.float32(nb - max_exact)).astype(np.int32)
    large = np.minimum(large, nb - 1)
    return ret + np.where(n < max_exact, n, large)


def _band_biases(rel_bias):
    period = SUBQ + SUBK
    nband = 2 * RADIUS + 1
    onehot = np.zeros((N_GROUPS, period, NUM_BUCKETS), np.float32)
    for g, dilation in enumerate(DILATIONS):
        onehot[g, np.arange(nband), _t5_bucket_np((np.arange(nband) - RADIUS) * dilation)] = 1.0
    tab = rel_bias.astype(F32).reshape(NUM_BUCKETS, N_GROUPS, HEADS)
    t = jnp.einsum("gpb,bgh->ghp", jnp.asarray(onehot), tab, precision=lax.Precision.HIGHEST)
    t = jnp.where(np.arange(period) < nband, t, NEG_INF)
    skew = jnp.tile(t, (1, 1, SUBQ))[:, :, :SUBQ * (period - 1)].reshape(N_GROUPS, HEADS, SUBQ, period - 1)
    return skew[..., :SUBK].reshape(N_GROUPS, HEADS * SUBQ, SUBK) * LOG2E


def _mem_kv_kernel(mem_ref, g_ref, w_ref, kv_ref):
    h = _rms(mem_ref[0], g_ref[...]).astype(BF16)
    kv_ref[0] = jnp.dot(h, w_ref[0], preferred_element_type=F32).astype(BF16)


def _mem_kv(mem, g, w, layer):
    B, M, _ = mem.shape
    return pl.pallas_call(
        _mem_kv_kernel,
        grid=(B,),
        in_specs=[pl.BlockSpec((1, M, D_MODEL), lambda b: (b, 0, 0)),
                  pl.BlockSpec((1, D_MODEL), lambda b: (0, 0)),
                  _layer_weight(w, layer)],
        out_specs=pl.BlockSpec((1, M, 2 * MEM_WIDTH), lambda b: (b, 0, 0)),
        out_shape=jax.ShapeDtypeStruct((B, M, 2 * MEM_WIDTH), BF16),
        compiler_params=pltpu.CompilerParams(
            dimension_semantics=("parallel",), vmem_limit_bytes=VMEM_LIMIT),
        name="mem_kv",
    )(mem, g, w)


MIX_TM = 1024
FFN_OUT_CAST_ROWS = 176


def _mix_kernel(x_ref, c_ref, qm_ref, kv_ref, oatt_ref, gate_ref, wc_ref, wa_ref, wm_ref, wo_ref,
                gpost_ref, wfi_ref, wfo_ref, out_ref, wfi_out_ref, wfo_out_ref):
    wfi_out_ref[...] = wfi_ref[...].astype(BF16)
    wfo_out_ref[...] = wfo_ref[...].astype(BF16)

    scale = MEM_HEAD_DIM ** -0.5
    heads = []
    for h in range(MEM_HEADS):
        lo = h * MEM_HEAD_DIM
        qh = qm_ref[:, lo:lo + MEM_HEAD_DIM]
        kh = kv_ref[0, :, lo:lo + MEM_HEAD_DIM]
        vh = kv_ref[0, :, MEM_WIDTH + lo:MEM_WIDTH + lo + MEM_HEAD_DIM]
        s = lax.dot_general(qh, kh, (((1,), (1,)), ((), ())), preferred_element_type=F32) * scale
        m = jnp.max(s, axis=-1, keepdims=True)
        e = jnp.exp(s - m)
        p = (e * (1.0 / jnp.sum(e, axis=-1, keepdims=True))).astype(BF16)
        heads.append(jnp.dot(p, vh, preferred_element_type=F32).astype(BF16))
    o_mem = jnp.concatenate(heads, axis=-1)
    y_mem = jnp.dot(o_mem, wm_ref[0], preferred_element_type=F32)
    y_conv = jnp.dot(c_ref[...], wc_ref[0], preferred_element_type=F32)
    y_att = jnp.dot(oatt_ref[...], wa_ref[0], preferred_element_type=F32)
    merged = (gate_ref[:, 0:D_MODEL].astype(F32) * y_conv
              + gate_ref[:, D_MODEL:2 * D_MODEL].astype(F32) * y_att
              + gate_ref[:, 2 * D_MODEL:].astype(F32) * y_mem)
    y = jnp.dot(merged.astype(BF16), wo_ref[0], preferred_element_type=F32)
    out_ref[...] = x_ref[...] + _rms(y, gpost_ref[...])


def _mix(x2, c2, qm, kv, o_att, gates, wc, wa, wm, wo, layer, gpost, w_ffn_in, w_ffn_out, ffn_layer,
         tiles_per_batch):
    T = x2.shape[0]
    M = kv.shape[1]
    steps = T // MIX_TM
    in_rows = D_MODEL // steps
    out_steps = FFN_HIDDEN // FFN_OUT_CAST_ROWS
    assert in_rows % BF16_ROWS == 0 and FFN_OUT_CAST_ROWS % BF16_ROWS == 0 and out_steps <= steps

    def w_slice(rows_per_step, width, lead, nsteps):
        return pl.BlockSpec((1, rows_per_step, width), lambda i: (lead, jnp.minimum(i, nsteps - 1), 0))

    def rows(width):
        return pl.BlockSpec((MIX_TM, width), lambda i: (i, 0))

    def whole(shape):
        return pl.BlockSpec(shape, lambda i: (0,) * len(shape))

    return pl.pallas_call(
        _mix_kernel,
        grid=(T // MIX_TM,),
        in_specs=[rows(D_MODEL), rows(CONV_WIDTH), rows(MEM_WIDTH),
                  pl.BlockSpec((1, M, 2 * MEM_WIDTH), lambda i: (i // tiles_per_batch, 0, 0)),
                  rows(ATT_OUT), rows(GATE_WIDTH),
                  _layer_weight(wc, layer), _layer_weight(wa, layer),
                  _layer_weight(wm, layer), _layer_weight(wo, layer), whole((1, D_MODEL)),
                  w_slice(in_rows, 2 * FFN_HIDDEN, ffn_layer, steps),
                  w_slice(FFN_OUT_CAST_ROWS, D_MODEL, ffn_layer, out_steps)],
        out_specs=[rows(D_MODEL),
                   w_slice(in_rows, 2 * FFN_HIDDEN, 0, steps),
                   w_slice(FFN_OUT_CAST_ROWS, D_MODEL, 0, out_steps)],
        out_shape=[jax.ShapeDtypeStruct((T, D_MODEL), F32),
                   jax.ShapeDtypeStruct((1, D_MODEL, 2 * FFN_HIDDEN), BF16),
                   jax.ShapeDtypeStruct((1, FFN_HIDDEN, D_MODEL), BF16)],
        compiler_params=pltpu.CompilerParams(
            dimension_semantics=("arbitrary",), vmem_limit_bytes=VMEM_LIMIT),
        name="mix_out",
    )(x2, c2, qm, kv, o_att, gates, wc, wa, wm, wo, gpost, w_ffn_in, w_ffn_out)


FFN_TM = 1024
FFN_TH = 256
FFN_CHUNKS = FFN_HIDDEN // FFN_TH


def _ffn_kernel(x_ref, gpre_ref, win_ref, wout_ref, gpost_ref, *refs):
    out_ref = refs[0] if len(refs) == 1 else refs[1]
    if len(refs) == 3:
        refs[2][...] = refs[0][...].astype(BF16)
    x = x_ref[...]
    h = _rms(x, gpre_ref[...]).astype(BF16)
    acc = None
    for c in range(FFN_CHUNKS):
        lo = c * FFN_TH
        gv = jnp.dot(h, win_ref[0, :, lo:lo + FFN_TH], preferred_element_type=F32)
        uv = jnp.dot(h, win_ref[0, :, FFN_HIDDEN + lo:FFN_HIDDEN + lo + FFN_TH], preferred_element_type=F32)
        a = (gv * _sigmoid(gv) * uv).astype(BF16)
        part = jnp.dot(a, wout_ref[0, lo:lo + FFN_TH, :], preferred_element_type=F32)
        acc = part if acc is None else acc + part
    out_ref[...] = x + _rms(acc, gpost_ref[...])


def _ffn(x2, gpre, w_in, w_out, layer, gpost, next_w=None):
    T = x2.shape[0]
    steps = T // FFN_TM
    cast_in, cast_out, cast_shape, cast_args = [], [], [], []
    if next_w is not None:
        w, lead = next_w
        assert w.shape[1] % (steps * BF16_ROWS) == 0
        cast_in = [pl.BlockSpec((1, w.shape[1] // steps, w.shape[2]), lambda i: (lead, i, 0))]
        cast_out = [pl.BlockSpec((1, w.shape[1] // steps, w.shape[2]), lambda i: (0, i, 0))]
        cast_shape = [jax.ShapeDtypeStruct((1,) + w.shape[1:], BF16)]
        cast_args = [w]

    def resident(shape):
        return pl.BlockSpec(shape, lambda i: (0, 0), pipeline_mode=pl.Buffered(1))

    outs = pl.pallas_call(
        _ffn_kernel,
        grid=(steps,),
        in_specs=[pl.BlockSpec((FFN_TM, D_MODEL), lambda i: (i, 0)),
                  resident((1, D_MODEL)),
                  _layer_weight(w_in, layer), _layer_weight(w_out, layer),
                  resident((1, D_MODEL))] + cast_in,
        out_specs=[pl.BlockSpec((FFN_TM, D_MODEL), lambda i: (i, 0))] + cast_out,
        out_shape=[jax.ShapeDtypeStruct((T, D_MODEL), F32)] + cast_shape,
        compiler_params=pltpu.CompilerParams(
            dimension_semantics=("arbitrary",), vmem_limit_bytes=VMEM_LIMIT),
        name="ffn",
    )(x2, gpre, w_in, w_out, gpost, *cast_args)
    return outs[0], (outs[1] if next_w is not None else None)


def kernel(x, mem, rel_bias, norm_mix_pre, w_in, b_gate, conv_dw, conv_dw_bias, conv_ln_g, conv_ln_b,
           w_conv_out, w_att_out, norm_mem, w_mem_kv, w_mem_out, w_out, norm_mix_post, norm_ffn_pre,
           w_ffn_in, w_ffn_out, norm_ffn_post):
    B, S, D = x.shape
    depth = w_in.shape[0]
    T = B * S
    biases = _band_biases(rel_bias)

    def row(v):
        return v.reshape(1, -1)

    w_in_bf16 = w_in[:1].astype(BF16)
    x2 = x.reshape(T, D)
    for l in range(depth):
        c, qm, qkv0, qkv1, qkv2, gates = _in_proj(
            x2, row(norm_mix_pre[l]), w_in_bf16, 0, row(b_gate[l]),
            conv_dw[l], row(conv_dw_bias[l]), row(conv_ln_g[l]), row(conv_ln_b[l]), B, S)
        o_att, wkv, wc, wa, wm, wo = _attention(
            (qkv0, qkv1, qkv2), biases, (w_mem_kv, w_conv_out, w_att_out, w_mem_out, w_out), l)
        kv = _mem_kv(mem, row(norm_mem[l]), wkv, 0)
        x2, wfi, wfo = _mix(x2, c, qm, kv, o_att.reshape(T, ATT_OUT), gates, wc, wa, wm, wo, 0,
                            row(norm_mix_post[l]), w_ffn_in, w_ffn_out, l, S // MIX_TM)
        next_w_in = (w_in, l + 1) if l + 1 < depth else None
        x2, w_in_bf16 = _ffn(x2, row(norm_ffn_pre[l]), wfi, wfo, 0, row(norm_ffn_post[l]), next_w_in)
    return x2.reshape(B, S, D)
```

```python
import functools
import math

import numpy as np
import jax
import jax.numpy as jnp
from jax import lax
from jax.experimental import pallas as pl
from jax.experimental.pallas import tpu as pltpu

F32 = jnp.float32
BF16 = jnp.bfloat16

D_MODEL = 1024
CONV_WIDTH = 512
CONV_KSIZE = 31
CONV_PAD = CONV_KSIZE // 2
DILATIONS = (1, 4, 16)
RADIUS = 64
N_GROUPS = 3
HEADS = 4
HEAD_DIM = 64
ATT_OUT = HEADS * HEAD_DIM
QKV_WIDTH = 3 * ATT_OUT
MEM_HEADS = 4
MEM_HEAD_DIM = 128
MEM_WIDTH = 512
FFN_HIDDEN = 2816
NUM_BUCKETS = 32
MAX_DISTANCE = 1024
RMS_EPS = 1e-6
LN_EPS = 1e-5
NEG_INF = -1e30

GATE_WIDTH = 3 * D_MODEL
SUBQ = 128
SUBK = SUBQ + 2 * RADIUS
LANES = 128
BF16_ROWS = 16
LSE_LANES = 128
LSE_PER_HEAD = LSE_LANES // HEADS
STAT_LANES = LSE_PER_HEAD // 2
LOG2E = math.log2(math.e)

V7X_VMEM_BYTES = 64 * 1024 * 1024
VMEM_LIMIT = V7X_VMEM_BYTES * 7 // 8


def _sigmoid(v):
    return 1.0 / (1.0 + jnp.exp(-v))


def _rms(v, g):
    return v * lax.rsqrt(jnp.mean(v * v, axis=-1, keepdims=True) + RMS_EPS) * g


def _derived_zero(v):
    bits = lax.bitcast_convert_type(v, jnp.uint32)
    bits = lax.shift_right_logical(lax.shift_right_logical(bits, jnp.uint32(16)), jnp.uint32(16))
    return lax.bitcast_convert_type(bits, F32)


def _layer_weight(w, layer):
    return pl.BlockSpec((1,) + w.shape[1:], lambda *_: (layer, 0, 0), pipeline_mode=pl.Buffered(1))


IN_TM = 512
IN_TN = 768
AG_WIDTH = 2 * CONV_WIDTH
QKV_COL = AG_WIDTH
QM_COL = QKV_COL + N_GROUPS * QKV_WIDTH
GATE_COL = QM_COL + MEM_WIDTH
N_GATE_TILES = GATE_WIDTH // IN_TN
assert IN_TN == QKV_WIDTH
DOT_TN = 256
SUBLANES = 8
CONV_HALO = 16
CONV_HALF = 256
CONV_RC = 16
CONV_FIRST = CONV_HALO - CONV_PAD
CONV_SHIFT_ROWS = CONV_HALF + (CONV_FIRST + CONV_KSIZE - 1) // SUBLANES * SUBLANES
IN_EXT = IN_TM + 2 * CONV_HALO


def _in_proj_kernel(x_ref, xp_ref, xn_ref, g_ref, w_ref, b_ref, cw_ref, cb_ref, lg_ref, lb_ref,
                    c_ref, qm_ref, q0_ref, q1_ref, q2_ref, gate_ref, acc_ref, u_ref, us_ref,
                    *, tiles_per_batch):
    ib = pl.program_id(0) % tiles_per_batch
    x_ext = jnp.concatenate([xp_ref[...], x_ref[...], xn_ref[...]], axis=0)
    h_ext = _rms(x_ext, g_ref[...]).astype(BF16)
    h = h_ext[CONV_HALO:CONV_HALO + IN_TM]

    for t in range(CONV_WIDTH // DOT_TN):
        cols = slice(t * DOT_TN, (t + 1) * DOT_TN)
        a = jnp.dot(h_ext, w_ref[0, :, cols], preferred_element_type=F32)
        gt = jnp.dot(h_ext, w_ref[0, :, CONV_WIDTH + t * DOT_TN:CONV_WIDTH + (t + 1) * DOT_TN],
                     preferred_element_type=F32)
        u = a * _sigmoid(gt)
        u_ref[0:CONV_HALO, cols] = jnp.where(ib > 0, u[0:CONV_HALO], 0.0)
        u_ref[CONV_HALO:CONV_HALO + IN_TM, cols] = u[CONV_HALO:CONV_HALO + IN_TM]
        u_ref[CONV_HALO + IN_TM:, cols] = jnp.where(ib < tiles_per_batch - 1, u[CONV_HALO + IN_TM:], 0.0)

    def shift_copies(half):
        for s in range(SUBLANES):
            us_ref[s] = u_ref[half * CONV_HALF + s:half * CONV_HALF + s + CONV_SHIFT_ROWS, :]

    def conv_chunk(half, c, zero):
        acc = jnp.concatenate([zero] * (CONV_RC // SUBLANES), axis=0)
        for k in range(CONV_KSIZE):
            off = CONV_FIRST + k
            r0 = c * CONV_RC + off // SUBLANES * SUBLANES
            acc = acc + us_ref[off % SUBLANES, r0:r0 + CONV_RC, :] * cw_ref[k:k + 1, :]
        y = acc + cb_ref[...]
        mu = jnp.mean(y, axis=-1, keepdims=True)
        yc = y - mu
        yn = yc * lax.rsqrt(jnp.mean(yc * yc, axis=-1, keepdims=True) + LN_EPS)
        yn = yn * lg_ref[...] + lb_ref[...]
        act = yn * _sigmoid(yn)
        rows = slice(half * CONV_HALF + c * CONV_RC, half * CONV_HALF + (c + 1) * CONV_RC)
        c_ref[rows, :] = act.astype(BF16)
        return _derived_zero(y[CONV_RC - SUBLANES:])

    def project(col, width):
        return jnp.dot(h, w_ref[0, :, col:col + width], preferred_element_type=F32)

    def mem_queries():
        for t in range(MEM_WIDTH // DOT_TN):
            acc = project(QM_COL + t * DOT_TN, DOT_TN)
            qm_ref[:, t * DOT_TN:(t + 1) * DOT_TN] = acc.astype(BF16)
        return acc

    q_scale = HEAD_DIM ** -0.5 * LOG2E

    def qkv_group(g):
        d, out_ref = DILATIONS[g], (q0_ref, q1_ref, q2_ref)[g]
        for which in range(3):
            acc = project(QKV_COL + (which * N_GROUPS + g) * ATT_OUT, ATT_OUT)
            if which == 0:
                acc = acc * q_scale
            cols = slice(which * ATT_OUT, (which + 1) * ATT_OUT)
            if d == 1:
                out_ref[0, 0, :, cols] = acc.astype(BF16)
                continue
            for half in range(ATT_OUT // LANES):
                acc_ref[2 * which + half] = acc[:, half * LANES:(half + 1) * LANES]
        if d > 1:
            for r in range(d):
                for cb in range(QKV_WIDTH // LANES):
                    out_ref[0, r, :, cb * LANES:(cb + 1) * LANES] = (
                        acc_ref[cb, pl.ds(r, IN_TM // d, stride=d), :].astype(BF16))
        return acc

    def gate_tile(t):
        for sub in range(IN_TN // DOT_TN):
            cols = slice(t * IN_TN + sub * DOT_TN, t * IN_TN + (sub + 1) * DOT_TN)
            acc = project(GATE_COL + cols.start, DOT_TN)
            gate_ref[:, cols] = _sigmoid(acc + b_ref[:, cols]).astype(BF16)
        return acc

    mxu_units = ([mem_queries] + [functools.partial(qkv_group, g) for g in range(N_GROUPS)]
                 + [functools.partial(gate_tile, t) for t in range(N_GATE_TILES)])
    chunks = [(half, c) for half in range(IN_TM // CONV_HALF) for c in range(CONV_HALF // CONV_RC)]
    chunks_per_unit = len(chunks) // len(mxu_units)
    assert chunks_per_unit * len(mxu_units) == len(chunks)
    zero = jnp.zeros((SUBLANES, CONV_WIDTH), F32)
    for step, unit in enumerate(mxu_units):
        for half, c in chunks[step * chunks_per_unit:(step + 1) * chunks_per_unit]:
            if c == 0:
                shift_copies(half)
            zero = conv_chunk(half, c, zero)
        acc = unit()
        tail = _derived_zero(acc[IN_TM - SUBLANES:, acc.shape[1] - LANES:])
        zero = zero + jnp.concatenate([tail] * (CONV_WIDTH // LANES), axis=1)


def _in_proj(x2, g, w, layer, b, conv_w, conv_b, ln_g, ln_b, batch, seq):
    T = x2.shape[0]
    tiles_per_batch = seq // IN_TM
    halo_per_tile = IN_TM // CONV_HALO
    last_halo = T // CONV_HALO - 1

    def qkv_spec(d):
        return pl.BlockSpec((1, d, IN_TM // d, QKV_WIDTH),
                            lambda i: (i // tiles_per_batch, 0, i % tiles_per_batch, 0))

    def resident(shape):
        return pl.BlockSpec(shape, lambda i: (0, 0), pipeline_mode=pl.Buffered(1))

    def rows(width):
        return pl.BlockSpec((IN_TM, width), lambda i: (i, 0))

    return pl.pallas_call(
        functools.partial(_in_proj_kernel, tiles_per_batch=tiles_per_batch),
        grid=(T // IN_TM,),
        in_specs=[
            rows(D_MODEL),
            pl.BlockSpec((CONV_HALO, D_MODEL), lambda i: (jnp.maximum(i * halo_per_tile - 1, 0), 0)),
            pl.BlockSpec((CONV_HALO, D_MODEL), lambda i: (jnp.minimum((i + 1) * halo_per_tile, last_halo), 0)),
            resident((1, D_MODEL)),
            _layer_weight(w, layer),
            resident((1, GATE_WIDTH)),
            resident((CONV_KSIZE, CONV_WIDTH)), resident((1, CONV_WIDTH)),
            resident((1, CONV_WIDTH)), resident((1, CONV_WIDTH)),
        ],
        out_specs=[rows(CONV_WIDTH), rows(MEM_WIDTH),
                   qkv_spec(DILATIONS[0]), qkv_spec(DILATIONS[1]), qkv_spec(DILATIONS[2]),
                   rows(GATE_WIDTH)],
        out_shape=[jax.ShapeDtypeStruct((T, CONV_WIDTH), BF16), jax.ShapeDtypeStruct((T, MEM_WIDTH), BF16)]
        + [jax.ShapeDtypeStruct((batch, d, seq // d, QKV_WIDTH), BF16) for d in DILATIONS]
        + [jax.ShapeDtypeStruct((T, GATE_WIDTH), BF16)],
        scratch_shapes=[pltpu.VMEM((QKV_WIDTH // LANES, IN_TM, LANES), F32),
                        pltpu.VMEM((IN_EXT, CONV_WIDTH), F32),
                        pltpu.VMEM((SUBLANES, CONV_SHIFT_ROWS, CONV_WIDTH), F32)],
        compiler_params=pltpu.CompilerParams(
            dimension_semantics=("parallel",), vmem_limit_bytes=VMEM_LIMIT),
        name="in_proj",
    )(x2, x2, x2, g, w, b, conv_w, conv_b, ln_g, ln_b)


ATT_TILE = 2048
ATT_COMBINE_ROWS = 1024


def _att_kernel(c0_ref, p0_ref, n0_ref, c1_ref, p1_ref, n1_ref, c2_ref, p2_ref, n2_ref,
                b0_ref, b1_ref, b2_ref, *refs):
    n_cast = (len(refs) - 4) // 2
    o_ref = refs[n_cast]
    onat_ref, lnat_ref, edge_ref = refs[2 * n_cast + 1:]
    for src_ref, dst_ref in zip(refs[:n_cast], refs[n_cast + 1:2 * n_cast + 1]):
        dst_ref[...] = src_ref[...].astype(BF16)
    i = pl.program_id(1)
    n = pl.num_programs(1)
    stat_of_lane = lax.broadcasted_iota(jnp.int32, (1, LSE_LANES), 1) // STAT_LANES
    key_col = lax.broadcasted_iota(jnp.int32, (1, SUBK), 1)
    K0, V0 = ATT_OUT, 2 * ATT_OUT
    even_head_lanes = lax.broadcasted_iota(jnp.int32, (1, LANES), 1) < HEAD_DIM

    before_start = jnp.where(key_col >= jnp.where(i == 0, RADIUS, 0), 0.0, NEG_INF)
    past_end = jnp.where(key_col < jnp.where(i == n - 1, SUBK - RADIUS, SUBK), 0.0, NEG_INF)
    for g, (d, b_ref) in enumerate(zip(DILATIONS, (b0_ref, b1_ref, b2_ref))):
        if ATT_TILE // d == SUBQ:
            edge_ref[2 * g] = b_ref[...] + before_start + past_end
        else:
            edge_ref[2 * g] = b_ref[...] + before_start
            edge_ref[2 * g + 1] = b_ref[...] + past_end

    def subtile(g, d, q, k, v, bias, row0):
        zeros = jnp.zeros((SUBQ, LANES), BF16)
        blocks = []
        for h in range(HEADS):
            tile = q[:, h // 2 * LANES:(h // 2 + 1) * LANES]
            tile = jnp.where(even_head_lanes, tile, zeros) if h % 2 == 0 else jnp.where(even_head_lanes, zeros, tile)
            blocks.append(jnp.concatenate([tile, zeros] if h < 2 else [zeros, tile], axis=1))
        qs = jnp.concatenate(blocks, axis=0)
        s = lax.dot_general(qs, k, (((1,), (1,)), ((), ())), preferred_element_type=F32) + bias
        m = jnp.max(s, axis=-1, keepdims=True)
        e = jnp.exp2(s - m)
        l = jnp.sum(e, axis=-1, keepdims=True)
        o_all = jnp.dot(e.astype(BF16), v, preferred_element_type=F32)
        o = jnp.concatenate(
            [jnp.where(even_head_lanes,
                       o_all[2 * t * SUBQ:(2 * t + 1) * SUBQ, t * LANES:(t + 1) * LANES],
                       o_all[(2 * t + 1) * SUBQ:(2 * t + 2) * SUBQ, t * LANES:(t + 1) * LANES])
             for t in range(ATT_OUT // LANES)], axis=1)
        stats = jnp.broadcast_to(l[(HEADS - 1) * SUBQ:], (SUBQ, LSE_LANES))
        stats = jnp.where(stat_of_lane == 2 * (HEADS - 1), m[(HEADS - 1) * SUBQ:], stats)
        for h in range(HEADS - 1):
            rows = slice(h * SUBQ, (h + 1) * SUBQ)
            stats = jnp.where(stat_of_lane == 2 * h, m[rows], stats)
            stats = jnp.where(stat_of_lane == 2 * h + 1, l[rows], stats)
        rows = pl.ds(row0, SUBQ) if d == 1 else pl.ds(row0, SUBQ, stride=d)
        for half in range(ATT_OUT // LANES):
            onat_ref[2 * g + half, rows, :] = o[:, half * LANES:(half + 1) * LANES]
        lnat_ref[g, rows, :] = stats

    groups = ((c0_ref, p0_ref, n0_ref, b0_ref), (c1_ref, p1_ref, n1_ref, b1_ref),
              (c2_ref, p2_ref, n2_ref, b2_ref))
    for g, (d, (c_ref, p_ref, n_ref, bias_ref)) in enumerate(zip(DILATIONS, groups)):
        tq = ATT_TILE // d
        nsub = tq // SUBQ
        for r in range(d):
            for j in range(nsub):
                parts = [(c_ref, slice(max(j * SUBQ - RADIUS, 0), min(j * SUBQ + SUBQ + RADIUS, tq)))]
                if j == 0:
                    parts = [(p_ref, slice(None))] + parts
                if j == nsub - 1:
                    parts = parts + [(n_ref, slice(None))]
                k, v = [jnp.concatenate([ref[0, r, rows, col:col + ATT_OUT] for ref, rows in parts], axis=0)
                        for col in (K0, V0)]
                q = c_ref[0, r, j * SUBQ:(j + 1) * SUBQ, 0:ATT_OUT]
                if j == 0:
                    bias = edge_ref[2 * g]
                elif j == nsub - 1:
                    bias = edge_ref[2 * g + 1]
                else:
                    bias = bias_ref[...]
                subtile(g, d, q, k, v, bias, j * SUBQ * d + r)

    w_lane = lax.broadcasted_iota(jnp.int32, (2 * LSE_LANES, ATT_OUT), 0) % LSE_LANES
    o_lane = lax.broadcasted_iota(jnp.int32, (2 * LSE_LANES, ATT_OUT), 1)
    spread = (w_lane == o_lane // HEAD_DIM * LSE_PER_HEAD).astype(BF16)
    weight_lane = lax.broadcasted_iota(jnp.int32, (1, LSE_LANES), 1) % LSE_PER_HEAD == 0

    def expand(w):
        w = jnp.where(weight_lane, w, 0.0)
        hi = w.astype(BF16)
        lo = (w - hi.astype(F32)).astype(BF16)
        return jnp.dot(jnp.concatenate([hi, lo], axis=1), spread, preferred_element_type=F32)

    def combine(t, carry):
        rows = pl.ds(pl.multiple_of(t * ATT_COMBINE_ROWS, ATT_COMBINE_ROWS), ATT_COMBINE_ROWS)
        stats = [lnat_ref[g, rows, :] for g in range(N_GROUPS)]
        mx = jnp.maximum(jnp.maximum(stats[0], stats[1]), stats[2])
        a = [jnp.exp2(st - mx) for st in stats]
        sums = [pltpu.roll(st, LSE_LANES - STAT_LANES, 1) for st in stats]
        inv = 1.0 / (a[0] * sums[0] + a[1] * sums[1] + a[2] * sums[2])
        o = None
        for g in range(N_GROUPS):
            acc = jnp.concatenate([onat_ref[2 * g, rows, :], onat_ref[2 * g + 1, rows, :]], axis=1)
            term = expand(a[g] * inv) * acc
            o = term if o is None else o + term
        o_ref[0, rows, :] = o.astype(BF16)
        return carry

    lax.fori_loop(0, ATT_TILE // ATT_COMBINE_ROWS, combine, 0)


def _attention(qkv, biases, weights, layer):
    B = qkv[0].shape[0]
    S = qkv[0].shape[2]
    tiles = S // ATT_TILE
    steps = B * tiles

    def w_slice(w, lead):
        assert w.shape[1] % (steps * BF16_ROWS) == 0
        return pl.BlockSpec((1, w.shape[1] // steps, w.shape[2]), lambda b, i: (lead, b * tiles + i, 0))
    in_specs = []
    for d in DILATIONS:
        tq = ATT_TILE // d
        nb = tq // RADIUS
        last_b = S // d // RADIUS - 1
        in_specs += [
            pl.BlockSpec((1, d, tq, QKV_WIDTH), lambda b, i: (b, 0, i, 0)),
            pl.BlockSpec((1, d, RADIUS, QKV_WIDTH),
                         lambda b, i, nb=nb: (b, 0, jnp.maximum(i * nb - 1, 0), 0)),
            pl.BlockSpec((1, d, RADIUS, QKV_WIDTH),
                         lambda b, i, nb=nb, last_b=last_b: (b, 0, jnp.minimum((i + 1) * nb, last_b), 0)),
        ]
    in_specs += [pl.BlockSpec((HEADS * SUBQ, SUBK), lambda b, i: (0, 0))] * N_GROUPS
    in_specs += [w_slice(w, layer) for w in weights]
    args = []
    for a in qkv:
        args += [a, a, a]
    return pl.pallas_call(
        _att_kernel,
        grid=(B, tiles),
        in_specs=in_specs,
        out_specs=[pl.BlockSpec((1, ATT_TILE, ATT_OUT), lambda b, i: (b, i, 0))]
        + [w_slice(w, 0) for w in weights],
        out_shape=[jax.ShapeDtypeStruct((B, S, ATT_OUT), BF16)]
        + [jax.ShapeDtypeStruct((1,) + w.shape[1:], BF16) for w in weights],
        scratch_shapes=[pltpu.VMEM((N_GROUPS * (ATT_OUT // LANES), ATT_TILE, LANES), F32),
                        pltpu.VMEM((N_GROUPS, ATT_TILE, LSE_LANES), F32),
                        pltpu.VMEM((2 * N_GROUPS, HEADS * SUBQ, SUBK), F32)],
        compiler_params=pltpu.CompilerParams(
            dimension_semantics=("arbitrary", "arbitrary"), vmem_limit_bytes=VMEM_LIMIT),
        name="dilated_att",
    )(*args, *biases, *weights)


def _t5_bucket_np(rel):
    nb = NUM_BUCKETS // 2
    max_exact = nb // 2
    ret = np.where(rel > 0, nb, 0)
    n = np.abs(rel)
    nf = np.maximum(n, 1).astype(np.float32)
    ratio = np.log(nf / np.float32(max_exact)) / np.float32(math.log(MAX_DISTANCE / max_exact))
    large = max_exact + (ratio * np.float32(nb - max_exact)).astype(np.int32)
    large = np.minimum(large, nb - 1)
    return ret + np.where(n < max_exact, n, large)


def _band_bias(rel_bias, g, dilation):
    period = SUBQ + SUBK
    nband = 2 * RADIUS + 1
    bucket = _t5_bucket_np((np.arange(nband) - RADIUS) * dilation)
    onehot = np.zeros((period, NUM_BUCKETS), np.float32)
    onehot[np.arange(nband), bucket] = 1.0
    tab = rel_bias[:, g * HEADS:(g + 1) * HEADS].astype(F32)
    t = jnp.dot(jnp.asarray(onehot), tab, precision=lax.Precision.HIGHEST)
    t = jnp.where((np.arange(period) < nband)[:, None], t, NEG_INF).T
    skew = jnp.tile(t, (1, SUBQ))[:, :SUBQ * (period - 1)].reshape(HEADS, SUBQ, period - 1)
    return skew[:, :, :SUBK].reshape(HEADS * SUBQ, SUBK) * LOG2E


def _mem_kv_kernel(mem_ref, g_ref, w_ref, kv_ref):
    h = _rms(mem_ref[0], g_ref[...]).astype(BF16)
    kv_ref[0] = jnp.dot(h, w_ref[0], preferred_element_type=F32).astype(BF16)


def _mem_kv(mem, g, w, layer):
    B, M, _ = mem.shape
    return pl.pallas_call(
        _mem_kv_kernel,
        grid=(B,),
        in_specs=[pl.BlockSpec((1, M, D_MODEL), lambda b: (b, 0, 0)),
                  pl.BlockSpec((1, D_MODEL), lambda b: (0, 0)),
                  _layer_weight(w, layer)],
        out_specs=pl.BlockSpec((1, M, 2 * MEM_WIDTH), lambda b: (b, 0, 0)),
        out_shape=jax.ShapeDtypeStruct((B, M, 2 * MEM_WIDTH), BF16),
        compiler_params=pltpu.CompilerParams(
            dimension_semantics=("parallel",), vmem_limit_bytes=VMEM_LIMIT),
        name="mem_kv",
    )(mem, g, w)


MIX_TM = 1024
FFN_OUT_CAST_ROWS = 176


def _mix_kernel(x_ref, c_ref, qm_ref, kv_ref, oatt_ref, gate_ref, wc_ref, wa_ref, wm_ref, wo_ref,
                gpost_ref, wfi_ref, wfo_ref, out_ref, wfi_out_ref, wfo_out_ref):
    wfi_out_ref[...] = wfi_ref[...].astype(BF16)
    wfo_out_ref[...] = wfo_ref[...].astype(BF16)

    scale = MEM_HEAD_DIM ** -0.5
    heads = []
    for h in range(MEM_HEADS):
        lo = h * MEM_HEAD_DIM
        qh = qm_ref[:, lo:lo + MEM_HEAD_DIM]
        kh = kv_ref[0, :, lo:lo + MEM_HEAD_DIM]
        vh = kv_ref[0, :, MEM_WIDTH + lo:MEM_WIDTH + lo + MEM_HEAD_DIM]
        s = lax.dot_general(qh, kh, (((1,), (1,)), ((), ())), preferred_element_type=F32) * scale
        m = jnp.max(s, axis=-1, keepdims=True)
        e = jnp.exp(s - m)
        p = (e * (1.0 / jnp.sum(e, axis=-1, keepdims=True))).astype(BF16)
        heads.append(jnp.dot(p, vh, preferred_element_type=F32).astype(BF16))
    o_mem = jnp.concatenate(heads, axis=-1)
    y_mem = jnp.dot(o_mem, wm_ref[0], preferred_element_type=F32)
    y_conv = jnp.dot(c_ref[...], wc_ref[0], preferred_element_type=F32)
    y_att = jnp.dot(oatt_ref[...], wa_ref[0], preferred_element_type=F32)
    merged = (gate_ref[:, 0:D_MODEL].astype(F32) * y_conv
              + gate_ref[:, D_MODEL:2 * D_MODEL].astype(F32) * y_att
              + gate_ref[:, 2 * D_MODEL:].astype(F32) * y_mem)
    y = jnp.dot(merged.astype(BF16), wo_ref[0], preferred_element_type=F32)
    out_ref[...] = x_ref[...] + _rms(y, gpost_ref[...])


def _mix(x2, c2, qm, kv, o_att, gates, wc, wa, wm, wo, layer, gpost, w_ffn_in, w_ffn_out, ffn_layer,
         tiles_per_batch):
    T = x2.shape[0]
    M = kv.shape[1]
    steps = T // MIX_TM
    in_rows = D_MODEL // steps
    out_steps = FFN_HIDDEN // FFN_OUT_CAST_ROWS
    assert in_rows % BF16_ROWS == 0 and FFN_OUT_CAST_ROWS % BF16_ROWS == 0 and out_steps <= steps

    def w_slice(rows_per_step, width, lead, nsteps):
        return pl.BlockSpec((1, rows_per_step, width), lambda i: (lead, jnp.minimum(i, nsteps - 1), 0))

    def rows(width):
        return pl.BlockSpec((MIX_TM, width), lambda i: (i, 0))

    def whole(shape):
        return pl.BlockSpec(shape, lambda i: (0,) * len(shape))

    return pl.pallas_call(
        _mix_kernel,
        grid=(T // MIX_TM,),
        in_specs=[rows(D_MODEL), rows(CONV_WIDTH), rows(MEM_WIDTH),
                  pl.BlockSpec((1, M, 2 * MEM_WIDTH), lambda i: (i // tiles_per_batch, 0, 0)),
                  rows(ATT_OUT), rows(GATE_WIDTH),
                  _layer_weight(wc, layer), _layer_weight(wa, layer),
                  _layer_weight(wm, layer), _layer_weight(wo, layer), whole((1, D_MODEL)),
                  w_slice(in_rows, 2 * FFN_HIDDEN, ffn_layer, steps),
                  w_slice(FFN_OUT_CAST_ROWS, D_MODEL, ffn_layer, out_steps)],
        out_specs=[rows(D_MODEL),
                   w_slice(in_rows, 2 * FFN_HIDDEN, 0, steps),
                   w_slice(FFN_OUT_CAST_ROWS, D_MODEL, 0, out_steps)],
        out_shape=[jax.ShapeDtypeStruct((T, D_MODEL), F32),
                   jax.ShapeDtypeStruct((1, D_MODEL, 2 * FFN_HIDDEN), BF16),
                   jax.ShapeDtypeStruct((1, FFN_HIDDEN, D_MODEL), BF16)],
        compiler_params=pltpu.CompilerParams(
            dimension_semantics=("arbitrary",), vmem_limit_bytes=VMEM_LIMIT),
        name="mix_out",
    )(x2, c2, qm, kv, o_att, gates, wc, wa, wm, wo, gpost, w_ffn_in, w_ffn_out)


FFN_TM = 1024
FFN_TH = 256
FFN_CHUNKS = FFN_HIDDEN // FFN_TH


def _ffn_kernel(x_ref, gpre_ref, win_ref, wout_ref, gpost_ref, *refs):
    out_ref = refs[0] if len(refs) == 1 else refs[1]
    if len(refs) == 3:
        refs[2][...] = refs[0][...].astype(BF16)
    x = x_ref[...]
    h = _rms(x, gpre_ref[...]).astype(BF16)
    acts = []
    for c in range(FFN_CHUNKS):
        lo = c * FFN_TH
        gv = jnp.dot(h, win_ref[0, :, lo:lo + FFN_TH], preferred_element_type=F32)
        uv = jnp.dot(h, win_ref[0, :, FFN_HIDDEN + lo:FFN_HIDDEN + lo + FFN_TH], preferred_element_type=F32)
        acts.append((gv * _sigmoid(gv) * uv).astype(BF16))
    acc = jnp.dot(jnp.concatenate(acts, axis=1), wout_ref[0], preferred_element_type=F32)
    out_ref[...] = x + _rms(acc, gpost_ref[...])


def _ffn(x2, gpre, w_in, w_out, layer, gpost, next_w=None):
    T = x2.shape[0]
    steps = T // FFN_TM
    cast_in, cast_out, cast_shape, cast_args = [], [], [], []
    if next_w is not None:
        w, lead = next_w
        assert w.shape[1] % (steps * BF16_ROWS) == 0
        cast_in = [pl.BlockSpec((1, w.shape[1] // steps, w.shape[2]), lambda i: (lead, i, 0))]
        cast_out = [pl.BlockSpec((1, w.shape[1] // steps, w.shape[2]), lambda i: (0, i, 0))]
        cast_shape = [jax.ShapeDtypeStruct((1,) + w.shape[1:], BF16)]
        cast_args = [w]

    def resident(shape):
        return pl.BlockSpec(shape, lambda i: (0, 0), pipeline_mode=pl.Buffered(1))

    outs = pl.pallas_call(
        _ffn_kernel,
        grid=(steps,),
        in_specs=[pl.BlockSpec((FFN_TM, D_MODEL), lambda i: (i, 0)),
                  resident((1, D_MODEL)),
                  _layer_weight(w_in, layer), _layer_weight(w_out, layer),
                  resident((1, D_MODEL))] + cast_in,
        out_specs=[pl.BlockSpec((FFN_TM, D_MODEL), lambda i: (i, 0))] + cast_out,
        out_shape=[jax.ShapeDtypeStruct((T, D_MODEL), F32)] + cast_shape,
        compiler_params=pltpu.CompilerParams(
            dimension_semantics=("arbitrary",), vmem_limit_bytes=VMEM_LIMIT),
        name="ffn",
    )(x2, gpre, w_in, w_out, gpost, *cast_args)
    return outs[0], (outs[1] if next_w is not None else None)


def kernel(x, mem, rel_bias, norm_mix_pre, w_in, b_gate, conv_dw, conv_dw_bias, conv_ln_g, conv_ln_b,
           w_conv_out, w_att_out, norm_mem, w_mem_kv, w_mem_out, w_out, norm_mix_post, norm_ffn_pre,
           w_ffn_in, w_ffn_out, norm_ffn_post):
    B, S, D = x.shape
    depth = w_in.shape[0]
    T = B * S
    biases = [_band_bias(rel_bias, g, d) for g, d in enumerate(DILATIONS)]

    def row(v):
        return v.reshape(1, -1)

    w_in_bf16 = w_in[:1].astype(BF16)
    x2 = x.reshape(T, D)
    for l in range(depth):
        c, qm, qkv0, qkv1, qkv2, gates = _in_proj(
            x2, row(norm_mix_pre[l]), w_in_bf16, 0, row(b_gate[l]),
            conv_dw[l], row(conv_dw_bias[l]), row(conv_ln_g[l]), row(conv_ln_b[l]), B, S)
        o_att, wkv, wc, wa, wm, wo = _attention(
            (qkv0, qkv1, qkv2), biases, (w_mem_kv, w_conv_out, w_att_out, w_mem_out, w_out), l)
        kv = _mem_kv(mem, row(norm_mem[l]), wkv, 0)
        x2, wfi, wfo = _mix(x2, c, qm, kv, o_att.reshape(T, ATT_OUT), gates, wc, wa, wm, wo, 0,
                            row(norm_mix_post[l]), w_ffn_in, w_ffn_out, l, S // MIX_TM)
        next_w_in = (w_in, l + 1) if l + 1 < depth else None
        x2, w_in_bf16 = _ffn(x2, row(norm_ffn_pre[l]), wfi, wfo, 0, row(norm_ffn_post[l]), next_w_in)
    return x2.reshape(B, S, D)
```

```python
import functools
import math

import numpy as np
import jax
import jax.numpy as jnp
from jax import lax
from jax.experimental import pallas as pl
from jax.experimental.pallas import tpu as pltpu

F32 = jnp.float32
BF16 = jnp.bfloat16

D_MODEL = 1024
CONV_WIDTH = 512
CONV_KSIZE = 31
CONV_PAD = CONV_KSIZE // 2
DILATIONS = (1, 4, 16)
RADIUS = 64
N_GROUPS = 3
HEADS = 4
HEAD_DIM = 64
ATT_OUT = HEADS * HEAD_DIM
QKV_WIDTH = 3 * ATT_OUT
MEM_HEADS = 4
MEM_HEAD_DIM = 128
MEM_WIDTH = 512
FFN_HIDDEN = 2816
NUM_BUCKETS = 32
MAX_DISTANCE = 1024
RMS_EPS = 1e-6
LN_EPS = 1e-5
NEG_INF = -1e30

GATE_WIDTH = 3 * D_MODEL
SUBQ = 128
SUBK = SUBQ + 2 * RADIUS
LANES = 128
BF16_ROWS = 16
LSE_LANES = 128
LSE_PER_HEAD = LSE_LANES // HEADS
STAT_LANES = LSE_PER_HEAD // 2
LOG2E = math.log2(math.e)

V7X_VMEM_BYTES = 64 * 1024 * 1024
VMEM_LIMIT = V7X_VMEM_BYTES * 7 // 8


def _sigmoid(v):
    return 1.0 / (1.0 + jnp.exp(-v))


def _rms(v, g):
    return v * lax.rsqrt(jnp.mean(v * v, axis=-1, keepdims=True) + RMS_EPS) * g


def _derived_zero(v):
    bits = lax.bitcast_convert_type(v, jnp.uint32)
    bits = lax.shift_right_logical(lax.shift_right_logical(bits, jnp.uint32(16)), jnp.uint32(16))
    return lax.bitcast_convert_type(bits, F32)


def _layer_weight(w, layer):
    return pl.BlockSpec((1,) + w.shape[1:], lambda *_: (layer, 0, 0), pipeline_mode=pl.Buffered(1))


IN_TM = 512
IN_TN = 768
AG_WIDTH = 2 * CONV_WIDTH
QKV_COL = AG_WIDTH
QM_COL = QKV_COL + N_GROUPS * QKV_WIDTH
GATE_COL = QM_COL + MEM_WIDTH
N_GATE_TILES = GATE_WIDTH // IN_TN
assert IN_TN == QKV_WIDTH
DOT_TN = 256
SUBLANES = 8
CONV_HALO = 16
CONV_HALF = 256
CONV_RC = 16
CONV_FIRST = CONV_HALO - CONV_PAD
CONV_SHIFT_ROWS = CONV_HALF + (CONV_FIRST + CONV_KSIZE - 1) // SUBLANES * SUBLANES
IN_EXT = IN_TM + 2 * CONV_HALO


def _in_proj_kernel(x_ref, xp_ref, xn_ref, g_ref, w_ref, b_ref, cw_ref, cb_ref, lg_ref, lb_ref,
                    c_ref, qm_ref, q0_ref, q1_ref, q2_ref, gate_ref, acc_ref, u_ref, us_ref,
                    *, tiles_per_batch):
    ib = pl.program_id(0) % tiles_per_batch
    x_ext = jnp.concatenate([xp_ref[...], x_ref[...], xn_ref[...]], axis=0)
    h_ext = _rms(x_ext, g_ref[...]).astype(BF16)
    h = h_ext[CONV_HALO:CONV_HALO + IN_TM]

    for t in range(CONV_WIDTH // DOT_TN):
        cols = slice(t * DOT_TN, (t + 1) * DOT_TN)
        a = jnp.dot(h_ext, w_ref[0, :, cols], preferred_element_type=F32)
        gt = jnp.dot(h_ext, w_ref[0, :, CONV_WIDTH + t * DOT_TN:CONV_WIDTH + (t + 1) * DOT_TN],
                     preferred_element_type=F32)
        u = a * _sigmoid(gt)
        u_ref[0:CONV_HALO, cols] = jnp.where(ib > 0, u[0:CONV_HALO], 0.0)
        u_ref[CONV_HALO:CONV_HALO + IN_TM, cols] = u[CONV_HALO:CONV_HALO + IN_TM]
        u_ref[CONV_HALO + IN_TM:, cols] = jnp.where(ib < tiles_per_batch - 1, u[CONV_HALO + IN_TM:], 0.0)

    def shift_copies(half):
        for s in range(SUBLANES):
            us_ref[s] = u_ref[half * CONV_HALF + s:half * CONV_HALF + s + CONV_SHIFT_ROWS, :]

    def conv_chunk(half, c, zero):
        acc = jnp.concatenate([zero] * (CONV_RC // SUBLANES), axis=0)
        for k in range(CONV_KSIZE):
            off = CONV_FIRST + k
            r0 = c * CONV_RC + off // SUBLANES * SUBLANES
            acc = acc + us_ref[off % SUBLANES, r0:r0 + CONV_RC, :] * cw_ref[k:k + 1, :]
        y = acc + cb_ref[...]
        mu = jnp.mean(y, axis=-1, keepdims=True)
        yc = y - mu
        yn = yc * lax.rsqrt(jnp.mean(yc * yc, axis=-1, keepdims=True) + LN_EPS)
        yn = yn * lg_ref[...] + lb_ref[...]
        act = yn * _sigmoid(yn)
        rows = slice(half * CONV_HALF + c * CONV_RC, half * CONV_HALF + (c + 1) * CONV_RC)
        c_ref[rows, :] = act.astype(BF16)
        return _derived_zero(y[CONV_RC - SUBLANES:])

    def project(col, width):
        return jnp.dot(h, w_ref[0, :, col:col + width], preferred_element_type=F32)

    def mem_queries():
        for t in range(MEM_WIDTH // DOT_TN):
            acc = project(QM_COL + t * DOT_TN, DOT_TN)
            qm_ref[:, t * DOT_TN:(t + 1) * DOT_TN] = acc.astype(BF16)
        return acc

    q_scale = HEAD_DIM ** -0.5 * LOG2E

    def qkv_group(g):
        d, out_ref = DILATIONS[g], (q0_ref, q1_ref, q2_ref)[g]
        for which in range(3):
            acc = project(QKV_COL + (which * N_GROUPS + g) * ATT_OUT, ATT_OUT)
            if which == 0:
                acc = acc * q_scale
            cols = slice(which * ATT_OUT, (which + 1) * ATT_OUT)
            if d == 1:
                out_ref[0, 0, :, cols] = acc.astype(BF16)
                continue
            for half in range(ATT_OUT // LANES):
                acc_ref[2 * which + half] = acc[:, half * LANES:(half + 1) * LANES]
        if d > 1:
            for r in range(d):
                for cb in range(QKV_WIDTH // LANES):
                    out_ref[0, r, :, cb * LANES:(cb + 1) * LANES] = (
                        acc_ref[cb, pl.ds(r, IN_TM // d, stride=d), :].astype(BF16))
        return acc

    def gate_tile(t):
        for sub in range(IN_TN // DOT_TN):
            cols = slice(t * IN_TN + sub * DOT_TN, t * IN_TN + (sub + 1) * DOT_TN)
            acc = project(GATE_COL + cols.start, DOT_TN)
            gate_ref[:, cols] = _sigmoid(acc + b_ref[:, cols]).astype(BF16)
        return acc

    mxu_units = ([mem_queries] + [functools.partial(qkv_group, g) for g in range(N_GROUPS)]
                 + [functools.partial(gate_tile, t) for t in range(N_GATE_TILES)])
    chunks = [(half, c) for half in range(IN_TM // CONV_HALF) for c in range(CONV_HALF // CONV_RC)]
    chunks_per_unit = len(chunks) // len(mxu_units)
    assert chunks_per_unit * len(mxu_units) == len(chunks)
    zero = jnp.zeros((SUBLANES, CONV_WIDTH), F32)
    for step, unit in enumerate(mxu_units):
        for half, c in chunks[step * chunks_per_unit:(step + 1) * chunks_per_unit]:
            if c == 0:
                shift_copies(half)
            zero = conv_chunk(half, c, zero)
        acc = unit()
        tail = _derived_zero(acc[IN_TM - SUBLANES:, acc.shape[1] - LANES:])
        zero = zero + jnp.concatenate([tail] * (CONV_WIDTH // LANES), axis=1)


def _in_proj(x2, g, w, layer, b, conv_w, conv_b, ln_g, ln_b, batch, seq):
    T = x2.shape[0]
    tiles_per_batch = seq // IN_TM
    halo_per_tile = IN_TM // CONV_HALO
    last_halo = T // CONV_HALO - 1

    def qkv_spec(d):
        return pl.BlockSpec((1, d, IN_TM // d, QKV_WIDTH),
                            lambda i: (i // tiles_per_batch, 0, i % tiles_per_batch, 0))

    def resident(shape):
        return pl.BlockSpec(shape, lambda i: (0, 0), pipeline_mode=pl.Buffered(1))

    def rows(width):
        return pl.BlockSpec((IN_TM, width), lambda i: (i, 0))

    return pl.pallas_call(
        functools.partial(_in_proj_kernel, tiles_per_batch=tiles_per_batch),
        grid=(T // IN_TM,),
        in_specs=[
            rows(D_MODEL),
            pl.BlockSpec((CONV_HALO, D_MODEL), lambda i: (jnp.maximum(i * halo_per_tile - 1, 0), 0)),
            pl.BlockSpec((CONV_HALO, D_MODEL), lambda i: (jnp.minimum((i + 1) * halo_per_tile, last_halo), 0)),
            resident((1, D_MODEL)),
            _layer_weight(w, layer),
            resident((1, GATE_WIDTH)),
            resident((CONV_KSIZE, CONV_WIDTH)), resident((1, CONV_WIDTH)),
            resident((1, CONV_WIDTH)), resident((1, CONV_WIDTH)),
        ],
        out_specs=[rows(CONV_WIDTH), rows(MEM_WIDTH),
                   qkv_spec(DILATIONS[0]), qkv_spec(DILATIONS[1]), qkv_spec(DILATIONS[2]),
                   rows(GATE_WIDTH)],
        out_shape=[jax.ShapeDtypeStruct((T, CONV_WIDTH), BF16), jax.ShapeDtypeStruct((T, MEM_WIDTH), BF16)]
        + [jax.ShapeDtypeStruct((batch, d, seq // d, QKV_WIDTH), BF16) for d in DILATIONS]
        + [jax.ShapeDtypeStruct((T, GATE_WIDTH), BF16)],
        scratch_shapes=[pltpu.VMEM((QKV_WIDTH // LANES, IN_TM, LANES), F32),
                        pltpu.VMEM((IN_EXT, CONV_WIDTH), F32),
                        pltpu.VMEM((SUBLANES, CONV_SHIFT_ROWS, CONV_WIDTH), F32)],
        compiler_params=pltpu.CompilerParams(
            dimension_semantics=("parallel",), vmem_limit_bytes=VMEM_LIMIT),
        name="in_proj",
    )(x2, x2, x2, g, w, b, conv_w, conv_b, ln_g, ln_b)


ATT_TILE = 2048
ATT_COMBINE_ROWS = 1024


def _att_kernel(c0_ref, p0_ref, n0_ref, c1_ref, p1_ref, n1_ref, c2_ref, p2_ref, n2_ref,
                b0_ref, b1_ref, b2_ref, *refs):
    n_cast = (len(refs) - 4) // 2
    o_ref = refs[n_cast]
    onat_ref, lnat_ref, edge_ref = refs[2 * n_cast + 1:]
    for src_ref, dst_ref in zip(refs[:n_cast], refs[n_cast + 1:2 * n_cast + 1]):
        dst_ref[...] = src_ref[...].astype(BF16)
    i = pl.program_id(1)
    n = pl.num_programs(1)
    stat_of_lane = lax.broadcasted_iota(jnp.int32, (1, LSE_LANES), 1) // STAT_LANES
    key_col = lax.broadcasted_iota(jnp.int32, (1, SUBK), 1)
    K0, V0 = ATT_OUT, 2 * ATT_OUT
    even_head_lanes = lax.broadcasted_iota(jnp.int32, (1, LANES), 1) < HEAD_DIM

    before_start = jnp.where(key_col >= jnp.where(i == 0, RADIUS, 0), 0.0, NEG_INF)
    past_end = jnp.where(key_col < jnp.where(i == n - 1, SUBK - RADIUS, SUBK), 0.0, NEG_INF)
    for g, (d, b_ref) in enumerate(zip(DILATIONS, (b0_ref, b1_ref, b2_ref))):
        if ATT_TILE // d == SUBQ:
            edge_ref[2 * g] = b_ref[...] + before_start + past_end
        else:
            edge_ref[2 * g] = b_ref[...] + before_start
            edge_ref[2 * g + 1] = b_ref[...] + past_end

    def subtile(g, d, q, k, v, bias, row0):
        zeros = jnp.zeros((SUBQ, LANES), BF16)
        blocks = []
        for h in range(HEADS):
            tile = q[:, h // 2 * LANES:(h // 2 + 1) * LANES]
            tile = jnp.where(even_head_lanes, tile, zeros) if h % 2 == 0 else jnp.where(even_head_lanes, zeros, tile)
            blocks.append(jnp.concatenate([tile, zeros] if h < 2 else [zeros, tile], axis=1))
        qs = jnp.concatenate(blocks, axis=0)
        s = lax.dot_general(qs, k, (((1,), (1,)), ((), ())), preferred_element_type=F32) + bias
        m = jnp.max(s, axis=-1, keepdims=True)
        e = jnp.exp2(s - m)
        l = jnp.sum(e, axis=-1, keepdims=True)
        o_all = jnp.dot(e.astype(BF16), v, preferred_element_type=F32)
        o = jnp.concatenate(
            [jnp.where(even_head_lanes,
                       o_all[2 * t * SUBQ:(2 * t + 1) * SUBQ, t * LANES:(t + 1) * LANES],
                       o_all[(2 * t + 1) * SUBQ:(2 * t + 2) * SUBQ, t * LANES:(t + 1) * LANES])
             for t in range(ATT_OUT // LANES)], axis=1)
        stats = jnp.broadcast_to(l[(HEADS - 1) * SUBQ:], (SUBQ, LSE_LANES))
        stats = jnp.where(stat_of_lane == 2 * (HEADS - 1), m[(HEADS - 1) * SUBQ:], stats)
        for h in range(HEADS - 1):
            rows = slice(h * SUBQ, (h + 1) * SUBQ)
            stats = jnp.where(stat_of_lane == 2 * h, m[rows], stats)
            stats = jnp.where(stat_of_lane == 2 * h + 1, l[rows], stats)
        rows = pl.ds(row0, SUBQ) if d == 1 else pl.ds(row0, SUBQ, stride=d)
        for half in range(ATT_OUT // LANES):
            onat_ref[2 * g + half, rows, :] = o[:, half * LANES:(half + 1) * LANES]
        lnat_ref[g, rows, :] = stats

    groups = ((c0_ref, p0_ref, n0_ref, b0_ref), (c1_ref, p1_ref, n1_ref, b1_ref),
              (c2_ref, p2_ref, n2_ref, b2_ref))
    for g, (d, (c_ref, p_ref, n_ref, bias_ref)) in enumerate(zip(DILATIONS, groups)):
        tq = ATT_TILE // d
        nsub = tq // SUBQ
        for r in range(d):
            for j in range(nsub):
                parts = [(c_ref, slice(max(j * SUBQ - RADIUS, 0), min(j * SUBQ + SUBQ + RADIUS, tq)))]
                if j == 0:
                    parts = [(p_ref, slice(None))] + parts
                if j == nsub - 1:
                    parts = parts + [(n_ref, slice(None))]
                k, v = [jnp.concatenate([ref[0, r, rows, col:col + ATT_OUT] for ref, rows in parts], axis=0)
                        for col in (K0, V0)]
                q = c_ref[0, r, j * SUBQ:(j + 1) * SUBQ, 0:ATT_OUT]
                if j == 0:
                    bias = edge_ref[2 * g]
                elif j == nsub - 1:
                    bias = edge_ref[2 * g + 1]
                else:
                    bias = bias_ref[...]
                subtile(g, d, q, k, v, bias, j * SUBQ * d + r)

    w_lane = lax.broadcasted_iota(jnp.int32, (2 * LSE_LANES, ATT_OUT), 0) % LSE_LANES
    o_lane = lax.broadcasted_iota(jnp.int32, (2 * LSE_LANES, ATT_OUT), 1)
    spread = (w_lane == o_lane // HEAD_DIM * LSE_PER_HEAD).astype(BF16)
    weight_lane = lax.broadcasted_iota(jnp.int32, (1, LSE_LANES), 1) % LSE_PER_HEAD == 0

    def expand(w):
        w = jnp.where(weight_lane, w, 0.0)
        hi = w.astype(BF16)
        lo = (w - hi.astype(F32)).astype(BF16)
        return jnp.dot(jnp.concatenate([hi, lo], axis=1), spread, preferred_element_type=F32)

    def combine(t, carry):
        rows = pl.ds(pl.multiple_of(t * ATT_COMBINE_ROWS, ATT_COMBINE_ROWS), ATT_COMBINE_ROWS)
        stats = [lnat_ref[g, rows, :] for g in range(N_GROUPS)]
        mx = jnp.maximum(jnp.maximum(stats[0], stats[1]), stats[2])
        a = [jnp.exp2(st - mx) for st in stats]
        sums = [pltpu.roll(st, LSE_LANES - STAT_LANES, 1) for st in stats]
        inv = 1.0 / (a[0] * sums[0] + a[1] * sums[1] + a[2] * sums[2])
        o = None
        for g in range(N_GROUPS):
            acc = jnp.concatenate([onat_ref[2 * g, rows, :], onat_ref[2 * g + 1, rows, :]], axis=1)
            term = expand(a[g] * inv) * acc
            o = term if o is None else o + term
        o_ref[0, rows, :] = o.astype(BF16)
        return carry

    lax.fori_loop(0, ATT_TILE // ATT_COMBINE_ROWS, combine, 0)


def _attention(qkv, biases, weights, layer):
    B = qkv[0].shape[0]
    S = qkv[0].shape[2]
    tiles = S // ATT_TILE
    steps = B * tiles

    def w_slice(w, lead):
        assert w.shape[1] % (steps * BF16_ROWS) == 0
        return pl.BlockSpec((1, w.shape[1] // steps, w.shape[2]), lambda b, i: (lead, b * tiles + i, 0))
    in_specs = []
    for d in DILATIONS:
        tq = ATT_TILE // d
        nb = tq // RADIUS
        last_b = S // d // RADIUS - 1
        in_specs += [
            pl.BlockSpec((1, d, tq, QKV_WIDTH), lambda b, i: (b, 0, i, 0)),
            pl.BlockSpec((1, d, RADIUS, QKV_WIDTH),
                         lambda b, i, nb=nb: (b, 0, jnp.maximum(i * nb - 1, 0), 0)),
            pl.BlockSpec((1, d, RADIUS, QKV_WIDTH),
                         lambda b, i, nb=nb, last_b=last_b: (b, 0, jnp.minimum((i + 1) * nb, last_b), 0)),
        ]
    in_specs += [pl.BlockSpec((HEADS * SUBQ, SUBK), lambda b, i: (0, 0))] * N_GROUPS
    in_specs += [w_slice(w, layer) for w in weights]
    args = []
    for a in qkv:
        args += [a, a, a]
    return pl.pallas_call(
        _att_kernel,
        grid=(B, tiles),
        in_specs=in_specs,
        out_specs=[pl.BlockSpec((1, ATT_TILE, ATT_OUT), lambda b, i: (b, i, 0))]
        + [w_slice(w, 0) for w in weights],
        out_shape=[jax.ShapeDtypeStruct((B, S, ATT_OUT), BF16)]
        + [jax.ShapeDtypeStruct((1,) + w.shape[1:], BF16) for w in weights],
        scratch_shapes=[pltpu.VMEM((N_GROUPS * (ATT_OUT // LANES), ATT_TILE, LANES), F32),
                        pltpu.VMEM((N_GROUPS, ATT_TILE, LSE_LANES), F32),
                        pltpu.VMEM((2 * N_GROUPS, HEADS * SUBQ, SUBK), F32)],
        compiler_params=pltpu.CompilerParams(
            dimension_semantics=("arbitrary", "arbitrary"), vmem_limit_bytes=VMEM_LIMIT),
        name="dilated_att",
    )(*args, *biases, *weights)


def _t5_bucket_np(rel):
    nb = NUM_BUCKETS // 2
    max_exact = nb // 2
    ret = np.where(rel > 0, nb, 0)
    n = np.abs(rel)
    nf = np.maximum(n, 1).astype(np.float32)
    ratio = np.log(nf / np.float32(max_exact)) / np.float32(math.log(MAX_DISTANCE / max_exact))
    large = max_exact + (ratio * np.float32(nb - max_exact)).astype(np.int32)
    large = np.minimum(large, nb - 1)
    return ret + np.where(n < max_exact, n, large)


def _band_bias(rel_bias, g, dilation):
    period = SUBQ + SUBK
    nband = 2 * RADIUS + 1
    bucket = _t5_bucket_np((np.arange(nband) - RADIUS) * dilation)
    onehot = np.zeros((period, NUM_BUCKETS), np.float32)
    onehot[np.arange(nband), bucket] = 1.0
    tab = rel_bias[:, g * HEADS:(g + 1) * HEADS].astype(F32)
    t = jnp.dot(jnp.asarray(onehot), tab, precision=lax.Precision.HIGHEST)
    t = jnp.where((np.arange(period) < nband)[:, None], t, NEG_INF).T
    skew = jnp.tile(t, (1, SUBQ))[:, :SUBQ * (period - 1)].reshape(HEADS, SUBQ, period - 1)
    return skew[:, :, :SUBK].reshape(HEADS * SUBQ, SUBK) * LOG2E


def _mem_kv_kernel(mem_ref, g_ref, w_ref, kv_ref):
    h = _rms(mem_ref[0], g_ref[...]).astype(BF16)
    kv_ref[0] = jnp.dot(h, w_ref[0], preferred_element_type=F32).astype(BF16)


def _mem_kv(mem, g, w, layer):
    B, M, _ = mem.shape
    return pl.pallas_call(
        _mem_kv_kernel,
        grid=(B,),
        in_specs=[pl.BlockSpec((1, M, D_MODEL), lambda b: (b, 0, 0)),
                  pl.BlockSpec((1, D_MODEL), lambda b: (0, 0)),
                  _layer_weight(w, layer)],
        out_specs=pl.BlockSpec((1, M, 2 * MEM_WIDTH), lambda b: (b, 0, 0)),
        out_shape=jax.ShapeDtypeStruct((B, M, 2 * MEM_WIDTH), BF16),
        compiler_params=pltpu.CompilerParams(
            dimension_semantics=("parallel",), vmem_limit_bytes=VMEM_LIMIT),
        name="mem_kv",
    )(mem, g, w)


MIX_TM = 1024
MIX_TN = 256
FFN_OUT_CAST_ROWS = 176


def _mix_kernel(x_ref, c_ref, qm_ref, kv_ref, oatt_ref, gate_ref, wc_ref, wa_ref, wm_ref, wo_ref,
                gpost_ref, wfi_ref, wfo_ref, out_ref, wfi_out_ref, wfo_out_ref):
    wfi_out_ref[...] = wfi_ref[...].astype(BF16)
    wfo_out_ref[...] = wfo_ref[...].astype(BF16)

    scale = MEM_HEAD_DIM ** -0.5
    heads = []
    for h in range(MEM_HEADS):
        lo = h * MEM_HEAD_DIM
        qh = qm_ref[:, lo:lo + MEM_HEAD_DIM]
        kh = kv_ref[0, :, lo:lo + MEM_HEAD_DIM]
        vh = kv_ref[0, :, MEM_WIDTH + lo:MEM_WIDTH + lo + MEM_HEAD_DIM]
        s = lax.dot_general(qh, kh, (((1,), (1,)), ((), ())), preferred_element_type=F32) * scale
        m = jnp.max(s, axis=-1, keepdims=True)
        e = jnp.exp(s - m)
        p = (e * (1.0 / jnp.sum(e, axis=-1, keepdims=True))).astype(BF16)
        heads.append(jnp.dot(p, vh, preferred_element_type=F32).astype(BF16))
    o_mem = jnp.concatenate(heads, axis=-1)
    slabs = []
    for t in range(D_MODEL // MIX_TN):
        cols = slice(t * MIX_TN, (t + 1) * MIX_TN)

        def gate(branch):
            return gate_ref[:, branch * D_MODEL + t * MIX_TN:branch * D_MODEL + (t + 1) * MIX_TN].astype(F32)

        slabs.append((gate(0) * jnp.dot(c_ref[...], wc_ref[0, :, cols], preferred_element_type=F32)
                      + gate(1) * jnp.dot(oatt_ref[...], wa_ref[0, :, cols], preferred_element_type=F32)
                      + gate(2) * jnp.dot(o_mem, wm_ref[0, :, cols], preferred_element_type=F32)).astype(BF16))
    y = jnp.dot(jnp.concatenate(slabs, axis=1), wo_ref[0], preferred_element_type=F32)
    out_ref[...] = x_ref[...] + _rms(y, gpost_ref[...])


def _mix(x2, c2, qm, kv, o_att, gates, wc, wa, wm, wo, layer, gpost, w_ffn_in, w_ffn_out, ffn_layer,
         tiles_per_batch):
    T = x2.shape[0]
    M = kv.shape[1]
    steps = T // MIX_TM
    in_rows = D_MODEL // steps
    out_steps = FFN_HIDDEN // FFN_OUT_CAST_ROWS
    assert in_rows % BF16_ROWS == 0 and FFN_OUT_CAST_ROWS % BF16_ROWS == 0 and out_steps <= steps

    def w_slice(rows_per_step, width, lead, nsteps):
        return pl.BlockSpec((1, rows_per_step, width), lambda i: (lead, jnp.minimum(i, nsteps - 1), 0))

    def rows(width):
        return pl.BlockSpec((MIX_TM, width), lambda i: (i, 0))

    def whole(shape):
        return pl.BlockSpec(shape, lambda i: (0,) * len(shape))

    return pl.pallas_call(
        _mix_kernel,
        grid=(T // MIX_TM,),
        in_specs=[rows(D_MODEL), rows(CONV_WIDTH), rows(MEM_WIDTH),
                  pl.BlockSpec((1, M, 2 * MEM_WIDTH), lambda i: (i // tiles_per_batch, 0, 0)),
                  rows(ATT_OUT), rows(GATE_WIDTH),
                  _layer_weight(wc, layer), _layer_weight(wa, layer),
                  _layer_weight(wm, layer), _layer_weight(wo, layer), whole((1, D_MODEL)),
                  w_slice(in_rows, 2 * FFN_HIDDEN, ffn_layer, steps),
                  w_slice(FFN_OUT_CAST_ROWS, D_MODEL, ffn_layer, out_steps)],
        out_specs=[rows(D_MODEL),
                   w_slice(in_rows, 2 * FFN_HIDDEN, 0, steps),
                   w_slice(FFN_OUT_CAST_ROWS, D_MODEL, 0, out_steps)],
        out_shape=[jax.ShapeDtypeStruct((T, D_MODEL), F32),
                   jax.ShapeDtypeStruct((1, D_MODEL, 2 * FFN_HIDDEN), BF16),
                   jax.ShapeDtypeStruct((1, FFN_HIDDEN, D_MODEL), BF16)],
        compiler_params=pltpu.CompilerParams(
            dimension_semantics=("arbitrary",), vmem_limit_bytes=VMEM_LIMIT),
        name="mix_out",
    )(x2, c2, qm, kv, o_att, gates, wc, wa, wm, wo, gpost, w_ffn_in, w_ffn_out)


FFN_TM = 1024
FFN_TH = 256
FFN_CHUNKS = FFN_HIDDEN // FFN_TH


def _ffn_kernel(x_ref, gpre_ref, win_ref, wout_ref, gpost_ref, *refs):
    out_ref = refs[0] if len(refs) == 1 else refs[1]
    if len(refs) == 3:
        refs[2][...] = refs[0][...].astype(BF16)
    x = x_ref[...]
    h = _rms(x, gpre_ref[...]).astype(BF16)
    acts = []
    for c in range(FFN_CHUNKS):
        lo = c * FFN_TH
        gv = jnp.dot(h, win_ref[0, :, lo:lo + FFN_TH], preferred_element_type=F32)
        uv = jnp.dot(h, win_ref[0, :, FFN_HIDDEN + lo:FFN_HIDDEN + lo + FFN_TH], preferred_element_type=F32)
        acts.append((gv * _sigmoid(gv) * uv).astype(BF16))
    acc = jnp.dot(jnp.concatenate(acts, axis=1), wout_ref[0], preferred_element_type=F32)
    out_ref[...] = x + _rms(acc, gpost_ref[...])


def _ffn(x2, gpre, w_in, w_out, layer, gpost, next_w=None):
    T = x2.shape[0]
    steps = T // FFN_TM
    cast_in, cast_out, cast_shape, cast_args = [], [], [], []
    if next_w is not None:
        w, lead = next_w
        assert w.shape[1] % (steps * BF16_ROWS) == 0
        cast_in = [pl.BlockSpec((1, w.shape[1] // steps, w.shape[2]), lambda i: (lead, i, 0))]
        cast_out = [pl.BlockSpec((1, w.shape[1] // steps, w.shape[2]), lambda i: (0, i, 0))]
        cast_shape = [jax.ShapeDtypeStruct((1,) + w.shape[1:], BF16)]
        cast_args = [w]

    def resident(shape):
        return pl.BlockSpec(shape, lambda i: (0, 0), pipeline_mode=pl.Buffered(1))

    outs = pl.pallas_call(
        _ffn_kernel,
        grid=(steps,),
        in_specs=[pl.BlockSpec((FFN_TM, D_MODEL), lambda i: (i, 0)),
                  resident((1, D_MODEL)),
                  _layer_weight(w_in, layer), _layer_weight(w_out, layer),
                  resident((1, D_MODEL))] + cast_in,
        out_specs=[pl.BlockSpec((FFN_TM, D_MODEL), lambda i: (i, 0))] + cast_out,
        out_shape=[jax.ShapeDtypeStruct((T, D_MODEL), F32)] + cast_shape,
        compiler_params=pltpu.CompilerParams(
            dimension_semantics=("arbitrary",), vmem_limit_bytes=VMEM_LIMIT),
        name="ffn",
    )(x2, gpre, w_in, w_out, gpost, *cast_args)
    return outs[0], (outs[1] if next_w is not None else None)


def kernel(x, mem, rel_bias, norm_mix_pre, w_in, b_gate, conv_dw, conv_dw_bias, conv_ln_g, conv_ln_b,
           w_conv_out, w_att_out, norm_mem, w_mem_kv, w_mem_out, w_out, norm_mix_post, norm_ffn_pre,
           w_ffn_in, w_ffn_out, norm_ffn_post):
    B, S, D = x.shape
    depth = w_in.shape[0]
    T = B * S
    biases = [_band_bias(rel_bias, g, d) for g, d in enumerate(DILATIONS)]

    def row(v):
        return v.reshape(1, -1)

    w_in_bf16 = w_in[:1].astype(BF16)
    x2 = x.reshape(T, D)
    for l in range(depth):
        c, qm, qkv0, qkv1, qkv2, gates = _in_proj(
            x2, row(norm_mix_pre[l]), w_in_bf16, 0, row(b_gate[l]),
            conv_dw[l], row(conv_dw_bias[l]), row(conv_ln_g[l]), row(conv_ln_b[l]), B, S)
        o_att, wkv, wc, wa, wm, wo = _attention(
            (qkv0, qkv1, qkv2), biases, (w_mem_kv, w_conv_out, w_att_out, w_mem_out, w_out), l)
        kv = _mem_kv(mem, row(norm_mem[l]), wkv, 0)
        x2, wfi, wfo = _mix(x2, c, qm, kv, o_att.reshape(T, ATT_OUT), gates, wc, wa, wm, wo, 0,
                            row(norm_mix_post[l]), w_ffn_in, w_ffn_out, l, S // MIX_TM)
        next_w_in = (w_in, l + 1) if l + 1 < depth else None
        x2, w_in_bf16 = _ffn(x2, row(norm_ffn_pre[l]), wfi, wfo, 0, row(norm_ffn_post[l]), next_w_in)
    return x2.reshape(B, S, D)
```
